```python
import math
import jax, jax.numpy as jnp
from jax import lax
import numpy as np

D_MODEL = 1024
BATCH = 8
SEQ = 2048
DEPTH = 4

MIX_W = 768
XATT_HEADS = 4
XATT_DH = 64
XATT_W = XATT_HEADS * XATT_DH
MEM_LEN = 256
CONV_K = 3
ML_HEADS = 6
ML_DH = MIX_W // ML_HEADS
CHUNK = 128
D_IN_CONV = 3 * MIX_W + XATT_W
D_IN_MLSTM = 4 * MIX_W + 2 * ML_HEADS + XATT_W
N_CONV = (DEPTH + 1) // 2
N_MLSTM = DEPTH // 2
N_EXPERTS = 32
N_GROUPS = 8
EXPERTS_PER_GROUP = N_EXPERTS // N_GROUPS
TOP_K = 2
D_FF = 512
MOE_BLOCK = 128
ALPHA = (2.0 * DEPTH) ** 0.25
BETA = (8.0 * DEPTH) ** -0.25
LN_EPS = 1e-5

kernel_name = "hybrid_conv_mlstm_xattn_moe_deepnorm"


def layer_norm(x, g, b):
    xf = x.astype(jnp.float32)
    mu = jnp.mean(xf, axis=-1, keepdims=True)
    var = jnp.mean(jnp.square(xf - mu), axis=-1, keepdims=True)
    return ((xf - mu) * lax.rsqrt(var + LN_EPS) * g + b).astype(x.dtype)


def conv_mixer(x, w_in, conv_w):
    proj = x @ w_in
    bg = proj[..., :MIX_W]
    cg = proj[..., MIX_W:2 * MIX_W]
    xin = proj[..., 2 * MIX_W:3 * MIX_W]
    q_mem = proj[..., 3 * MIX_W:]
    u = cg * xin
    seq = u.shape[1]
    up = jnp.pad(u, ((0, 0), (CONV_K - 1, 0), (0, 0)))
    c = (conv_w[0] * up[:, 0:seq] + conv_w[1] * up[:, 1:seq + 1] + conv_w[2] * up[:, 2:seq + 2])
    return bg * c, q_mem


def mlstm_cell(q, k, v, log_i, log_f):
    nb, nh, seq, dh = q.shape
    nc = seq // CHUNK
    q = q.reshape(nb, nh, nc, CHUNK, dh)
    k = k.reshape(nb, nh, nc, CHUNK, dh)
    v = v.reshape(nb, nh, nc, CHUNK, dh)
    log_i = log_i.reshape(nb, nh, nc, CHUNK)
    log_f = log_f.reshape(nb, nh, nc, CHUNK)
    b = jnp.cumsum(log_f, axis=-1)
    g = b[..., -1]
    a = g[..., None] - b + log_i

    def step(carry, inp):
        c_st, n_st, m_st = carry
        kc, vc, ac, gc = inp
        m_new = jnp.maximum(gc + m_st, ac.max(-1))
        decay = jnp.exp(gc + m_st - m_new)
        w = jnp.exp(ac - m_new[..., None])
        c_new = decay[..., None, None] * c_st + jnp.einsum('bhl,bhld,bhle->bhde', w, kc, vc)
        n_new = decay[..., None] * n_st + jnp.einsum('bhl,bhld->bhd', w, kc)
        return (c_new, n_new, m_new), (c_st, n_st, m_st)

    init = (jnp.zeros((nb, nh, dh, dh), jnp.float32),
            jnp.zeros((nb, nh, dh), jnp.float32),
            jnp.zeros((nb, nh), jnp.float32))
    xs = (jnp.moveaxis(k, 2, 0), jnp.moveaxis(v, 2, 0), jnp.moveaxis(a, 2, 0), jnp.moveaxis(g, 2, 0))
    _, (c_prev, n_prev, m_prev) = lax.scan(step, init, xs)
    c_prev = jnp.moveaxis(c_prev, 0, 2)
    n_prev = jnp.moveaxis(n_prev, 0, 2)
    m_prev = jnp.moveaxis(m_prev, 0, 2)

    causal = jnp.tril(jnp.ones((CHUNK, CHUNK), dtype=bool))
    d = jnp.where(causal, b[..., :, None] - b[..., None, :] + log_i[..., None, :], -jnp.inf)
    inter = b + m_prev[..., None]
    m_t = jnp.maximum(d.max(-1), inter)
    s = jnp.einsum('bhcld,bhcsd->bhcls', q, k) * jnp.exp(d - m_t[..., None])
    w_inter = jnp.exp(inter - m_t)
    num = (jnp.einsum('bhcls,bhcsd->bhcld', s, v)
           + w_inter[..., None] * jnp.einsum('bhcld,bhcde->bhcle', q, c_prev))
    den = s.sum(-1) + w_inter * jnp.einsum('bhcld,bhcd->bhcl', q, n_prev)
    h = num / jnp.maximum(jnp.abs(den), jnp.exp(-m_t))[..., None]
    return h.reshape(nb, nh, seq, dh)


def mlstm_mixer(x, w_in, gate_b, norm_g):
    nb, seq, _ = x.shape
    proj = x @ w_in
    o0 = 4 * MIX_W

    def heads(t):
        return t.reshape(nb, seq, ML_HEADS, ML_DH).transpose(0, 2, 1, 3).astype(jnp.float32)

    q = heads(proj[..., 0:MIX_W])
    k = heads(proj[..., MIX_W:2 * MIX_W]) * (ML_DH ** -0.5)
    v = heads(proj[..., 2 * MIX_W:3 * MIX_W])
    o_gate = proj[..., 3 * MIX_W:4 * MIX_W]
    log_i = (proj[..., o0:o0 + ML_HEADS] + gate_b[:ML_HEADS]).astype(jnp.float32).transpose(0, 2, 1)
    log_f = jax.nn.log_sigmoid((proj[..., o0 + ML_HEADS:o0 + 2 * ML_HEADS]
                                + gate_b[ML_HEADS:]).astype(jnp.float32)).transpose(0, 2, 1)
    q_mem = proj[..., o0 + 2 * ML_HEADS:]
    h = mlstm_cell(q, k, v, log_i, log_f)
    mu = jnp.mean(h, axis=-1, keepdims=True)
    var = jnp.mean(jnp.square(h - mu), axis=-1, keepdims=True)
    h = ((h - mu) * lax.rsqrt(var + LN_EPS)).transpose(0, 2, 1, 3).reshape(nb, seq, MIX_W)
    h = (h * norm_g).astype(x.dtype)
    return jax.nn.sigmoid(o_gate) * h, q_mem


def cross_attention(q_mem, mem, w_kv):
    nb, seq, _ = q_mem.shape
    kv = mem @ w_kv
    k = kv[..., :XATT_W].reshape(nb, -1, XATT_HEADS, XATT_DH)
    v = kv[..., XATT_W:].reshape(nb, -1, XATT_HEADS, XATT_DH)
    q = q_mem.reshape(nb, seq, XATT_HEADS, XATT_DH)
    s = jnp.einsum('bshd,bmhd->bhsm', q, k).astype(jnp.float32) * (XATT_DH ** -0.5)
    p = jax.nn.softmax(s, axis=-1).astype(v.dtype)
    return jnp.einsum('bhsm,bmhd->bshd', p, v).reshape(nb, seq, XATT_W)


def grouped_moe(x2, router_w, router_b, w_gate, w_up, w_down):
    n_tok, d = x2.shape
    scores = jax.nn.sigmoid((x2 @ router_w).astype(jnp.float32))
    sel = (scores + router_b).reshape(n_tok, N_GROUPS, EXPERTS_PER_GROUP)
    group_score = lax.top_k(sel, TOP_K)[0].sum(-1)
    _, g_idx = lax.top_k(group_score, 1)
    in_group = jnp.take_along_axis(sel, g_idx[:, :, None], axis=1)[:, 0]
    _, local = lax.top_k(in_group, TOP_K)
    expert = g_idx * EXPERTS_PER_GROUP + local
    gates = jnp.take_along_axis(scores, expert, axis=1)
    gates = gates / gates.sum(-1, keepdims=True)

    n_asg = n_tok * TOP_K
    e_flat = expert.reshape(n_asg)
    tok_flat = jnp.repeat(jnp.arange(n_tok, dtype=jnp.int32), TOP_K)
    g_flat = gates.reshape(n_asg)
    counts = jax.ops.segment_sum(jnp.ones((n_asg,), jnp.int32), e_flat, num_segments=N_EXPERTS)
    padded = (counts + MOE_BLOCK - 1) // MOE_BLOCK * MOE_BLOCK
    pad_end = jnp.cumsum(padded)
    pad_start = pad_end - padded
    start = jnp.cumsum(counts) - counts
    order = jnp.argsort(e_flat)
    e_sorted = e_flat[order]
    dest = pad_start[e_sorted] + (jnp.arange(n_asg, dtype=jnp.int32) - start[e_sorted])
    n_rows = ((n_asg + MOE_BLOCK - 1) // MOE_BLOCK + N_EXPERTS) * MOE_BLOCK
    n_blocks = n_rows // MOE_BLOCK
    tok_buf = jnp.zeros((n_rows,), jnp.int32).at[dest].set(tok_flat[order])
    gate_buf = jnp.zeros((n_rows,), x2.dtype).at[dest].set(g_flat[order].astype(x2.dtype))
    block_start = jnp.arange(n_blocks, dtype=jnp.int32) * MOE_BLOCK
    block_expert = jnp.clip(jnp.searchsorted(pad_end, block_start, side='right'), 0, N_EXPERTS - 1)
    xb = x2[tok_buf].reshape(n_blocks, MOE_BLOCK, d)

    def expert_block(args):
        xblk, e = args
        h = jax.nn.silu(xblk @ w_gate[e]) * (xblk @ w_up[e])
        return h @ w_down[e]

    yb = lax.map(expert_block, (xb, block_expert)).reshape(n_rows, d)
    return jnp.zeros_like(x2).at[tok_buf].add(yb * gate_buf[:, None])


def setup_inputs(seed: int = 0) -> dict:
    key = jax.random.key(seed)
    ks = jax.random.split(key, 20)
    f32 = jnp.float32
    nrm = lambda k, s, sc: jax.random.normal(k, s, f32) * sc
    gate_i = nrm(ks[5], (N_MLSTM, ML_HEADS), 0.1)
    gate_f = jax.random.uniform(ks[6], (N_MLSTM, ML_HEADS), f32, 3.0, 6.0)
    return {
        "x": nrm(ks[0], (BATCH, SEQ, D_MODEL), 1.0),
        "mem": nrm(ks[1], (BATCH, MEM_LEN, D_MODEL), 1.0),
        "conv_w_in": nrm(ks[2], (N_CONV, D_MODEL, D_IN_CONV), D_MODEL ** -0.5),
        "conv_w": nrm(ks[3], (N_CONV, CONV_K, MIX_W), CONV_K ** -0.5),
        "mlstm_w_in": nrm(ks[4], (N_MLSTM, D_MODEL, D_IN_MLSTM), D_MODEL ** -0.5),
        "mlstm_gate_b": jnp.concatenate([gate_i, gate_f], axis=-1),
        "mlstm_norm_g": 1.0 + nrm(ks[7], (N_MLSTM, MIX_W), 0.02),
        "w_kv_mem": nrm(ks[8], (DEPTH, D_MODEL, 2 * XATT_W), D_MODEL ** -0.5),
        "w_out": nrm(ks[9], (DEPTH, MIX_W + XATT_W, D_MODEL), (MIX_W + XATT_W) ** -0.5 * BETA),
        "ln1_g": 1.0 + nrm(ks[10], (DEPTH, D_MODEL), 0.02),
        "ln1_b": nrm(ks[11], (DEPTH, D_MODEL), 0.02),
        "ln2_g": 1.0 + nrm(ks[12], (DEPTH, D_MODEL), 0.02),
        "ln2_b": nrm(ks[13], (DEPTH, D_MODEL), 0.02),
        "router_w": nrm(ks[14], (D_MODEL, N_EXPERTS), D_MODEL ** -0.5),
        "router_b": nrm(ks[15], (N_EXPERTS,), 0.01),
        "w_gate": nrm(ks[16], (DEPTH, N_EXPERTS, D_MODEL, D_FF), D_MODEL ** -0.5),
        "w_up": nrm(ks[17], (DEPTH, N_EXPERTS, D_MODEL, D_FF), D_MODEL ** -0.5),
        "w_down": nrm(ks[18], (DEPTH, N_EXPERTS, D_FF, D_MODEL), D_FF ** -0.5 * BETA),
    }


def reference(x, mem, conv_w_in, conv_w, mlstm_w_in, mlstm_gate_b, mlstm_norm_g, w_kv_mem, w_out,
              ln1_g, ln1_b, ln2_g, ln2_b, router_w, router_b, w_gate, w_up, w_down):
    nb, seq, d = x.shape
    for i in range(DEPTH):
        j = i // 2
        if i % 2 == 0:
            h_loc, q_mem = conv_mixer(x, conv_w_in[j], conv_w[j])
        else:
            h_loc, q_mem = mlstm_mixer(x, mlstm_w_in[j], mlstm_gate_b[j], mlstm_norm_g[j])
        h_mem = cross_attention(q_mem, mem, w_kv_mem[i])
        y = jnp.concatenate([h_loc, h_mem], axis=-1) @ w_out[i]
        x = layer_norm(ALPHA * x + y, ln1_g[i], ln1_b[i])
        m = grouped_moe(x.reshape(nb * seq, d), router_w, router_b,
                        w_gate[i], w_up[i], w_down[i]).reshape(nb, seq, d)
        x = layer_norm(ALPHA * x + m, ln2_g[i], ln2_b[i])
    return x
```

```python
import functools

import jax
import jax.numpy as jnp
from jax import lax
from jax.experimental import pallas as pl
from jax.experimental.pallas import tpu as pltpu

F32 = jnp.float32
BF16 = jnp.bfloat16

D_MODEL = 1024
DEPTH = 4
MIX_W = 768
XATT_HEADS = 4
XATT_DH = 64
XATT_W = XATT_HEADS * XATT_DH
CONV_K = 3
ML_HEADS = 6
ML_DH = MIX_W // ML_HEADS
CHUNK = 128
N_EXPERTS = 32
N_GROUPS = 8
EXPERTS_PER_GROUP = N_EXPERTS // N_GROUPS
TOP_K = 2
D_FF = 512
ALPHA = (2.0 * DEPTH) ** 0.25
LN_EPS = 1e-5

LANES = 128
GATE_PAD = LANES
MOE_ROWS = 128
PAIR_LO = (0, 0, 1, 1, 2, 0)
PAIR_HI = (1, 2, 2, 3, 3, 3)
N_PAIRS = len(PAIR_LO)
N_CLASSES = N_GROUPS * N_PAIRS
VMEM_LIMIT = 52 * 1024 * 1024

TM_CONV = 512
TM_MLSTM = 256


def _layer_norm(z, g, b):
    mu = jnp.mean(z, axis=-1, keepdims=True)
    zc = z - mu
    var = jnp.mean(zc * zc, axis=-1, keepdims=True)
    return zc * lax.rsqrt(var + LN_EPS) * g + b


def _dot(a, b):
    return jnp.dot(a, b, preferred_element_type=F32)


def _dot_nt(a, b):
    return lax.dot_general(a, b, (((1,), (1,)), ((), ())), preferred_element_type=F32)


def _matmul_kernel(a_ref, b_ref, o_ref):
    o_ref[...] = _dot(a_ref[...].astype(BF16), b_ref[...])


def _matmul(a, b_bf16, tm, tn):
    m, k = a.shape
    n = b_bf16.shape[1]
    return pl.pallas_call(
        _matmul_kernel,
        grid=(m // tm, n // tn),
        in_specs=[pl.BlockSpec((tm, k), lambda i, j: (i, 0)),
                  pl.BlockSpec((k, tn), lambda i, j: (0, j))],
        out_specs=pl.BlockSpec((tm, tn), lambda i, j: (i, j)),
        out_shape=jax.ShapeDtypeStruct((m, n), F32),
        compiler_params=pltpu.CompilerParams(
            dimension_semantics=("arbitrary", "arbitrary"), vmem_limit_bytes=VMEM_LIMIT),
        name="kv_proj",
    )(a, b_bf16)


def _mixer_tail(x, h_loc, q_mem, kbd_ref, vbd_ref, wout_ref, g_ref, b_ref, rwt_ref,
                x1_ref, lg_ref):
    s = _dot(q_mem.astype(BF16), kbd_ref[0])
    mem_len = s.shape[1] // XATT_HEADS
    ps = []
    for h in range(XATT_HEADS):
        sh = s[:, h * mem_len:(h + 1) * mem_len]
        e = jnp.exp(sh - jnp.max(sh, axis=-1, keepdims=True))
        ps.append((e / jnp.sum(e, axis=-1, keepdims=True)).astype(BF16))
    h_mem = _dot(jnp.concatenate(ps, axis=-1), vbd_ref[0])
    hcat = jnp.concatenate([h_loc.astype(BF16), h_mem.astype(BF16)], axis=-1)
    y = _dot(hcat, wout_ref[...])
    x1 = _layer_norm(ALPHA * x + y, g_ref[...], b_ref[...])
    x1_ref[...] = x1
    lg_ref[...] = _dot_nt(rwt_ref[...], x1.astype(BF16))


def _conv_half_kernel(x_ref, win_ref, cw_ref, kbd_ref, vbd_ref, wout_ref, g_ref, b_ref, rwt_ref,
                      x1_ref, lg_ref, carry_ref):
    tm = x_ref.shape[0]

    @pl.when(pl.program_id(1) == 0)
    def _():
        carry_ref[...] = jnp.zeros_like(carry_ref)

    x = x_ref[...]
    proj = _dot(x.astype(BF16), win_ref[...])
    bg = proj[:, 0:MIX_W]
    cg = proj[:, MIX_W:2 * MIX_W]
    xin = proj[:, 2 * MIX_W:3 * MIX_W]
    q_mem = proj[:, 3 * MIX_W:]
    u = cg * xin
    prev = carry_ref[...]
    row = lax.broadcasted_iota(jnp.int32, u.shape, 0)
    u1 = jnp.where(row == 0, prev[7:8], pltpu.roll(u, 1, 0))
    u2 = jnp.where(row == 0, prev[6:7],
                   jnp.where(row == 1, prev[7:8], pltpu.roll(u, 2, 0)))
    carry_ref[...] = u[tm - 8:tm]
    cw = cw_ref[...]
    c = cw[0:1] * u2 + cw[1:2] * u1 + cw[2:3] * u
    _mixer_tail(x, bg * c, q_mem, kbd_ref, vbd_ref, wout_ref, g_ref, b_ref, rwt_ref,
                x1_ref, lg_ref)


def _log_sigmoid(z):
    return -(jnp.maximum(-z, 0.0) + jnp.log1p(jnp.exp(-jnp.abs(z))))


def _mlstm_half_kernel(x_ref, win_ref, gb_ref, ng_ref, kbd_ref, vbd_ref, wout_ref, g_ref, b_ref,
                       rwt_ref, x1_ref, lg_ref, state_ref, m_ref, h_ref):
    tm = x_ref.shape[0]

    @pl.when(pl.program_id(1) == 0)
    def _():
        state_ref[...] = jnp.zeros_like(state_ref)
        m_ref[...] = jnp.zeros_like(m_ref)

    x = x_ref[...]
    proj = _dot(x.astype(BF16), win_ref[...])
    g0 = 4 * MIX_W + XATT_W
    gates = proj[:, g0:g0 + GATE_PAD] + gb_ref[...]
    logf = _log_sigmoid(gates)
    row = lax.broadcasted_iota(jnp.int32, (CHUNK, GATE_PAD), 0)
    lane = lax.broadcasted_iota(jnp.int32, (CHUNK, GATE_PAD), 1)
    causal = (lax.broadcasted_iota(jnp.int32, (CHUNK, CHUNK), 0)
              >= lax.broadcasted_iota(jnp.int32, (CHUNK, CHUNK), 1))
    ones_col = (lax.broadcasted_iota(jnp.int32, (CHUNK, ML_DH), 1) == 0).astype(BF16)

    for c in range(tm // CHUNK):
        r0 = c * CHUNK
        bc = logf[r0:r0 + CHUNK]
        sh = 1
        while sh < CHUNK:
            bc = bc + jnp.where(row >= sh, pltpu.roll(bc, sh, 0), 0.0)
            sh *= 2
        mct = jnp.where(lane < ML_HEADS, gates[r0:r0 + CHUNK], bc).T
        for h in range(ML_HEADS):
            q = proj[r0:r0 + CHUNK, h * ML_DH:(h + 1) * ML_DH]
            k = proj[r0:r0 + CHUNK, MIX_W + h * ML_DH:MIX_W + (h + 1) * ML_DH] * (ML_DH ** -0.5)
            v = proj[r0:r0 + CHUNK, 2 * MIX_W + h * ML_DH:2 * MIX_W + (h + 1) * ML_DH]
            qb, kb, vb = q.astype(BF16), k.astype(BF16), v.astype(BF16)
            b_col = bc[:, ML_HEADS + h:ML_HEADS + h + 1]
            b_row = mct[ML_HEADS + h:ML_HEADS + h + 1, :]
            li_row = mct[h:h + 1, :]
            m_prev = m_ref[h][:, 0:1]
            state = state_ref[h]

            d = jnp.where(causal, b_col - b_row + li_row, -jnp.inf)
            inter = b_col + m_prev
            m_t = jnp.maximum(jnp.max(d, axis=-1, keepdims=True), inter)
            s = _dot_nt(qb, kb) * jnp.exp(d - m_t)
            w_inter = jnp.exp(inter - m_t)
            qs = _dot(qb, state.astype(BF16))
            num = _dot(s.astype(BF16), vb) + w_inter * qs[:, 0:ML_DH]
            den = jnp.sum(s, axis=-1, keepdims=True) + w_inter * qs[:, ML_DH:ML_DH + 1]
            hh = num / jnp.maximum(jnp.abs(den), jnp.exp(-m_t))
            mu = jnp.mean(hh, axis=-1, keepdims=True)
            hc = hh - mu
            var = jnp.mean(hc * hc, axis=-1, keepdims=True)
            h_ref[r0:r0 + CHUNK, h * ML_DH:(h + 1) * ML_DH] = hc * lax.rsqrt(var + LN_EPS)

            g = b_row[:, CHUNK - 1:CHUNK]
            a_row = g - b_row + li_row
            m_new = jnp.maximum(g + m_prev, jnp.max(a_row, axis=-1, keepdims=True))
            decay = jnp.exp(g + m_prev - m_new)
            w_row = jnp.exp(a_row - m_new)
            ktw = (k.T * w_row).astype(BF16)
            v_aug = jnp.concatenate([vb, ones_col], axis=-1)
            state_ref[h] = decay * state + _dot(ktw, v_aug)
            m_ref[h] = jnp.broadcast_to(m_new, (1, LANES))

    o_gate = proj[:, 3 * MIX_W:4 * MIX_W]
    h_loc = jax.nn.sigmoid(o_gate) * (h_ref[...] * ng_ref[...])
    q_mem = proj[:, 4 * MIX_W:4 * MIX_W + XATT_W]
    _mixer_tail(x, h_loc, q_mem, kbd_ref, vbd_ref, wout_ref, g_ref, b_ref, rwt_ref,
                x1_ref, lg_ref)


def _mixer_half(kernel_fn, tm, x, nb, seq, mixer_args, mixer_specs, kbd, vbd, wout, ln_g, ln_b,
                rwt, scratch, name):
    d = D_MODEL
    n_tok = nb * seq
    spb = seq // tm
    const2 = lambda b, s: (0, 0)
    in_specs = ([pl.BlockSpec((tm, d), lambda b, s: (b * spb + s, 0))] + mixer_specs + [
        pl.BlockSpec((1,) + kbd.shape[1:], lambda b, s: (b, 0, 0)),
        pl.BlockSpec((1,) + vbd.shape[1:], lambda b, s: (b, 0, 0)),
        pl.BlockSpec(wout.shape, const2),
        pl.BlockSpec((1, d), const2),
        pl.BlockSpec((1, d), const2),
        pl.BlockSpec(rwt.shape, const2),
    ])
    return pl.pallas_call(
        kernel_fn,
        grid=(nb, spb),
        in_specs=in_specs,
        out_specs=[pl.BlockSpec((tm, d), lambda b, s: (b * spb + s, 0)),
                   pl.BlockSpec((N_EXPERTS, tm), lambda b, s: (0, b * spb + s))],
        out_shape=[jax.ShapeDtypeStruct((n_tok, d), F32),
                   jax.ShapeDtypeStruct((N_EXPERTS, n_tok), F32)],
        scratch_shapes=scratch,
        compiler_params=pltpu.CompilerParams(
            dimension_semantics=("arbitrary", "arbitrary"), vmem_limit_bytes=VMEM_LIMIT),
        name=name,
    )(x, *mixer_args, kbd, vbd, wout, ln_g.reshape(1, d), ln_b.reshape(1, d), rwt)


def _moe_kernel(tok_ref, dst_ref, e1_ref, e2_ref, nblk_ref,
                x_hbm, g1_ref, g2_ref, wg1_ref, wu1_ref, wd1_ref, wg2_ref, wu2_ref, wd2_ref,
                lng_ref, lnb_ref, out_hbm,
                xbuf, obuf, zbuf, bg1, bu1, bd1, bg2, bu2, bd2, gsem, ssem, zsem):
    b = pl.program_id(0)
    nblk = nblk_ref[0]
    slot = b % 2

    def issue_gather(blk, slt):
        def body(r, carry):
            t = tok_ref[blk * MOE_ROWS + r]
            pltpu.make_async_copy(x_hbm.at[pl.ds(t, 1)], xbuf.at[slt, pl.ds(r, 1)],
                                  gsem.at[slt]).start()
            return carry
        lax.fori_loop(0, MOE_ROWS, body, 0)

    def issue_scatter(blk, slt):
        def body(r, carry):
            t = dst_ref[blk * MOE_ROWS + r]
            pltpu.make_async_copy(obuf.at[slt, pl.ds(r, 1)], out_hbm.at[pl.ds(t, 1)],
                                  ssem.at[slt]).start()
            return carry
        lax.fori_loop(0, MOE_ROWS, body, 0)

    def wait_gather(slt):
        pltpu.make_async_copy(x_hbm.at[pl.ds(0, MOE_ROWS)], xbuf.at[slt], gsem.at[slt]).wait()

    def wait_scatter(slt):
        pltpu.make_async_copy(obuf.at[slt], out_hbm.at[pl.ds(0, MOE_ROWS)], ssem.at[slt]).wait()

    @pl.when(b < nblk)
    def _():
        @pl.when(b == 0)
        def _():
            issue_gather(0, 0)

        @pl.when(b + 1 < nblk)
        def _():
            issue_gather(b + 1, 1 - slot)

        prev = jnp.maximum(b - 1, 0)

        @pl.when((b == 0) | (e1_ref[b] != e1_ref[prev]))
        def _():
            bg1[...] = wg1_ref[0].astype(BF16)
            bu1[...] = wu1_ref[0].astype(BF16)
            bd1[...] = wd1_ref[0].astype(BF16)

        @pl.when((b == 0) | (e2_ref[b] != e2_ref[prev]))
        def _():
            bg2[...] = wg2_ref[0].astype(BF16)
            bu2[...] = wu2_ref[0].astype(BF16)
            bd2[...] = wd2_ref[0].astype(BF16)

        wait_gather(slot)
        x = xbuf[slot]
        xb = x.astype(BF16)

        def ffn(wg, wu, wd):
            hg = _dot(xb, wg[...])
            hu = _dot(xb, wu[...])
            return _dot((hg * jax.nn.sigmoid(hg) * hu).astype(BF16), wd[...])

        m = g1_ref[...] * ffn(bg1, bu1, bd1) + g2_ref[...] * ffn(bg2, bu2, bd2)
        o = _layer_norm(ALPHA * x + m, lng_ref[...], lnb_ref[...])

        @pl.when(b >= 2)
        def _():
            wait_scatter(slot)

        obuf[slot] = o
        issue_scatter(b, slot)

    @pl.when(b >= nblk)
    def _():
        zbuf[...] = jnp.zeros_like(zbuf)
        fill = pltpu.make_async_copy(zbuf, out_hbm.at[pl.ds(b * MOE_ROWS, MOE_ROWS)], zsem)
        fill.start()
        fill.wait()

    @pl.when(b == pl.num_programs(0) - 1)
    def _():
        @pl.when(nblk >= 1)
        def _():
            wait_scatter((nblk - 1) % 2)

        @pl.when(nblk >= 2)
        def _():
            wait_scatter(nblk % 2)


def _moe_half(x1, row_tok, row_dst, blk_e1, blk_e2, nblk, g1_rows, g2_rows,
              w_gate, w_up, w_down, ln_g, ln_b):
    n_rows = row_tok.shape[0]
    n_blocks = n_rows // MOE_ROWS
    d, f = D_MODEL, D_FF
    w_in_spec1 = pl.BlockSpec((1, d, f), lambda b, tok, dst, e1, e2, nb: (e1[b], 0, 0))
    w_in_spec2 = pl.BlockSpec((1, d, f), lambda b, tok, dst, e1, e2, nb: (e2[b], 0, 0))
    w_dn_spec1 = pl.BlockSpec((1, f, d), lambda b, tok, dst, e1, e2, nb: (e1[b], 0, 0))
    w_dn_spec2 = pl.BlockSpec((1, f, d), lambda b, tok, dst, e1, e2, nb: (e2[b], 0, 0))
    row_spec = pl.BlockSpec((MOE_ROWS, 1), lambda b, *_: (b, 0))
    const_spec = pl.BlockSpec((1, d), lambda b, *_: (0, 0))
    grid_spec = pltpu.PrefetchScalarGridSpec(
        num_scalar_prefetch=5,
        grid=(n_blocks,),
        in_specs=[pl.BlockSpec(memory_space=pl.ANY), row_spec, row_spec,
                  w_in_spec1, w_in_spec1, w_dn_spec1, w_in_spec2, w_in_spec2, w_dn_spec2,
                  const_spec, const_spec],
        out_specs=pl.BlockSpec(memory_space=pl.ANY),
        scratch_shapes=[
            pltpu.VMEM((2, MOE_ROWS, d), F32),
            pltpu.VMEM((2, MOE_ROWS, d), F32),
            pltpu.VMEM((MOE_ROWS, d), F32),
            pltpu.VMEM((d, f), BF16), pltpu.VMEM((d, f), BF16), pltpu.VMEM((f, d), BF16),
            pltpu.VMEM((d, f), BF16), pltpu.VMEM((d, f), BF16), pltpu.VMEM((f, d), BF16),
            pltpu.SemaphoreType.DMA((2,)),
            pltpu.SemaphoreType.DMA((2,)),
            pltpu.SemaphoreType.DMA(()),
        ],
    )
    return pl.pallas_call(
        _moe_kernel,
        grid_spec=grid_spec,
        out_shape=jax.ShapeDtypeStruct((n_rows, d), F32),
        compiler_params=pltpu.CompilerParams(
            dimension_semantics=("arbitrary",), vmem_limit_bytes=VMEM_LIMIT),
        name="moe_half",
    )(row_tok, row_dst, blk_e1, blk_e2, nblk,
      x1, g1_rows, g2_rows, w_gate, w_up, w_down, w_gate, w_up, w_down,
      ln_g.reshape(1, d), ln_b.reshape(1, d))


def _route(logits_t, router_b, n_tok):
    scores = jax.nn.sigmoid(logits_t).T
    sel = (scores + router_b).reshape(n_tok, N_GROUPS, EXPERTS_PER_GROUP)
    group_score = lax.top_k(sel, TOP_K)[0].sum(-1)
    g_idx = jnp.argmax(group_score, axis=-1).astype(jnp.int32)
    in_group = jnp.take_along_axis(sel, g_idx[:, None, None], axis=1)[:, 0]
    _, local = lax.top_k(in_group, TOP_K)
    expert = g_idx[:, None] * EXPERTS_PER_GROUP + local
    gates = jnp.take_along_axis(scores, expert, axis=1)
    gates = gates / gates.sum(-1, keepdims=True)
    first_lo = local[:, 0] < local[:, 1]
    lo = jnp.where(first_lo, local[:, 0], local[:, 1])
    hi = jnp.where(first_lo, local[:, 1], local[:, 0])
    g_lo = jnp.where(first_lo, gates[:, 0], gates[:, 1])
    g_hi = jnp.where(first_lo, gates[:, 1], gates[:, 0])
    pair_lo = jnp.array(PAIR_LO, jnp.int32)
    pair_hi = jnp.array(PAIR_HI, jnp.int32)
    pair = jnp.argmax((lo[:, None] == pair_lo[None]) & (hi[:, None] == pair_hi[None]), axis=-1)
    cls = g_idx * N_PAIRS + pair.astype(jnp.int32)

    n_rows = n_tok + N_CLASSES * MOE_ROWS
    n_blocks = n_rows // MOE_ROWS
    onehot = (cls[:, None] == jnp.arange(N_CLASSES, dtype=jnp.int32)[None]).astype(jnp.int32)
    csum = jnp.cumsum(onehot, axis=0)
    counts = csum[-1]
    rank = jnp.take_along_axis(csum, cls[:, None], axis=1)[:, 0] - 1
    padded = (counts + MOE_ROWS - 1) // MOE_ROWS * MOE_ROWS
    pad_end = jnp.cumsum(padded)
    dest = (pad_end - padded)[cls] + rank
    tok_ids = jnp.arange(n_tok, dtype=jnp.int32)
    row_tok = jnp.zeros((n_rows,), jnp.int32).at[dest].set(tok_ids)
    valid = jnp.zeros((n_rows,), jnp.int32).at[dest].set(1)
    pad_rank = jnp.cumsum(1 - valid) - 1
    row_dst = jnp.where(valid == 1, row_tok, n_tok + pad_rank).astype(jnp.int32)
    g1_rows = jnp.zeros((n_rows,), F32).at[dest].set(g_lo).reshape(n_rows, 1)
    g2_rows = jnp.zeros((n_rows,), F32).at[dest].set(g_hi).reshape(n_rows, 1)
    nblk = (pad_end[-1] // MOE_ROWS).astype(jnp.int32)
    blk_ids = jnp.arange(n_blocks, dtype=jnp.int32)
    blk_cls = jnp.clip(jnp.searchsorted(pad_end, blk_ids * MOE_ROWS, side='right'),
                       0, N_CLASSES - 1).astype(jnp.int32)
    blk_cls = jnp.where(blk_ids < nblk, blk_cls, blk_cls[jnp.maximum(nblk - 1, 0)])
    blk_e1 = (blk_cls // N_PAIRS) * EXPERTS_PER_GROUP + pair_lo[blk_cls % N_PAIRS]
    blk_e2 = (blk_cls // N_PAIRS) * EXPERTS_PER_GROUP + pair_hi[blk_cls % N_PAIRS]
    return row_tok, row_dst, blk_e1, blk_e2, nblk.reshape(1), g1_rows, g2_rows


def kernel(x, mem, conv_w_in, conv_w, mlstm_w_in, mlstm_gate_b, mlstm_norm_g, w_kv_mem, w_out,
           ln1_g, ln1_b, ln2_g, ln2_b, router_w, router_b, w_gate, w_up, w_down):
    nb, seq, d = x.shape
    n_tok = nb * seq
    mem_len = mem.shape[1]
    assert d == D_MODEL and seq % TM_CONV == 0 and seq % TM_MLSTM == 0 and TM_MLSTM % CHUNK == 0

    w_kv_all = jnp.transpose(w_kv_mem, (1, 0, 2)).reshape(d, DEPTH * 2 * XATT_W).astype(BF16)
    kv = _matmul(mem.reshape(nb * mem_len, d), w_kv_all, 512, 1024)
    kv = kv.reshape(nb, mem_len, DEPTH, 2, XATT_HEADS, XATT_DH)
    eye = jnp.eye(XATT_HEADS, dtype=bool)

    rwt = router_w.T.astype(BF16)
    xf = x.reshape(n_tok, d)
    for i in range(DEPTH):
        j = i // 2
        k = kv[:, :, i, 0] * (XATT_DH ** -0.5)
        v = kv[:, :, i, 1]
        kbd = jnp.where(eye[None, :, None, :, None],
                        jnp.transpose(k, (0, 2, 3, 1))[:, :, :, None, :], 0.0)
        kbd = kbd.reshape(nb, XATT_W, XATT_HEADS * mem_len).astype(BF16)
        vbd = jnp.where(eye[None, :, None, :, None],
                        jnp.transpose(v, (0, 2, 1, 3))[:, :, :, None, :], 0.0)
        vbd = vbd.reshape(nb, XATT_HEADS * mem_len, XATT_W).astype(BF16)
        wout = w_out[i].astype(BF16)
        const2 = lambda b, s: (0, 0)
        if i % 2 == 0:
            win = conv_w_in[j].astype(BF16)
            x1, logits_t = _mixer_half(
                _conv_half_kernel, TM_CONV, xf, nb, seq,
                [win, conv_w[j]],
                [pl.BlockSpec(win.shape, const2), pl.BlockSpec(conv_w[j].shape, const2)],
                kbd, vbd, wout, ln1_g[i], ln1_b[i], rwt,
                [pltpu.VMEM((8, MIX_W), F32)], "conv_half")
        else:
            w = mlstm_w_in[j]
            o0 = 4 * MIX_W
            w_gates = jnp.pad(w[:, o0:o0 + 2 * ML_HEADS], ((0, 0), (0, GATE_PAD - 2 * ML_HEADS)))
            win = jnp.concatenate([w[:, :o0], w[:, o0 + 2 * ML_HEADS:], w_gates], axis=1).astype(BF16)
            gate_b = jnp.pad(mlstm_gate_b[j], (0, GATE_PAD - 2 * ML_HEADS)).reshape(1, GATE_PAD)
            norm_g = mlstm_norm_g[j].reshape(1, MIX_W)
            x1, logits_t = _mixer_half(
                _mlstm_half_kernel, TM_MLSTM, xf, nb, seq,
                [win, gate_b, norm_g],
                [pl.BlockSpec(win.shape, const2), pl.BlockSpec(gate_b.shape, const2),
                 pl.BlockSpec(norm_g.shape, const2)],
                kbd, vbd, wout, ln1_g[i], ln1_b[i], rwt,
                [pltpu.VMEM((ML_HEADS, ML_DH, 2 * ML_DH), F32),
                 pltpu.VMEM((ML_HEADS, 1, LANES), F32),
                 pltpu.VMEM((TM_MLSTM, MIX_W), F32)], "mlstm_half")
        route = _route(logits_t, router_b, n_tok)
        xf = _moe_half(x1, *route, w_gate[i], w_up[i], w_down[i], ln2_g[i], ln2_b[i])
    return xf[:n_tok].reshape(nb, seq, d)
```

```python
import jax
import jax.numpy as jnp
from jax import lax
from jax.experimental import pallas as pl
from jax.experimental.pallas import tpu as pltpu

F32 = jnp.float32
BF16 = jnp.bfloat16
I32 = jnp.int32

D_MODEL = 1024
DEPTH = 4
MIX_W = 768
XATT_HEADS = 4
XATT_DH = 64
XATT_W = XATT_HEADS * XATT_DH
CONV_K = 3
ML_HEADS = 6
ML_DH = MIX_W // ML_HEADS
CHUNK = 128
N_EXPERTS = 32
N_GROUPS = 8
EXPERTS_PER_GROUP = N_EXPERTS // N_GROUPS
D_FF = 512
ALPHA = (2.0 * DEPTH) ** 0.25
LN_EPS = 1e-5

LANES = 128
SUBLANES = 8
GATE_PAD = LANES
MOE_ROWS = 128
PAIR_LO = (0, 0, 1, 1, 2, 0)
PAIR_HI = (1, 2, 2, 3, 3, 3)
N_PAIRS = len(PAIR_LO)
N_CLASSES = N_GROUPS * N_PAIRS
CLASS_PAD = 64
VMEM_LIMIT = 52 * 1024 * 1024

TM_CONV = 512
TM_MLSTM = 256


def _layer_norm(z, g, b):
    mu = jnp.mean(z, axis=-1, keepdims=True)
    zc = z - mu
    var = jnp.mean(zc * zc, axis=-1, keepdims=True)
    return zc * lax.rsqrt(var + LN_EPS) * g + b


def _dot(a, b):
    return jnp.dot(a, b, preferred_element_type=F32)


def _dot_nt(a, b):
    return lax.dot_general(a, b, (((1,), (1,)), ((), ())), preferred_element_type=F32)


def _matmul_kernel(a_ref, b_ref, o_ref):
    o_ref[...] = _dot(a_ref[...].astype(BF16), b_ref[...])


def _matmul(a, b_bf16, tm, tn):
    m, k = a.shape
    n = b_bf16.shape[1]
    return pl.pallas_call(
        _matmul_kernel,
        grid=(m // tm, n // tn),
        in_specs=[pl.BlockSpec((tm, k), lambda i, j: (i, 0)),
                  pl.BlockSpec((k, tn), lambda i, j: (0, j))],
        out_specs=pl.BlockSpec((tm, tn), lambda i, j: (i, j)),
        out_shape=jax.ShapeDtypeStruct((m, n), F32),
        compiler_params=pltpu.CompilerParams(
            dimension_semantics=("arbitrary", "arbitrary"), vmem_limit_bytes=VMEM_LIMIT),
        name="kv_proj",
    )(a, b_bf16)


def _route_tile(x1b, rwt_ref, rb_ref, tri_ref, base_ref, rf_ref, ri_ref, cnt_ref):
    tm = x1b.shape[0]
    scores = jax.nn.sigmoid(_dot_nt(rwt_ref[...], x1b))
    sel = scores + rb_ref[...]
    s = [sel[SUBLANES * j:SUBLANES * (j + 1)] for j in range(EXPERTS_PER_GROUP)]
    p = [scores[SUBLANES * j:SUBLANES * (j + 1)] for j in range(EXPERTS_PER_GROUP)]
    hi1, lo1 = jnp.maximum(s[0], s[1]), jnp.minimum(s[0], s[1])
    hi2, lo2 = jnp.maximum(s[2], s[3]), jnp.minimum(s[2], s[3])
    gs = jnp.maximum(hi1, hi2) + jnp.maximum(jnp.minimum(hi1, hi2), jnp.maximum(lo1, lo2))
    sub = lax.broadcasted_iota(I32, (N_GROUPS, tm), 0)
    gi = jnp.min(jnp.where(gs == jnp.max(gs, axis=0, keepdims=True), sub, N_GROUPS),
                 axis=0, keepdims=True)
    gm = sub == gi
    v = [jnp.sum(jnp.where(gm, sj, 0.0), axis=0, keepdims=True) for sj in s]
    pv = [jnp.sum(jnp.where(gm, pj, 0.0), axis=0, keepdims=True) for pj in p]
    best, i1 = v[0], jnp.zeros((1, tm), I32)
    for j in range(1, EXPERTS_PER_GROUP):
        u = v[j] > best
        best, i1 = jnp.where(u, v[j], best), jnp.where(u, j, i1)
    best2, i2 = jnp.full((1, tm), -jnp.inf, F32), jnp.zeros((1, tm), I32)
    for j in range(EXPERTS_PER_GROUP):
        cand = jnp.where(i1 == j, -jnp.inf, v[j])
        u = cand > best2
        best2, i2 = jnp.where(u, cand, best2), jnp.where(u, j, i2)

    def pick(idx):
        return jnp.where(idx == 0, pv[0], jnp.where(idx == 1, pv[1],
                                                    jnp.where(idx == 2, pv[2], pv[3])))

    p1, p2 = pick(i1), pick(i2)
    tot = p1 + p2
    first_lo = i1 < i2
    g_lo = jnp.where(first_lo, p1, p2) / tot
    g_hi = jnp.where(first_lo, p2, p1) / tot
    lo, hi = jnp.minimum(i1, i2), jnp.maximum(i1, i2)
    pair = jnp.where(hi == 1, 0, jnp.where(hi == 2, jnp.where(lo == 0, 1, 2),
                                           jnp.where(lo == 1, 3, jnp.where(lo == 2, 4, 5))))
    cls = gi * N_PAIRS + pair

    oh = lax.broadcasted_iota(I32, (CLASS_PAD, tm), 0) == cls
    pref = _dot(oh.astype(F32).astype(BF16), tri_ref[...])
    base = base_ref[...]
    rank = jnp.sum(jnp.where(oh, pref + base[:, 0:1] - 1.0, 0.0), axis=0, keepdims=True)
    base = base + pref[:, tm - 1:tm]
    base_ref[...] = base
    cnt_ref[...] = base

    sub8 = lax.broadcasted_iota(I32, (SUBLANES, tm), 0)
    rf_ref[...] = jnp.where(sub8 == 0, g_lo, jnp.where(sub8 == 1, g_hi, 0.0))
    ri_ref[...] = jnp.where(sub8 == 0, cls, jnp.where(sub8 == 1, rank.astype(I32), 0))


def _mixer_tail(x, h_loc, q_mem, tail_refs):
    (kbd_ref, vbd_ref, wout_ref, g_ref, b_ref, rwt_ref, rb_ref, tri_ref,
     x1_ref, rf_ref, ri_ref, cnt_ref, base_ref) = tail_refs

    @pl.when((pl.program_id(0) == 0) & (pl.program_id(1) == 0))
    def _():
        base_ref[...] = jnp.zeros_like(base_ref)

    s = _dot(q_mem.astype(BF16), kbd_ref[0])
    mem_len = s.shape[1] // XATT_HEADS
    ps = []
    for h in range(XATT_HEADS):
        sh = s[:, h * mem_len:(h + 1) * mem_len]
        e = jnp.exp(sh - jnp.max(sh, axis=-1, keepdims=True))
        ps.append((e / jnp.sum(e, axis=-1, keepdims=True)).astype(BF16))
    h_mem = _dot(jnp.concatenate(ps, axis=-1), vbd_ref[0])
    hcat = jnp.concatenate([h_loc.astype(BF16), h_mem.astype(BF16)], axis=-1)
    y = _dot(hcat, wout_ref[...])
    x1 = _layer_norm(ALPHA * x + y, g_ref[...], b_ref[...])
    x1_ref[...] = x1
    _route_tile(x1.astype(BF16), rwt_ref, rb_ref, tri_ref, base_ref, rf_ref, ri_ref, cnt_ref)


def _conv_half_kernel(x_ref, win_ref, cw_ref, *refs):
    tail_refs, carry_ref = refs[:-1], refs[-1]
    tm = x_ref.shape[0]

    @pl.when(pl.program_id(1) == 0)
    def _():
        carry_ref[...] = jnp.zeros_like(carry_ref)

    x = x_ref[...]
    proj = _dot(x.astype(BF16), win_ref[...])
    bg = proj[:, 0:MIX_W]
    cg = proj[:, MIX_W:2 * MIX_W]
    xin = proj[:, 2 * MIX_W:3 * MIX_W]
    q_mem = proj[:, 3 * MIX_W:]
    u = cg * xin
    prev = carry_ref[...]
    row = lax.broadcasted_iota(I32, u.shape, 0)
    u1 = jnp.where(row == 0, prev[7:8], pltpu.roll(u, 1, 0))
    u2 = jnp.where(row == 0, prev[6:7],
                   jnp.where(row == 1, prev[7:8], pltpu.roll(u, 2, 0)))
    carry_ref[...] = u[tm - 8:tm]
    cw = cw_ref[...]
    c = cw[0:1] * u2 + cw[1:2] * u1 + cw[2:3] * u
    _mixer_tail(x, bg * c, q_mem, tail_refs)


def _log_sigmoid(z):
    return -(jnp.maximum(-z, 0.0) + jnp.log1p(jnp.exp(-jnp.abs(z))))


def _mlstm_half_kernel(x_ref, win_ref, gb_ref, ng_ref, *refs):
    tail_refs, (state_ref, m_ref, h_ref) = refs[:-3], refs[-3:]
    tm = x_ref.shape[0]

    @pl.when(pl.program_id(1) == 0)
    def _():
        state_ref[...] = jnp.zeros_like(state_ref)
        m_ref[...] = jnp.zeros_like(m_ref)

    x = x_ref[...]
    proj = _dot(x.astype(BF16), win_ref[...])
    g0 = 4 * MIX_W + XATT_W
    gates = proj[:, g0:g0 + GATE_PAD] + gb_ref[...]
    logf = _log_sigmoid(gates)
    row = lax.broadcasted_iota(I32, (CHUNK, GATE_PAD), 0)
    lane = lax.broadcasted_iota(I32, (CHUNK, GATE_PAD), 1)
    causal = (lax.broadcasted_iota(I32, (CHUNK, CHUNK), 0)
              >= lax.broadcasted_iota(I32, (CHUNK, CHUNK), 1))
    ones_col = (lax.broadcasted_iota(I32, (CHUNK, ML_DH), 1) == 0).astype(BF16)

    for c in range(tm // CHUNK):
        r0 = c * CHUNK
        bc = logf[r0:r0 + CHUNK]
        sh = 1
        while sh < CHUNK:
            bc = bc + jnp.where(row >= sh, pltpu.roll(bc, sh, 0), 0.0)
            sh *= 2
        mct = jnp.where(lane < ML_HEADS, gates[r0:r0 + CHUNK], bc).T
        for h in range(ML_HEADS):
            q = proj[r0:r0 + CHUNK, h * ML_DH:(h + 1) * ML_DH]
            k = proj[r0:r0 + CHUNK, MIX_W + h * ML_DH:MIX_W + (h + 1) * ML_DH] * (ML_DH ** -0.5)
            v = proj[r0:r0 + CHUNK, 2 * MIX_W + h * ML_DH:2 * MIX_W + (h + 1) * ML_DH]
            qb, kb, vb = q.astype(BF16), k.astype(BF16), v.astype(BF16)
            b_col = bc[:, ML_HEADS + h:ML_HEADS + h + 1]
            b_row = mct[ML_HEADS + h:ML_HEADS + h + 1, :]
            li_row = mct[h:h + 1, :]
            m_prev = m_ref[h][:, 0:1]
            state = state_ref[h]

            d = jnp.where(causal, b_col - b_row + li_row, -jnp.inf)
            inter = b_col + m_prev
            m_t = jnp.maximum(jnp.max(d, axis=-1, keepdims=True), inter)
            s = _dot_nt(qb, kb) * jnp.exp(d - m_t)
            w_inter = jnp.exp(inter - m_t)
            qs = _dot(qb, state.astype(BF16))
            num = _dot(s.astype(BF16), vb) + w_inter * qs[:, 0:ML_DH]
            den = jnp.sum(s, axis=-1, keepdims=True) + w_inter * qs[:, ML_DH:ML_DH + 1]
            hh = num / jnp.maximum(jnp.abs(den), jnp.exp(-m_t))
            mu = jnp.mean(hh, axis=-1, keepdims=True)
            hc = hh - mu
            var = jnp.mean(hc * hc, axis=-1, keepdims=True)
            h_ref[r0:r0 + CHUNK, h * ML_DH:(h + 1) * ML_DH] = hc * lax.rsqrt(var + LN_EPS)

            g = b_row[:, CHUNK - 1:CHUNK]
            a_row = g - b_row + li_row
            m_new = jnp.maximum(g + m_prev, jnp.max(a_row, axis=-1, keepdims=True))
            decay = jnp.exp(g + m_prev - m_new)
            w_row = jnp.exp(a_row - m_new)
            ktw = (k.T * w_row).astype(BF16)
            v_aug = jnp.concatenate([vb, ones_col], axis=-1)
            state_ref[h] = decay * state + _dot(ktw, v_aug)
            m_ref[h] = jnp.broadcast_to(m_new, (1, LANES))

    o_gate = proj[:, 3 * MIX_W:4 * MIX_W]
    h_loc = jax.nn.sigmoid(o_gate) * (h_ref[...] * ng_ref[...])
    q_mem = proj[:, 4 * MIX_W:4 * MIX_W + XATT_W]
    _mixer_tail(x, h_loc, q_mem, tail_refs)


def _mixer_half(kernel_fn, tm, x, nb, seq, mixer_args, kbd, vbd, wout, ln_g, ln_b, rwt, rb,
                scratch, name):
    d = D_MODEL
    n_tok = nb * seq
    spb = seq // tm
    const2 = lambda b, s: (0, 0)
    tri = jnp.triu(jnp.ones((tm, tm), BF16))
    tail_args = [kbd, vbd, wout, ln_g.reshape(1, d), ln_b.reshape(1, d), rwt, rb, tri]
    in_specs = ([pl.BlockSpec((tm, d), lambda b, s: (b * spb + s, 0))]
                + [pl.BlockSpec(a.shape, const2) for a in mixer_args]
                + [pl.BlockSpec((1,) + kbd.shape[1:], lambda b, s: (b, 0, 0)),
                   pl.BlockSpec((1,) + vbd.shape[1:], lambda b, s: (b, 0, 0))]
                + [pl.BlockSpec(a.shape, const2) for a in tail_args[2:]])
    tok_spec = lambda rows: pl.BlockSpec((rows, tm), lambda b, s: (0, b * spb + s))
    return pl.pallas_call(
        kernel_fn,
        grid=(nb, spb),
        in_specs=in_specs,
        out_specs=[pl.BlockSpec((tm, d), lambda b, s: (b * spb + s, 0)),
                   tok_spec(SUBLANES), tok_spec(SUBLANES),
                   pl.BlockSpec((CLASS_PAD, LANES), const2)],
        out_shape=[jax.ShapeDtypeStruct((n_tok, d), F32),
                   jax.ShapeDtypeStruct((SUBLANES, n_tok), F32),
                   jax.ShapeDtypeStruct((SUBLANES, n_tok), I32),
                   jax.ShapeDtypeStruct((CLASS_PAD, LANES), F32)],
        scratch_shapes=[pltpu.VMEM((CLASS_PAD, LANES), F32)] + scratch,
        compiler_params=pltpu.CompilerParams(
            dimension_semantics=("arbitrary", "arbitrary"), vmem_limit_bytes=VMEM_LIMIT),
        name=name,
    )(x, *mixer_args, *tail_args)


def _moe_kernel(order_ref, off_ref, nvalid_ref, trash_ref, e1_ref, e2_ref, nblk_ref,
                x_hbm, g_hbm, wg1_ref, wu1_ref, wd1_ref, wg2_ref, wu2_ref, wd2_ref,
                lng_ref, lnb_ref, out_hbm,
                xbuf, gbuf, obuf, zbuf, bg1, bu1, bd1, bg2, bu2, bd2, gsem, ssem, zsem):
    b = pl.program_id(0)
    nblk = nblk_ref[0]
    slot = b % 2

    def gate_copy(blk, slt):
        return pltpu.make_async_copy(g_hbm.at[pl.ds(off_ref[blk], MOE_ROWS)], gbuf.at[slt],
                                     gsem.at[slt])

    def issue_gather(blk, slt):
        base = off_ref[blk]

        def body(r, carry):
            t = order_ref[base + r]
            pltpu.make_async_copy(x_hbm.at[pl.ds(t, 1)], xbuf.at[slt, pl.ds(r, 1)],
                                  gsem.at[slt]).start()
            return carry
        lax.fori_loop(0, MOE_ROWS, body, 0)
        gate_copy(blk, slt).start()

    def wait_gather(slt):
        pltpu.make_async_copy(x_hbm.at[pl.ds(0, MOE_ROWS)], xbuf.at[slt], gsem.at[slt]).wait()
        gate_copy(0, slt).wait()

    def issue_scatter(blk, slt):
        base = off_ref[blk]
        nvalid = nvalid_ref[blk]
        trash = trash_ref[blk]

        def valid_row(r, carry):
            t = order_ref[base + r]
            pltpu.make_async_copy(obuf.at[slt, pl.ds(r, 1)], out_hbm.at[pl.ds(t, 1)],
                                  ssem.at[slt]).start()
            return carry

        def pad_row(r, carry):
            pltpu.make_async_copy(obuf.at[slt, pl.ds(r, 1)], out_hbm.at[pl.ds(trash + r, 1)],
                                  ssem.at[slt]).start()
            return carry
        lax.fori_loop(0, nvalid, valid_row, 0)
        lax.fori_loop(nvalid, MOE_ROWS, pad_row, 0)

    def wait_scatter(slt):
        pltpu.make_async_copy(obuf.at[slt], out_hbm.at[pl.ds(0, MOE_ROWS)], ssem.at[slt]).wait()

    @pl.when(b < nblk)
    def _():
        @pl.when(b == 0)
        def _():
            issue_gather(0, 0)

        @pl.when(b + 1 < nblk)
        def _():
            issue_gather(b + 1, 1 - slot)

        prev = jnp.maximum(b - 1, 0)

        @pl.when((b == 0) | (e1_ref[b] != e1_ref[prev]))
        def _():
            bg1[...] = wg1_ref[0, 0].astype(BF16)
            bu1[...] = wu1_ref[0, 0].astype(BF16)
            bd1[...] = wd1_ref[0, 0].astype(BF16)

        @pl.when((b == 0) | (e2_ref[b] != e2_ref[prev]))
        def _():
            bg2[...] = wg2_ref[0, 0].astype(BF16)
            bu2[...] = wu2_ref[0, 0].astype(BF16)
            bd2[...] = wd2_ref[0, 0].astype(BF16)

        wait_gather(slot)
        x = xbuf[slot]
        xb = x.astype(BF16)
        gates = gbuf[slot]

        def ffn(wg, wu, wd):
            hg = _dot(xb, wg[...])
            hu = _dot(xb, wu[...])
            return _dot((hg * jax.nn.sigmoid(hg) * hu).astype(BF16), wd[...])

        m = gates[:, 0:1] * ffn(bg1, bu1, bd1) + gates[:, 1:2] * ffn(bg2, bu2, bd2)
        o = _layer_norm(ALPHA * x + m, lng_ref[...], lnb_ref[...])

        @pl.when(b >= 2)
        def _():
            wait_scatter(slot)

        obuf[slot] = o
        issue_scatter(b, slot)

    @pl.when(b >= nblk)
    def _():
        zbuf[...] = jnp.zeros_like(zbuf)
        fill = pltpu.make_async_copy(zbuf, out_hbm.at[pl.ds(b * MOE_ROWS, MOE_ROWS)], zsem)
        fill.start()
        fill.wait()

    @pl.when(b == pl.num_programs(0) - 1)
    def _():
        @pl.when(nblk >= 1)
        def _():
            wait_scatter((nblk - 1) % 2)

        @pl.when(nblk >= 2)
        def _():
            wait_scatter(nblk % 2)


def _moe_half(layer, x1, plan, gates_sorted, w_gate, w_up, w_down, ln_g, ln_b):
    n_blocks = plan[1].shape[0]
    n_rows = n_blocks * MOE_ROWS
    d, f = D_MODEL, D_FF
    e1_map = lambda b, order, off, nv, tr, e1, e2, nb: (layer, e1[b], 0, 0)
    e2_map = lambda b, order, off, nv, tr, e1, e2, nb: (layer, e2[b], 0, 0)
    const_spec = pl.BlockSpec((1, d), lambda b, *_: (0, 0))
    any_spec = pl.BlockSpec(memory_space=pl.ANY)
    grid_spec = pltpu.PrefetchScalarGridSpec(
        num_scalar_prefetch=7,
        grid=(n_blocks,),
        in_specs=[any_spec, any_spec,
                  pl.BlockSpec((1, 1, d, f), e1_map), pl.BlockSpec((1, 1, d, f), e1_map),
                  pl.BlockSpec((1, 1, f, d), e1_map),
                  pl.BlockSpec((1, 1, d, f), e2_map), pl.BlockSpec((1, 1, d, f), e2_map),
                  pl.BlockSpec((1, 1, f, d), e2_map),
                  const_spec, const_spec],
        out_specs=any_spec,
        scratch_shapes=[
            pltpu.VMEM((2, MOE_ROWS, d), F32),
            pltpu.VMEM((2, MOE_ROWS, LANES), F32),
            pltpu.VMEM((2, MOE_ROWS, d), F32),
            pltpu.VMEM((MOE_ROWS, d), F32),
            pltpu.VMEM((d, f), BF16), pltpu.VMEM((d, f), BF16), pltpu.VMEM((f, d), BF16),
            pltpu.VMEM((d, f), BF16), pltpu.VMEM((d, f), BF16), pltpu.VMEM((f, d), BF16),
            pltpu.SemaphoreType.DMA((2,)),
            pltpu.SemaphoreType.DMA((2,)),
            pltpu.SemaphoreType.DMA(()),
        ],
    )
    return pl.pallas_call(
        _moe_kernel,
        grid_spec=grid_spec,
        out_shape=jax.ShapeDtypeStruct((n_rows, d), F32),
        compiler_params=pltpu.CompilerParams(
            dimension_semantics=("arbitrary",), vmem_limit_bytes=VMEM_LIMIT),
        name="moe_half",
    )(*plan, x1, gates_sorted, w_gate, w_up, w_down, w_gate, w_up, w_down,
      ln_g.reshape(1, d), ln_b.reshape(1, d))


def _plan_blocks(route_f, route_i, counts_f, n_tok):
    cls, rank = route_i[0], route_i[1]
    counts = counts_f[:N_CLASSES, 0].astype(I32)
    padded = (counts + MOE_ROWS - 1) // MOE_ROWS * MOE_ROWS
    pad_end = jnp.cumsum(padded)
    pad_start = pad_end - padded
    start = jnp.cumsum(counts) - counts
    class_ids = jnp.arange(N_CLASSES, dtype=I32)
    dest = jnp.sum(jnp.where(cls[:, None] == class_ids[None], pad_start[None], 0), axis=1) + rank
    _, order, g_lo, g_hi = lax.sort(
        (dest, jnp.arange(n_tok, dtype=I32), route_f[0], route_f[1]), num_keys=1)
    order = jnp.pad(order, (0, MOE_ROWS))
    gates_sorted = jnp.pad(jnp.stack([g_lo, g_hi], axis=1), ((0, MOE_ROWS), (0, LANES - 2)))

    n_blocks = n_tok // MOE_ROWS + N_CLASSES
    blk_ids = jnp.arange(n_blocks, dtype=I32)
    blk_start = blk_ids * MOE_ROWS
    nblk = pad_end[-1] // MOE_ROWS
    blk_cls = jnp.minimum(jnp.sum(pad_end[None] <= blk_start[:, None], axis=1), N_CLASSES - 1)
    blk_cls = jnp.where(blk_ids < nblk, blk_cls, blk_cls[jnp.maximum(nblk - 1, 0)]).astype(I32)
    in_cls = blk_start - pad_start[blk_cls]
    active = blk_ids < nblk
    blk_off = jnp.where(active, start[blk_cls] + in_cls, 0).astype(I32)
    nvalid = jnp.where(active, jnp.clip(counts[blk_cls] - in_cls, 0, MOE_ROWS), 0).astype(I32)
    trash = (n_tok + blk_start - blk_off - nvalid).astype(I32)
    pair_lo = jnp.array(PAIR_LO, I32)
    pair_hi = jnp.array(PAIR_HI, I32)
    blk_e1 = (blk_cls // N_PAIRS) * EXPERTS_PER_GROUP + pair_lo[blk_cls % N_PAIRS]
    blk_e2 = (blk_cls // N_PAIRS) * EXPERTS_PER_GROUP + pair_hi[blk_cls % N_PAIRS]
    plan = (order, blk_off, nvalid, trash, blk_e1, blk_e2, nblk.astype(I32).reshape(1))
    return plan, gates_sorted


def kernel(x, mem, conv_w_in, conv_w, mlstm_w_in, mlstm_gate_b, mlstm_norm_g, w_kv_mem, w_out,
           ln1_g, ln1_b, ln2_g, ln2_b, router_w, router_b, w_gate, w_up, w_down):
    nb, seq, d = x.shape
    n_tok = nb * seq
    mem_len = mem.shape[1]
    assert d == D_MODEL and seq % TM_CONV == 0 and seq % TM_MLSTM == 0 and TM_MLSTM % CHUNK == 0

    w_kv_all = jnp.transpose(w_kv_mem, (1, 0, 2)).reshape(d, DEPTH * 2 * XATT_W).astype(BF16)
    kv = _matmul(mem.reshape(nb * mem_len, d), w_kv_all, 512, 1024)
    kv = kv.reshape(nb, mem_len, DEPTH, 2, XATT_HEADS, XATT_DH)
    eye = jnp.eye(XATT_HEADS, dtype=bool)

    perm = lambda a: jnp.swapaxes(a.reshape(N_GROUPS, EXPERTS_PER_GROUP, -1), 0, 1)
    rwt = perm(router_w.T).reshape(N_EXPERTS, d).astype(BF16)
    rb = perm(router_b).reshape(N_EXPERTS, 1)

    xf = x.reshape(n_tok, d)
    for i in range(DEPTH):
        j = i // 2
        k = kv[:, :, i, 0] * (XATT_DH ** -0.5)
        v = kv[:, :, i, 1]
        kbd = jnp.where(eye[None, :, None, :, None],
                        jnp.transpose(k, (0, 2, 3, 1))[:, :, :, None, :], 0.0)
        kbd = kbd.reshape(nb, XATT_W, XATT_HEADS * mem_len).astype(BF16)
        vbd = jnp.where(eye[None, :, None, :, None],
                        jnp.transpose(v, (0, 2, 1, 3))[:, :, :, None, :], 0.0)
        vbd = vbd.reshape(nb, XATT_HEADS * mem_len, XATT_W).astype(BF16)
        wout = w_out[i].astype(BF16)
        if i % 2 == 0:
            win = conv_w_in[j].astype(BF16)
            outs = _mixer_half(_conv_half_kernel, TM_CONV, xf, nb, seq, [win, conv_w[j]],
                               kbd, vbd, wout, ln1_g[i], ln1_b[i], rwt, rb,
                               [pltpu.VMEM((8, MIX_W), F32)], "conv_half")
        else:
            w = mlstm_w_in[j]
            o0 = 4 * MIX_W
            w_gates = jnp.pad(w[:, o0:o0 + 2 * ML_HEADS], ((0, 0), (0, GATE_PAD - 2 * ML_HEADS)))
            win = jnp.concatenate([w[:, :o0], w[:, o0 + 2 * ML_HEADS:], w_gates], axis=1).astype(BF16)
            gate_b = jnp.pad(mlstm_gate_b[j], (0, GATE_PAD - 2 * ML_HEADS)).reshape(1, GATE_PAD)
            norm_g = mlstm_norm_g[j].reshape(1, MIX_W)
            outs = _mixer_half(_mlstm_half_kernel, TM_MLSTM, xf, nb, seq, [win, gate_b, norm_g],
                               kbd, vbd, wout, ln1_g[i], ln1_b[i], rwt, rb,
                               [pltpu.VMEM((ML_HEADS, ML_DH, 2 * ML_DH), F32),
                                pltpu.VMEM((ML_HEADS, 1, LANES), F32),
                                pltpu.VMEM((TM_MLSTM, MIX_W), F32)], "mlstm_half")
        x1, route_f, route_i, counts_f = outs
        plan, gates_sorted = _plan_blocks(route_f, route_i, counts_f, n_tok)
        xf = _moe_half(i, x1, plan, gates_sorted, w_gate, w_up, w_down, ln2_g[i], ln2_b[i])
    return xf[:n_tok].reshape(nb, seq, d)
```

```python
import jax
import jax.numpy as jnp
from jax import lax
from jax.experimental import pallas as pl
from jax.experimental.pallas import tpu as pltpu

F32 = jnp.float32
BF16 = jnp.bfloat16
I32 = jnp.int32

D_MODEL = 1024
DEPTH = 4
MIX_W = 768
XATT_HEADS = 4
XATT_DH = 64
XATT_W = XATT_HEADS * XATT_DH
CONV_K = 3
ML_HEADS = 6
ML_DH = MIX_W // ML_HEADS
CHUNK = 128
N_EXPERTS = 32
N_GROUPS = 8
EXPERTS_PER_GROUP = N_EXPERTS // N_GROUPS
D_FF = 512
ALPHA = (2.0 * DEPTH) ** 0.25
LN_EPS = 1e-5

LANES = 128
SUBLANES = 8
ROW_TILES = D_MODEL // LANES
GATE_PAD = LANES
MOE_ROWS = 128
PAIR_LO = (0, 0, 1, 1, 2, 0)
PAIR_HI = (1, 2, 2, 3, 3, 3)
N_PAIRS = len(PAIR_LO)
N_CLASSES = N_GROUPS * N_PAIRS
CLASS_PAD = 64
VMEM_LIMIT = 52 * 1024 * 1024

TM_CONV = 512
TM_MLSTM = 256


def _layer_norm(z, g, b):
    mu = jnp.mean(z, axis=-1, keepdims=True)
    zc = z - mu
    var = jnp.mean(zc * zc, axis=-1, keepdims=True)
    return zc * lax.rsqrt(var + LN_EPS) * g + b


def _load_rows(ref, n):
    return jnp.concatenate([ref[pl.ds(j, n, stride=ROW_TILES), :] for j in range(ROW_TILES)],
                           axis=-1)


def _load_x(x_ref):
    if x_ref.shape[1] == D_MODEL:
        return x_ref[...]
    return _load_rows(x_ref, x_ref.shape[0] // ROW_TILES)


def _store_rows(ref, val):
    for j in range(ROW_TILES):
        ref[pl.ds(j, val.shape[0], stride=ROW_TILES), :] = val[:, j * LANES:(j + 1) * LANES]


def _dot(a, b):
    return jnp.dot(a, b, preferred_element_type=F32)


def _dot_nt(a, b):
    return lax.dot_general(a, b, (((1,), (1,)), ((), ())), preferred_element_type=F32)


def _matmul_kernel(a_ref, b_ref, o_ref):
    o_ref[...] = _dot(a_ref[...].astype(BF16), b_ref[...])


def _matmul(a, b_bf16, tm, tn):
    m, k = a.shape
    n = b_bf16.shape[1]
    return pl.pallas_call(
        _matmul_kernel,
        grid=(m // tm, n // tn),
        in_specs=[pl.BlockSpec((tm, k), lambda i, j: (i, 0)),
                  pl.BlockSpec((k, tn), lambda i, j: (0, j))],
        out_specs=pl.BlockSpec((tm, tn), lambda i, j: (i, j)),
        out_shape=jax.ShapeDtypeStruct((m, n), F32),
        compiler_params=pltpu.CompilerParams(
            dimension_semantics=("arbitrary", "arbitrary"), vmem_limit_bytes=VMEM_LIMIT),
        name="kv_proj",
    )(a, b_bf16)


def _route_tile(x1b, rwt_ref, rb_ref, tri_ref, base_ref, rf_ref, ri_ref, cnt_ref):
    tm = x1b.shape[0]
    scores = jax.nn.sigmoid(_dot_nt(rwt_ref[...], x1b))
    sel = scores + rb_ref[...]
    s = [sel[SUBLANES * j:SUBLANES * (j + 1)] for j in range(EXPERTS_PER_GROUP)]
    p = [scores[SUBLANES * j:SUBLANES * (j + 1)] for j in range(EXPERTS_PER_GROUP)]
    hi1, lo1 = jnp.maximum(s[0], s[1]), jnp.minimum(s[0], s[1])
    hi2, lo2 = jnp.maximum(s[2], s[3]), jnp.minimum(s[2], s[3])
    gs = jnp.maximum(hi1, hi2) + jnp.maximum(jnp.minimum(hi1, hi2), jnp.maximum(lo1, lo2))
    sub = lax.broadcasted_iota(I32, (N_GROUPS, tm), 0)
    gi = jnp.min(jnp.where(gs == jnp.max(gs, axis=0, keepdims=True), sub, N_GROUPS),
                 axis=0, keepdims=True)
    gm = sub == gi
    v = [jnp.sum(jnp.where(gm, sj, 0.0), axis=0, keepdims=True) for sj in s]
    pv = [jnp.sum(jnp.where(gm, pj, 0.0), axis=0, keepdims=True) for pj in p]
    best, i1 = v[0], jnp.zeros((1, tm), I32)
    for j in range(1, EXPERTS_PER_GROUP):
        u = v[j] > best
        best, i1 = jnp.where(u, v[j], best), jnp.where(u, j, i1)
    best2, i2 = jnp.full((1, tm), -jnp.inf, F32), jnp.zeros((1, tm), I32)
    for j in range(EXPERTS_PER_GROUP):
        cand = jnp.where(i1 == j, -jnp.inf, v[j])
        u = cand > best2
        best2, i2 = jnp.where(u, cand, best2), jnp.where(u, j, i2)

    def pick(idx):
        return jnp.where(idx == 0, pv[0], jnp.where(idx == 1, pv[1],
                                                    jnp.where(idx == 2, pv[2], pv[3])))

    p1, p2 = pick(i1), pick(i2)
    tot = p1 + p2
    first_lo = i1 < i2
    g_lo = jnp.where(first_lo, p1, p2) / tot
    g_hi = jnp.where(first_lo, p2, p1) / tot
    lo, hi = jnp.minimum(i1, i2), jnp.maximum(i1, i2)
    pair = jnp.where(hi == 1, 0, jnp.where(hi == 2, jnp.where(lo == 0, 1, 2),
                                           jnp.where(lo == 1, 3, jnp.where(lo == 2, 4, 5))))
    cls = gi * N_PAIRS + pair

    oh = lax.broadcasted_iota(I32, (CLASS_PAD, tm), 0) == cls
    pref = _dot(oh.astype(F32).astype(BF16), tri_ref[...])
    base = base_ref[...]
    rank = jnp.sum(jnp.where(oh, pref + base[:, 0:1] - 1.0, 0.0), axis=0, keepdims=True)
    base = base + pref[:, tm - 1:tm]
    base_ref[...] = base
    cnt_ref[...] = base

    sub8 = lax.broadcasted_iota(I32, (SUBLANES, tm), 0)
    rf_ref[...] = jnp.where(sub8 == 0, g_lo, jnp.where(sub8 == 1, g_hi, 0.0))
    ri_ref[...] = jnp.where(sub8 == 0, cls, jnp.where(sub8 == 1, rank.astype(I32), 0))


def _mixer_tail(x, h_loc, q_mem, tail_refs):
    (kbd_ref, vbd_ref, wout_ref, g_ref, b_ref, rwt_ref, rb_ref, tri_ref,
     x1_ref, rf_ref, ri_ref, cnt_ref, base_ref) = tail_refs

    @pl.when((pl.program_id(0) == 0) & (pl.program_id(1) == 0))
    def _():
        base_ref[...] = jnp.zeros_like(base_ref)

    s = _dot(q_mem.astype(BF16), kbd_ref[0])
    mem_len = s.shape[1] // XATT_HEADS
    ps = []
    for h in range(XATT_HEADS):
        sh = s[:, h * mem_len:(h + 1) * mem_len]
        e = jnp.exp(sh - jnp.max(sh, axis=-1, keepdims=True))
        ps.append((e / jnp.sum(e, axis=-1, keepdims=True)).astype(BF16))
    h_mem = _dot(jnp.concatenate(ps, axis=-1), vbd_ref[0])
    hcat = jnp.concatenate([h_loc.astype(BF16), h_mem.astype(BF16)], axis=-1)
    y = _dot(hcat, wout_ref[...])
    x1 = _layer_norm(ALPHA * x + y, g_ref[...], b_ref[...])
    _store_rows(x1_ref, x1)
    _route_tile(x1.astype(BF16), rwt_ref, rb_ref, tri_ref, base_ref, rf_ref, ri_ref, cnt_ref)


def _conv_half_kernel(x_ref, win_ref, cw_ref, *refs):
    tail_refs, carry_ref = refs[:-1], refs[-1]

    @pl.when(pl.program_id(1) == 0)
    def _():
        carry_ref[...] = jnp.zeros_like(carry_ref)

    x = _load_x(x_ref)
    tm = x.shape[0]
    proj = _dot(x.astype(BF16), win_ref[...])
    bg = proj[:, 0:MIX_W]
    cg = proj[:, MIX_W:2 * MIX_W]
    xin = proj[:, 2 * MIX_W:3 * MIX_W]
    q_mem = proj[:, 3 * MIX_W:]
    u = cg * xin
    prev = carry_ref[...]
    row = lax.broadcasted_iota(I32, u.shape, 0)
    u1 = jnp.where(row == 0, prev[7:8], pltpu.roll(u, 1, 0))
    u2 = jnp.where(row == 0, prev[6:7],
                   jnp.where(row == 1, prev[7:8], pltpu.roll(u, 2, 0)))
    carry_ref[...] = u[tm - 8:tm]
    cw = cw_ref[...]
    c = cw[0:1] * u2 + cw[1:2] * u1 + cw[2:3] * u
    _mixer_tail(x, bg * c, q_mem, tail_refs)


def _log_sigmoid(z):
    return -(jnp.maximum(-z, 0.0) + jnp.log1p(jnp.exp(-jnp.abs(z))))


def _mlstm_half_kernel(x_ref, win_ref, gb_ref, ng_ref, *refs):
    tail_refs, (state_ref, m_ref, h_ref) = refs[:-3], refs[-3:]

    @pl.when(pl.program_id(1) == 0)
    def _():
        state_ref[...] = jnp.zeros_like(state_ref)
        m_ref[...] = jnp.zeros_like(m_ref)

    x = _load_x(x_ref)
    tm = x.shape[0]
    proj = _dot(x.astype(BF16), win_ref[...])
    g0 = 4 * MIX_W + XATT_W
    gates = proj[:, g0:g0 + GATE_PAD] + gb_ref[...]
    logf = _log_sigmoid(gates)
    row = lax.broadcasted_iota(I32, (CHUNK, GATE_PAD), 0)
    lane = lax.broadcasted_iota(I32, (CHUNK, GATE_PAD), 1)
    causal = (lax.broadcasted_iota(I32, (CHUNK, CHUNK), 0)
              >= lax.broadcasted_iota(I32, (CHUNK, CHUNK), 1))
    ones_col = (lax.broadcasted_iota(I32, (CHUNK, ML_DH), 1) == 0).astype(BF16)

    for c in range(tm // CHUNK):
        r0 = c * CHUNK
        bc = logf[r0:r0 + CHUNK]
        sh = 1
        while sh < CHUNK:
            bc = bc + jnp.where(row >= sh, pltpu.roll(bc, sh, 0), 0.0)
            sh *= 2
        mct = jnp.where(lane < ML_HEADS, gates[r0:r0 + CHUNK], bc).T
        for h in range(ML_HEADS):
            q = proj[r0:r0 + CHUNK, h * ML_DH:(h + 1) * ML_DH]
            k = proj[r0:r0 + CHUNK, MIX_W + h * ML_DH:MIX_W + (h + 1) * ML_DH] * (ML_DH ** -0.5)
            v = proj[r0:r0 + CHUNK, 2 * MIX_W + h * ML_DH:2 * MIX_W + (h + 1) * ML_DH]
            qb, kb, vb = q.astype(BF16), k.astype(BF16), v.astype(BF16)
            b_col = bc[:, ML_HEADS + h:ML_HEADS + h + 1]
            b_row = mct[ML_HEADS + h:ML_HEADS + h + 1, :]
            li_row = mct[h:h + 1, :]
            m_prev = m_ref[h][:, 0:1]
            state = state_ref[h]

            d = jnp.where(causal, b_col - b_row + li_row, -jnp.inf)
            inter = b_col + m_prev
            m_t = jnp.maximum(jnp.max(d, axis=-1, keepdims=True), inter)
            s = _dot_nt(qb, kb) * jnp.exp(d - m_t)
            w_inter = jnp.exp(inter - m_t)
            qs = _dot(qb, state.astype(BF16))
            num = _dot(s.astype(BF16), vb) + w_inter * qs[:, 0:ML_DH]
            den = jnp.sum(s, axis=-1, keepdims=True) + w_inter * qs[:, ML_DH:ML_DH + 1]
            hh = num / jnp.maximum(jnp.abs(den), jnp.exp(-m_t))
            mu = jnp.mean(hh, axis=-1, keepdims=True)
            hc = hh - mu
            var = jnp.mean(hc * hc, axis=-1, keepdims=True)
            h_ref[r0:r0 + CHUNK, h * ML_DH:(h + 1) * ML_DH] = hc * lax.rsqrt(var + LN_EPS)

            g = b_row[:, CHUNK - 1:CHUNK]
            a_row = g - b_row + li_row
            m_new = jnp.maximum(g + m_prev, jnp.max(a_row, axis=-1, keepdims=True))
            decay = jnp.exp(g + m_prev - m_new)
            w_row = jnp.exp(a_row - m_new)
            ktw = (k.T * w_row).astype(BF16)
            v_aug = jnp.concatenate([vb, ones_col], axis=-1)
            state_ref[h] = decay * state + _dot(ktw, v_aug)
            m_ref[h] = jnp.broadcast_to(m_new, (1, LANES))

    o_gate = proj[:, 3 * MIX_W:4 * MIX_W]
    h_loc = jax.nn.sigmoid(o_gate) * (h_ref[...] * ng_ref[...])
    q_mem = proj[:, 4 * MIX_W:4 * MIX_W + XATT_W]
    _mixer_tail(x, h_loc, q_mem, tail_refs)


def _mixer_half(kernel_fn, tm, x, nb, seq, mixer_args, kbd, vbd, wout, ln_g, ln_b, rwt, rb,
                scratch, name):
    d = D_MODEL
    n_tok = nb * seq
    spb = seq // tm
    const2 = lambda b, s: (0, 0)
    tri = jnp.triu(jnp.ones((tm, tm), BF16))
    tail_args = [kbd, vbd, wout, ln_g.reshape(1, d), ln_b.reshape(1, d), rwt, rb, tri]
    row_map = lambda b, s: (b * spb + s, 0)
    x_spec = (pl.BlockSpec((tm, d), row_map) if x.shape[1] == d
              else pl.BlockSpec((tm * ROW_TILES, LANES), row_map))
    in_specs = ([x_spec]
                + [pl.BlockSpec(a.shape, const2) for a in mixer_args]
                + [pl.BlockSpec((1,) + kbd.shape[1:], lambda b, s: (b, 0, 0)),
                   pl.BlockSpec((1,) + vbd.shape[1:], lambda b, s: (b, 0, 0))]
                + [pl.BlockSpec(a.shape, const2) for a in tail_args[2:]])
    tok_spec = lambda rows: pl.BlockSpec((rows, tm), lambda b, s: (0, b * spb + s))
    return pl.pallas_call(
        kernel_fn,
        grid=(nb, spb),
        in_specs=in_specs,
        out_specs=[pl.BlockSpec((tm * ROW_TILES, LANES), row_map),
                   tok_spec(SUBLANES), tok_spec(SUBLANES),
                   pl.BlockSpec((CLASS_PAD, LANES), const2)],
        out_shape=[jax.ShapeDtypeStruct((n_tok * ROW_TILES, LANES), F32),
                   jax.ShapeDtypeStruct((SUBLANES, n_tok), F32),
                   jax.ShapeDtypeStruct((SUBLANES, n_tok), I32),
                   jax.ShapeDtypeStruct((CLASS_PAD, LANES), F32)],
        scratch_shapes=[pltpu.VMEM((CLASS_PAD, LANES), F32)] + scratch,
        compiler_params=pltpu.CompilerParams(
            dimension_semantics=("arbitrary", "arbitrary"), vmem_limit_bytes=VMEM_LIMIT),
        name=name,
    )(x, *mixer_args, *tail_args)


def _moe_kernel(order_ref, off_ref, nvalid_ref, trash_ref, e1_ref, e2_ref, nblk_ref,
                x_hbm, g_hbm, wg1_ref, wu1_ref, wd1_ref, wg2_ref, wu2_ref, wd2_ref,
                lng_ref, lnb_ref, out_hbm,
                xbuf, gbuf, obuf, zbuf, bg1, bu1, bd1, bg2, bu2, bd2, gsem, ssem, zsem):
    b = pl.program_id(0)
    nblk = nblk_ref[0]
    slot = b % 2

    def gate_copy(blk, slt):
        return pltpu.make_async_copy(g_hbm.at[pl.ds(off_ref[blk], MOE_ROWS)], gbuf.at[slt],
                                     gsem.at[slt])

    def issue_gather(blk, slt):
        base = off_ref[blk]
        for r in range(MOE_ROWS):
            t8 = pl.multiple_of(order_ref[base + r], ROW_TILES)
            pltpu.make_async_copy(x_hbm.at[pl.ds(t8, ROW_TILES)],
                                  xbuf.at[slt, pl.ds(r * ROW_TILES, ROW_TILES)],
                                  gsem.at[slt]).start()
        gate_copy(blk, slt).start()

    def wait_gather(slt):
        pltpu.make_async_copy(x_hbm.at[pl.ds(0, MOE_ROWS * ROW_TILES)], xbuf.at[slt],
                              gsem.at[slt]).wait()
        gate_copy(0, slt).wait()

    def issue_scatter(blk, slt):
        base = off_ref[blk]
        nvalid = nvalid_ref[blk]
        trash8 = trash_ref[blk]
        for r in range(MOE_ROWS):
            t8 = jnp.where(r < nvalid, order_ref[base + r], trash8 + r * ROW_TILES)
            pltpu.make_async_copy(obuf.at[slt, pl.ds(r * ROW_TILES, ROW_TILES)],
                                  out_hbm.at[pl.ds(pl.multiple_of(t8, ROW_TILES), ROW_TILES)],
                                  ssem.at[slt]).start()

    def wait_scatter(slt):
        pltpu.make_async_copy(obuf.at[slt], out_hbm.at[pl.ds(0, MOE_ROWS * ROW_TILES)],
                              ssem.at[slt]).wait()

    @pl.when(b < nblk)
    def _():
        @pl.when(b == 0)
        def _():
            issue_gather(0, 0)

        @pl.when(b + 1 < nblk)
        def _():
            issue_gather(b + 1, 1 - slot)

        prev = jnp.maximum(b - 1, 0)

        @pl.when((b == 0) | (e1_ref[b] != e1_ref[prev]))
        def _():
            bg1[...] = wg1_ref[0, 0].astype(BF16)
            bu1[...] = wu1_ref[0, 0].astype(BF16)
            bd1[...] = wd1_ref[0, 0].astype(BF16)

        @pl.when((b == 0) | (e2_ref[b] != e2_ref[prev]))
        def _():
            bg2[...] = wg2_ref[0, 0].astype(BF16)
            bu2[...] = wu2_ref[0, 0].astype(BF16)
            bd2[...] = wd2_ref[0, 0].astype(BF16)

        wait_gather(slot)
        x = _load_rows(xbuf.at[slot], MOE_ROWS)
        xb = x.astype(BF16)
        gates = gbuf[slot]

        def ffn(wg, wu, wd):
            hg = _dot(xb, wg[...])
            hu = _dot(xb, wu[...])
            return _dot((hg * jax.nn.sigmoid(hg) * hu).astype(BF16), wd[...])

        m = gates[:, 0:1] * ffn(bg1, bu1, bd1) + gates[:, 1:2] * ffn(bg2, bu2, bd2)
        o = _layer_norm(ALPHA * x + m, lng_ref[...], lnb_ref[...])

        @pl.when(b >= 2)
        def _():
            wait_scatter(slot)

        _store_rows(obuf.at[slot], o)
        issue_scatter(b, slot)

    @pl.when(b >= nblk)
    def _():
        zbuf[...] = jnp.zeros_like(zbuf)
        fill = pltpu.make_async_copy(
            zbuf, out_hbm.at[pl.ds(b * MOE_ROWS * ROW_TILES, MOE_ROWS * ROW_TILES)], zsem)
        fill.start()
        fill.wait()

    @pl.when(b == pl.num_programs(0) - 1)
    def _():
        @pl.when(nblk >= 1)
        def _():
            wait_scatter((nblk - 1) % 2)

        @pl.when(nblk >= 2)
        def _():
            wait_scatter(nblk % 2)


def _moe_half(layer, x1, plan, gates_sorted, w_gate, w_up, w_down, ln_g, ln_b):
    n_blocks = plan[1].shape[0]
    n_rows = n_blocks * MOE_ROWS
    d, f = D_MODEL, D_FF
    e1_map = lambda b, order, off, nv, tr, e1, e2, nb: (layer, e1[b], 0, 0)
    e2_map = lambda b, order, off, nv, tr, e1, e2, nb: (layer, e2[b], 0, 0)
    const_spec = pl.BlockSpec((1, d), lambda b, *_: (0, 0))
    any_spec = pl.BlockSpec(memory_space=pl.ANY)
    grid_spec = pltpu.PrefetchScalarGridSpec(
        num_scalar_prefetch=7,
        grid=(n_blocks,),
        in_specs=[any_spec, any_spec,
                  pl.BlockSpec((1, 1, d, f), e1_map), pl.BlockSpec((1, 1, d, f), e1_map),
                  pl.BlockSpec((1, 1, f, d), e1_map),
                  pl.BlockSpec((1, 1, d, f), e2_map), pl.BlockSpec((1, 1, d, f), e2_map),
                  pl.BlockSpec((1, 1, f, d), e2_map),
                  const_spec, const_spec],
        out_specs=any_spec,
        scratch_shapes=[
            pltpu.VMEM((2, MOE_ROWS * ROW_TILES, LANES), F32),
            pltpu.VMEM((2, MOE_ROWS, LANES), F32),
            pltpu.VMEM((2, MOE_ROWS * ROW_TILES, LANES), F32),
            pltpu.VMEM((MOE_ROWS * ROW_TILES, LANES), F32),
            pltpu.VMEM((d, f), BF16), pltpu.VMEM((d, f), BF16), pltpu.VMEM((f, d), BF16),
            pltpu.VMEM((d, f), BF16), pltpu.VMEM((d, f), BF16), pltpu.VMEM((f, d), BF16),
            pltpu.SemaphoreType.DMA((2,)),
            pltpu.SemaphoreType.DMA((2,)),
            pltpu.SemaphoreType.DMA(()),
        ],
    )
    return pl.pallas_call(
        _moe_kernel,
        grid_spec=grid_spec,
        out_shape=jax.ShapeDtypeStruct((n_rows * ROW_TILES, LANES), F32),
        compiler_params=pltpu.CompilerParams(
            dimension_semantics=("arbitrary",), vmem_limit_bytes=VMEM_LIMIT),
        name="moe_half",
    )(*plan, x1, gates_sorted, w_gate, w_up, w_down, w_gate, w_up, w_down,
      ln_g.reshape(1, d), ln_b.reshape(1, d))


def _plan_blocks(route_f, route_i, counts_f, n_tok):
    cls, rank = route_i[0], route_i[1]
    counts = counts_f[:N_CLASSES, 0].astype(I32)
    padded = (counts + MOE_ROWS - 1) // MOE_ROWS * MOE_ROWS
    pad_end = jnp.cumsum(padded)
    pad_start = pad_end - padded
    start = jnp.cumsum(counts) - counts
    class_ids = jnp.arange(N_CLASSES, dtype=I32)
    dest = jnp.sum(jnp.where(cls[:, None] == class_ids[None], pad_start[None], 0), axis=1) + rank
    _, order, g_lo, g_hi = lax.sort(
        (dest, jnp.arange(n_tok, dtype=I32), route_f[0], route_f[1]), num_keys=1)
    order = jnp.pad(order, (0, MOE_ROWS)) * ROW_TILES
    gates_sorted = jnp.pad(jnp.stack([g_lo, g_hi], axis=1), ((0, MOE_ROWS), (0, LANES - 2)))

    n_blocks = n_tok // MOE_ROWS + N_CLASSES
    blk_ids = jnp.arange(n_blocks, dtype=I32)
    blk_start = blk_ids * MOE_ROWS
    nblk = pad_end[-1] // MOE_ROWS
    blk_cls = jnp.minimum(jnp.sum(pad_end[None] <= blk_start[:, None], axis=1), N_CLASSES - 1)
    blk_cls = jnp.where(blk_ids < nblk, blk_cls, blk_cls[jnp.maximum(nblk - 1, 0)]).astype(I32)
    in_cls = blk_start - pad_start[blk_cls]
    active = blk_ids < nblk
    blk_off = jnp.where(active, start[blk_cls] + in_cls, 0).astype(I32)
    nvalid = jnp.where(active, jnp.clip(counts[blk_cls] - in_cls, 0, MOE_ROWS), 0).astype(I32)
    trash = ((n_tok + blk_start - blk_off - nvalid) * ROW_TILES).astype(I32)
    pair_lo = jnp.array(PAIR_LO, I32)
    pair_hi = jnp.array(PAIR_HI, I32)
    blk_e1 = (blk_cls // N_PAIRS) * EXPERTS_PER_GROUP + pair_lo[blk_cls % N_PAIRS]
    blk_e2 = (blk_cls // N_PAIRS) * EXPERTS_PER_GROUP + pair_hi[blk_cls % N_PAIRS]
    plan = (order, blk_off, nvalid, trash, blk_e1, blk_e2, nblk.astype(I32).reshape(1))
    return plan, gates_sorted


def kernel(x, mem, conv_w_in, conv_w, mlstm_w_in, mlstm_gate_b, mlstm_norm_g, w_kv_mem, w_out,
           ln1_g, ln1_b, ln2_g, ln2_b, router_w, router_b, w_gate, w_up, w_down):
    nb, seq, d = x.shape
    n_tok = nb * seq
    mem_len = mem.shape[1]
    assert d == D_MODEL and seq % TM_CONV == 0 and seq % TM_MLSTM == 0 and TM_MLSTM % CHUNK == 0

    w_kv_all = jnp.transpose(w_kv_mem, (1, 0, 2)).reshape(d, DEPTH * 2 * XATT_W).astype(BF16)
    kv = _matmul(mem.reshape(nb * mem_len, d), w_kv_all, 512, 1024)
    kv = kv.reshape(nb, mem_len, DEPTH, 2, XATT_HEADS, XATT_DH)
    eye = jnp.eye(XATT_HEADS, dtype=bool)

    perm = lambda a: jnp.swapaxes(a.reshape(N_GROUPS, EXPERTS_PER_GROUP, -1), 0, 1)
    rwt = perm(router_w.T).reshape(N_EXPERTS, d).astype(BF16)
    rb = perm(router_b).reshape(N_EXPERTS, 1)

    xf = x.reshape(n_tok, d)
    for i in range(DEPTH):
        j = i // 2
        k = kv[:, :, i, 0] * (XATT_DH ** -0.5)
        v = kv[:, :, i, 1]
        kbd = jnp.where(eye[None, :, None, :, None],
                        jnp.transpose(k, (0, 2, 3, 1))[:, :, :, None, :], 0.0)
        kbd = kbd.reshape(nb, XATT_W, XATT_HEADS * mem_len).astype(BF16)
        vbd = jnp.where(eye[None, :, None, :, None],
                        jnp.transpose(v, (0, 2, 1, 3))[:, :, :, None, :], 0.0)
        vbd = vbd.reshape(nb, XATT_HEADS * mem_len, XATT_W).astype(BF16)
        wout = w_out[i].astype(BF16)
        if i % 2 == 0:
            win = conv_w_in[j].astype(BF16)
            outs = _mixer_half(_conv_half_kernel, TM_CONV, xf, nb, seq, [win, conv_w[j]],
                               kbd, vbd, wout, ln1_g[i], ln1_b[i], rwt, rb,
                               [pltpu.VMEM((8, MIX_W), F32)], "conv_half")
        else:
            w = mlstm_w_in[j]
            o0 = 4 * MIX_W
            w_gates = jnp.pad(w[:, o0:o0 + 2 * ML_HEADS], ((0, 0), (0, GATE_PAD - 2 * ML_HEADS)))
            win = jnp.concatenate([w[:, :o0], w[:, o0 + 2 * ML_HEADS:], w_gates], axis=1).astype(BF16)
            gate_b = jnp.pad(mlstm_gate_b[j], (0, GATE_PAD - 2 * ML_HEADS)).reshape(1, GATE_PAD)
            norm_g = mlstm_norm_g[j].reshape(1, MIX_W)
            outs = _mixer_half(_mlstm_half_kernel, TM_MLSTM, xf, nb, seq, [win, gate_b, norm_g],
                               kbd, vbd, wout, ln1_g[i], ln1_b[i], rwt, rb,
                               [pltpu.VMEM((ML_HEADS, ML_DH, 2 * ML_DH), F32),
                                pltpu.VMEM((ML_HEADS, 1, LANES), F32),
                                pltpu.VMEM((TM_MLSTM, MIX_W), F32)], "mlstm_half")
        x1, route_f, route_i, counts_f = outs
        plan, gates_sorted = _plan_blocks(route_f, route_i, counts_f, n_tok)
        xf = _moe_half(i, x1, plan, gates_sorted, w_gate, w_up, w_down, ln2_g[i], ln2_b[i])
    return xf[:n_tok * ROW_TILES].reshape(nb, seq, d)
```

```python
import jax
import jax.numpy as jnp
from jax import lax
from jax.experimental import pallas as pl
from jax.experimental.pallas import tpu as pltpu

F32 = jnp.float32
BF16 = jnp.bfloat16
I32 = jnp.int32

D_MODEL = 1024
DEPTH = 4
MIX_W = 768
XATT_HEADS = 4
XATT_DH = 64
XATT_W = XATT_HEADS * XATT_DH
CONV_K = 3
ML_HEADS = 6
ML_DH = MIX_W // ML_HEADS
CHUNK = 128
N_EXPERTS = 32
N_GROUPS = 8
EXPERTS_PER_GROUP = N_EXPERTS // N_GROUPS
D_FF = 512
ALPHA = (2.0 * DEPTH) ** 0.25
LN_EPS = 1e-5

LANES = 128
SUBLANES = 8
ROW_TILES = D_MODEL // LANES
GATE_PAD = LANES
MOE_ROWS = 128
PAIR_LO = (0, 0, 1, 1, 2, 0)
PAIR_HI = (1, 2, 2, 3, 3, 3)
N_PAIRS = len(PAIR_LO)
N_CLASSES = N_GROUPS * N_PAIRS
CLASS_PAD = 64
VMEM_LIMIT = 52 * 1024 * 1024

TM_CONV = 512
TM_MLSTM = 256


def _layer_norm(z, g, b):
    mu = jnp.mean(z, axis=-1, keepdims=True)
    zc = z - mu
    var = jnp.mean(zc * zc, axis=-1, keepdims=True)
    return zc * lax.rsqrt(var + LN_EPS) * g + b


def _load_rows(ref, n):
    return jnp.concatenate([ref[pl.ds(j, n, stride=ROW_TILES), :] for j in range(ROW_TILES)],
                           axis=-1)


def _load_x(x_ref):
    if x_ref.shape[1] == D_MODEL:
        return x_ref[...]
    return _load_rows(x_ref, x_ref.shape[0] // ROW_TILES)


def _store_rows(ref, val):
    for j in range(ROW_TILES):
        ref[pl.ds(j, val.shape[0], stride=ROW_TILES), :] = val[:, j * LANES:(j + 1) * LANES]


def _dot(a, b):
    return jnp.dot(a, b, preferred_element_type=F32)


def _dot_nt(a, b):
    return lax.dot_general(a, b, (((1,), (1,)), ((), ())), preferred_element_type=F32)


def _matmul_kernel(a_ref, b_ref, o_ref):
    o_ref[...] = _dot(a_ref[...].astype(BF16), b_ref[...])


def _matmul(a, b_bf16, tm, tn):
    m, k = a.shape
    n = b_bf16.shape[1]
    return pl.pallas_call(
        _matmul_kernel,
        grid=(m // tm, n // tn),
        in_specs=[pl.BlockSpec((tm, k), lambda i, j: (i, 0)),
                  pl.BlockSpec((k, tn), lambda i, j: (0, j))],
        out_specs=pl.BlockSpec((tm, tn), lambda i, j: (i, j)),
        out_shape=jax.ShapeDtypeStruct((m, n), F32),
        compiler_params=pltpu.CompilerParams(
            dimension_semantics=("arbitrary", "arbitrary"), vmem_limit_bytes=VMEM_LIMIT),
        name="kv_proj",
    )(a, b_bf16)


def _route_tile(x1b, rwt_ref, rb_ref, tri_ref, base_ref, rf_ref, ri_ref, cnt_ref):
    tm = x1b.shape[0]
    scores = jax.nn.sigmoid(_dot_nt(rwt_ref[...], x1b))
    sel = scores + rb_ref[...]
    s = [sel[SUBLANES * j:SUBLANES * (j + 1)] for j in range(EXPERTS_PER_GROUP)]
    p = [scores[SUBLANES * j:SUBLANES * (j + 1)] for j in range(EXPERTS_PER_GROUP)]
    hi1, lo1 = jnp.maximum(s[0], s[1]), jnp.minimum(s[0], s[1])
    hi2, lo2 = jnp.maximum(s[2], s[3]), jnp.minimum(s[2], s[3])
    gs = jnp.maximum(hi1, hi2) + jnp.maximum(jnp.minimum(hi1, hi2), jnp.maximum(lo1, lo2))
    sub = lax.broadcasted_iota(I32, (N_GROUPS, tm), 0)
    gi = jnp.min(jnp.where(gs == jnp.max(gs, axis=0, keepdims=True), sub, N_GROUPS),
                 axis=0, keepdims=True)
    gm = sub == gi
    v = [jnp.sum(jnp.where(gm, sj, 0.0), axis=0, keepdims=True) for sj in s]
    pv = [jnp.sum(jnp.where(gm, pj, 0.0), axis=0, keepdims=True) for pj in p]
    best, i1 = v[0], jnp.zeros((1, tm), I32)
    for j in range(1, EXPERTS_PER_GROUP):
        u = v[j] > best
        best, i1 = jnp.where(u, v[j], best), jnp.where(u, j, i1)
    best2, i2 = jnp.full((1, tm), -jnp.inf, F32), jnp.zeros((1, tm), I32)
    for j in range(EXPERTS_PER_GROUP):
        cand = jnp.where(i1 == j, -jnp.inf, v[j])
        u = cand > best2
        best2, i2 = jnp.where(u, cand, best2), jnp.where(u, j, i2)

    def pick(idx):
        return jnp.where(idx == 0, pv[0], jnp.where(idx == 1, pv[1],
                                                    jnp.where(idx == 2, pv[2], pv[3])))

    p1, p2 = pick(i1), pick(i2)
    tot = p1 + p2
    first_lo = i1 < i2
    g_lo = jnp.where(first_lo, p1, p2) / tot
    g_hi = jnp.where(first_lo, p2, p1) / tot
    lo, hi = jnp.minimum(i1, i2), jnp.maximum(i1, i2)
    pair = jnp.where(hi == 1, 0, jnp.where(hi == 2, jnp.where(lo == 0, 1, 2),
                                           jnp.where(lo == 1, 3, jnp.where(lo == 2, 4, 5))))
    cls = gi * N_PAIRS + pair

    oh = lax.broadcasted_iota(I32, (CLASS_PAD, tm), 0) == cls
    pref = _dot(oh.astype(F32).astype(BF16), tri_ref[...])
    base = base_ref[...]
    rank = jnp.sum(jnp.where(oh, pref + base[:, 0:1] - 1.0, 0.0), axis=0, keepdims=True)
    base = base + pref[:, tm - 1:tm]
    base_ref[...] = base
    cnt_ref[...] = base

    sub8 = lax.broadcasted_iota(I32, (SUBLANES, tm), 0)
    rf_ref[...] = jnp.where(sub8 == 0, g_lo, jnp.where(sub8 == 1, g_hi, 0.0))
    ri_ref[...] = jnp.where(sub8 == 0, cls, jnp.where(sub8 == 1, rank.astype(I32), 0))


def _mixer_tail(x, h_loc, q_mem, tail_refs):
    (kbd_ref, vbd_ref, wout_ref, g_ref, b_ref, rwt_ref, rb_ref, tri_ref,
     x1_ref, rf_ref, ri_ref, cnt_ref, base_ref) = tail_refs

    @pl.when((pl.program_id(0) == 0) & (pl.program_id(1) == 0))
    def _():
        base_ref[...] = jnp.zeros_like(base_ref)

    s = _dot(q_mem.astype(BF16), kbd_ref[0])
    mem_len = s.shape[1] // XATT_HEADS
    ps = []
    for h in range(XATT_HEADS):
        sh = s[:, h * mem_len:(h + 1) * mem_len]
        e = jnp.exp(sh - jnp.max(sh, axis=-1, keepdims=True))
        ps.append((e / jnp.sum(e, axis=-1, keepdims=True)).astype(BF16))
    h_mem = _dot(jnp.concatenate(ps, axis=-1), vbd_ref[0])
    hcat = jnp.concatenate([h_loc.astype(BF16), h_mem.astype(BF16)], axis=-1)
    y = _dot(hcat, wout_ref[...])
    x1 = _layer_norm(ALPHA * x + y, g_ref[...], b_ref[...])
    _store_rows(x1_ref, x1)
    _route_tile(x1.astype(BF16), rwt_ref, rb_ref, tri_ref, base_ref, rf_ref, ri_ref, cnt_ref)


def _conv_half_kernel(x_ref, win_ref, cw_ref, *refs):
    tail_refs, carry_ref = refs[:-1], refs[-1]

    @pl.when(pl.program_id(1) == 0)
    def _():
        carry_ref[...] = jnp.zeros_like(carry_ref)

    x = _load_x(x_ref)
    tm = x.shape[0]
    proj = _dot(x.astype(BF16), win_ref[...])
    bg = proj[:, 0:MIX_W]
    cg = proj[:, MIX_W:2 * MIX_W]
    xin = proj[:, 2 * MIX_W:3 * MIX_W]
    q_mem = proj[:, 3 * MIX_W:]
    u = cg * xin
    prev = carry_ref[...]
    row = lax.broadcasted_iota(I32, u.shape, 0)
    u1 = jnp.where(row == 0, prev[7:8], pltpu.roll(u, 1, 0))
    u2 = jnp.where(row == 0, prev[6:7],
                   jnp.where(row == 1, prev[7:8], pltpu.roll(u, 2, 0)))
    carry_ref[...] = u[tm - 8:tm]
    cw = cw_ref[...]
    c = cw[0:1] * u2 + cw[1:2] * u1 + cw[2:3] * u
    _mixer_tail(x, bg * c, q_mem, tail_refs)


def _log_sigmoid(z):
    return -(jnp.maximum(-z, 0.0) + jnp.log1p(jnp.exp(-jnp.abs(z))))


def _mlstm_half_kernel(x_ref, win_ref, gb_ref, ng_ref, *refs):
    tail_refs, (state_ref, m_ref, h_ref) = refs[:-3], refs[-3:]

    @pl.when(pl.program_id(1) == 0)
    def _():
        state_ref[...] = jnp.zeros_like(state_ref)
        m_ref[...] = jnp.zeros_like(m_ref)

    x = _load_x(x_ref)
    tm = x.shape[0]
    proj = _dot(x.astype(BF16), win_ref[...])
    g0 = 4 * MIX_W + XATT_W
    gates = proj[:, g0:g0 + GATE_PAD] + gb_ref[...]
    logf = _log_sigmoid(gates)
    row = lax.broadcasted_iota(I32, (CHUNK, GATE_PAD), 0)
    lane = lax.broadcasted_iota(I32, (CHUNK, GATE_PAD), 1)
    causal = (lax.broadcasted_iota(I32, (CHUNK, CHUNK), 0)
              >= lax.broadcasted_iota(I32, (CHUNK, CHUNK), 1))
    ones_col = (lax.broadcasted_iota(I32, (CHUNK, ML_DH), 1) == 0).astype(BF16)

    for c in range(tm // CHUNK):
        r0 = c * CHUNK
        bc = logf[r0:r0 + CHUNK]
        sh = 1
        while sh < CHUNK:
            bc = bc + jnp.where(row >= sh, pltpu.roll(bc, sh, 0), 0.0)
            sh *= 2
        mct = jnp.where(lane < ML_HEADS, gates[r0:r0 + CHUNK], bc).T
        for h in range(ML_HEADS):
            q = proj[r0:r0 + CHUNK, h * ML_DH:(h + 1) * ML_DH]
            k = proj[r0:r0 + CHUNK, MIX_W + h * ML_DH:MIX_W + (h + 1) * ML_DH] * (ML_DH ** -0.5)
            v = proj[r0:r0 + CHUNK, 2 * MIX_W + h * ML_DH:2 * MIX_W + (h + 1) * ML_DH]
            qb, kb, vb = q.astype(BF16), k.astype(BF16), v.astype(BF16)
            b_col = bc[:, ML_HEADS + h:ML_HEADS + h + 1]
            b_row = mct[ML_HEADS + h:ML_HEADS + h + 1, :]
            li_row = mct[h:h + 1, :]
            m_prev = m_ref[h][:, 0:1]
            state = state_ref[h]

            d = jnp.where(causal, b_col - b_row + li_row, -jnp.inf)
            inter = b_col + m_prev
            m_t = jnp.maximum(jnp.max(d, axis=-1, keepdims=True), inter)
            s = _dot_nt(qb, kb) * jnp.exp(d - m_t)
            w_inter = jnp.exp(inter - m_t)
            qs = _dot(qb, state.astype(BF16))
            num = _dot(s.astype(BF16), vb) + w_inter * qs[:, 0:ML_DH]
            den = jnp.sum(s, axis=-1, keepdims=True) + w_inter * qs[:, ML_DH:ML_DH + 1]
            hh = num / jnp.maximum(jnp.abs(den), jnp.exp(-m_t))
            mu = jnp.mean(hh, axis=-1, keepdims=True)
            hc = hh - mu
            var = jnp.mean(hc * hc, axis=-1, keepdims=True)
            h_ref[r0:r0 + CHUNK, h * ML_DH:(h + 1) * ML_DH] = hc * lax.rsqrt(var + LN_EPS)

            g = b_row[:, CHUNK - 1:CHUNK]
            a_row = g - b_row + li_row
            m_new = jnp.maximum(g + m_prev, jnp.max(a_row, axis=-1, keepdims=True))
            decay = jnp.exp(g + m_prev - m_new)
            w_row = jnp.exp(a_row - m_new)
            ktw = (k.T * w_row).astype(BF16)
            v_aug = jnp.concatenate([vb, ones_col], axis=-1)
            state_ref[h] = decay * state + _dot(ktw, v_aug)
            m_ref[h] = jnp.broadcast_to(m_new, (1, LANES))

    o_gate = proj[:, 3 * MIX_W:4 * MIX_W]
    h_loc = jax.nn.sigmoid(o_gate) * (h_ref[...] * ng_ref[...])
    q_mem = proj[:, 4 * MIX_W:4 * MIX_W + XATT_W]
    _mixer_tail(x, h_loc, q_mem, tail_refs)


def _mixer_half(kernel_fn, tm, x, nb, seq, mixer_args, kbd, vbd, wout, ln_g, ln_b, rwt, rb,
                scratch, name):
    d = D_MODEL
    n_tok = nb * seq
    spb = seq // tm
    const2 = lambda b, s: (0, 0)
    tri = jnp.triu(jnp.ones((tm, tm), BF16))
    tail_args = [kbd, vbd, wout, ln_g.reshape(1, d), ln_b.reshape(1, d), rwt, rb, tri]
    row_map = lambda b, s: (b * spb + s, 0)
    x_spec = (pl.BlockSpec((tm, d), row_map) if x.shape[1] == d
              else pl.BlockSpec((tm * ROW_TILES, LANES), row_map))
    in_specs = ([x_spec]
                + [pl.BlockSpec(a.shape, const2) for a in mixer_args]
                + [pl.BlockSpec((1,) + kbd.shape[1:], lambda b, s: (b, 0, 0)),
                   pl.BlockSpec((1,) + vbd.shape[1:], lambda b, s: (b, 0, 0))]
                + [pl.BlockSpec(a.shape, const2) for a in tail_args[2:]])
    tok_spec = lambda rows: pl.BlockSpec((rows, tm), lambda b, s: (0, b * spb + s))
    return pl.pallas_call(
        kernel_fn,
        grid=(nb, spb),
        in_specs=in_specs,
        out_specs=[pl.BlockSpec((tm * ROW_TILES, LANES), row_map),
                   tok_spec(SUBLANES), tok_spec(SUBLANES),
                   pl.BlockSpec((CLASS_PAD, LANES), const2)],
        out_shape=[jax.ShapeDtypeStruct((n_tok * ROW_TILES, LANES), F32),
                   jax.ShapeDtypeStruct((SUBLANES, n_tok), F32),
                   jax.ShapeDtypeStruct((SUBLANES, n_tok), I32),
                   jax.ShapeDtypeStruct((CLASS_PAD, LANES), F32)],
        scratch_shapes=[pltpu.VMEM((CLASS_PAD, LANES), F32)] + scratch,
        compiler_params=pltpu.CompilerParams(
            dimension_semantics=("arbitrary", "arbitrary"), vmem_limit_bytes=VMEM_LIMIT),
        name=name,
    )(x, *mixer_args, *tail_args)


def _moe_kernel(order_ref, off_ref, nvalid_ref, trash_ref, e1_ref, e2_ref, nblk_ref,
                x_hbm, g_hbm, wg1_ref, wu1_ref, wd1_ref, wg2_ref, wu2_ref, wd2_ref,
                lng_ref, lnb_ref, out_hbm,
                xbuf, gbuf, obuf, zbuf, acc, bg1, bu1, bd1, bg2, bu2, bd2, gsem, ssem, zsem):
    b = pl.program_id(0)
    nblk = nblk_ref[0]
    slot = b % 2

    def gate_copy(blk, slt):
        return pltpu.make_async_copy(g_hbm.at[pl.ds(off_ref[blk + 1], MOE_ROWS)], gbuf.at[slt],
                                     gsem.at[slt])

    def tile_row(r):
        return r * ROW_TILES if isinstance(r, int) else pl.multiple_of(r * ROW_TILES, ROW_TILES)

    def gather_row(base, r, slt):
        t8 = pl.multiple_of(order_ref[base + r], ROW_TILES)
        pltpu.make_async_copy(x_hbm.at[pl.ds(t8, ROW_TILES)],
                              xbuf.at[slt, pl.ds(tile_row(r), ROW_TILES)], gsem.at[slt]).start()

    def issue_gather(blk, slt, unrolled=True):
        base = off_ref[blk + 1]
        if unrolled:
            for r in range(MOE_ROWS):
                gather_row(base, r, slt)
        else:
            lax.fori_loop(0, MOE_ROWS, lambda r, c: (gather_row(base, r, slt), c)[1], 0)
        gate_copy(blk, slt).start()

    def wait_gather(slt):
        pltpu.make_async_copy(x_hbm.at[pl.ds(0, MOE_ROWS * ROW_TILES)], xbuf.at[slt],
                              gsem.at[slt]).wait()
        gate_copy(0, slt).wait()

    def scatter_row(base, nvalid, trash8, r, slt):
        t8 = jnp.where(r < nvalid, order_ref[base + r], trash8 + r * ROW_TILES)
        pltpu.make_async_copy(
            obuf.at[slt, pl.ds(tile_row(r), ROW_TILES)],
            out_hbm.at[pl.ds(pl.multiple_of(t8, ROW_TILES), ROW_TILES)], ssem.at[slt]).start()

    def issue_scatter(blk, slt, unrolled=True):
        args = (off_ref[blk + 1], nvalid_ref[blk + 1], trash_ref[blk + 1])
        if unrolled:
            for r in range(MOE_ROWS):
                scatter_row(*args, r, slt)
        else:
            lax.fori_loop(0, MOE_ROWS, lambda r, c: (scatter_row(*args, r, slt), c)[1], 0)

    def wait_scatter(slt):
        pltpu.make_async_copy(obuf.at[slt], out_hbm.at[pl.ds(0, MOE_ROWS * ROW_TILES)],
                              ssem.at[slt]).wait()

    def ffn(xb, wg, wu, wd):
        hg = _dot(xb, wg[...])
        hu = _dot(xb, wu[...])
        return _dot((hg * jax.nn.sigmoid(hg) * hu).astype(BF16), wd[...])

    @pl.when(b == 0)
    def _():
        issue_gather(0, 0, unrolled=False)
        obuf[1] = jnp.zeros(obuf.shape[1:], F32)

    @pl.when(b < nblk)
    def _():
        prev = jnp.maximum(b - 1, 0)

        @pl.when((b == 0) | (e1_ref[b] != e1_ref[prev]))
        def _():
            bg1[...] = wg1_ref[0, 0].astype(BF16)
            bu1[...] = wu1_ref[0, 0].astype(BF16)
            bd1[...] = wd1_ref[0, 0].astype(BF16)

        wait_gather(slot)
        xb = _load_rows(xbuf.at[slot], MOE_ROWS).astype(BF16)
        issue_gather(jnp.minimum(b + 1, nblk - 1), 1 - slot)
        acc[...] = gbuf[slot][:, 0:1] * ffn(xb, bg1, bu1, bd1)

        @pl.when((b == 0) | (e2_ref[b] != e2_ref[prev]))
        def _():
            bg2[...] = wg2_ref[0, 0].astype(BF16)
            bu2[...] = wu2_ref[0, 0].astype(BF16)
            bd2[...] = wd2_ref[0, 0].astype(BF16)

        @pl.when(b >= 1)
        def _():
            wait_scatter(slot)

        x = _load_rows(xbuf.at[slot], MOE_ROWS)
        m = acc[...] + gbuf[slot][:, 1:2] * ffn(x.astype(BF16), bg2, bu2, bd2)
        o = _layer_norm(ALPHA * x + m, lng_ref[...], lnb_ref[...])
        issue_scatter(b - 1, 1 - slot)
        _store_rows(obuf.at[slot], o)

    @pl.when(b == nblk)
    def _():
        issue_scatter(nblk - 1, 1 - slot, unrolled=False)

    @pl.when((b >= nblk) & (b < pl.num_programs(0) - 1))
    def _():
        zbuf[...] = jnp.zeros_like(zbuf)
        fill = pltpu.make_async_copy(
            zbuf, out_hbm.at[pl.ds(b * MOE_ROWS * ROW_TILES, MOE_ROWS * ROW_TILES)], zsem)
        fill.start()
        fill.wait()

    @pl.when(b == pl.num_programs(0) - 1)
    def _():
        wait_scatter(0)
        wait_scatter(1)
        wait_gather(nblk % 2)


def _moe_half(layer, x1, plan, gates_sorted, w_gate, w_up, w_down, ln_g, ln_b):
    n_steps = plan[4].shape[0]
    n_rows = n_steps * MOE_ROWS
    d, f = D_MODEL, D_FF
    e1_map = lambda b, order, off, nv, tr, e1, e2, nb: (layer, e1[b], 0, 0)
    e2_map = lambda b, order, off, nv, tr, e1, e2, nb: (layer, e2[b], 0, 0)
    const_spec = pl.BlockSpec((1, d), lambda b, *_: (0, 0))
    any_spec = pl.BlockSpec(memory_space=pl.ANY)
    grid_spec = pltpu.PrefetchScalarGridSpec(
        num_scalar_prefetch=7,
        grid=(n_steps,),
        in_specs=[any_spec, any_spec,
                  pl.BlockSpec((1, 1, d, f), e1_map), pl.BlockSpec((1, 1, d, f), e1_map),
                  pl.BlockSpec((1, 1, f, d), e1_map),
                  pl.BlockSpec((1, 1, d, f), e2_map), pl.BlockSpec((1, 1, d, f), e2_map),
                  pl.BlockSpec((1, 1, f, d), e2_map),
                  const_spec, const_spec],
        out_specs=any_spec,
        scratch_shapes=[
            pltpu.VMEM((2, MOE_ROWS * ROW_TILES, LANES), F32),
            pltpu.VMEM((2, MOE_ROWS, LANES), F32),
            pltpu.VMEM((2, MOE_ROWS * ROW_TILES, LANES), F32),
            pltpu.VMEM((MOE_ROWS * ROW_TILES, LANES), F32),
            pltpu.VMEM((MOE_ROWS, d), F32),
            pltpu.VMEM((d, f), BF16), pltpu.VMEM((d, f), BF16), pltpu.VMEM((f, d), BF16),
            pltpu.VMEM((d, f), BF16), pltpu.VMEM((d, f), BF16), pltpu.VMEM((f, d), BF16),
            pltpu.SemaphoreType.DMA((2,)),
            pltpu.SemaphoreType.DMA((2,)),
            pltpu.SemaphoreType.DMA(()),
        ],
    )
    return pl.pallas_call(
        _moe_kernel,
        grid_spec=grid_spec,
        out_shape=jax.ShapeDtypeStruct((n_rows * ROW_TILES, LANES), F32),
        compiler_params=pltpu.CompilerParams(
            dimension_semantics=("arbitrary",), vmem_limit_bytes=VMEM_LIMIT),
        name="moe_half",
    )(*plan, x1, gates_sorted, w_gate, w_up, w_down, w_gate, w_up, w_down,
      ln_g.reshape(1, d), ln_b.reshape(1, d))


def _plan_blocks(route_f, route_i, counts_f, n_tok):
    cls, rank = route_i[0], route_i[1]
    counts = counts_f[:N_CLASSES, 0].astype(I32)
    padded = (counts + MOE_ROWS - 1) // MOE_ROWS * MOE_ROWS
    pad_end = jnp.cumsum(padded)
    pad_start = pad_end - padded
    start = jnp.cumsum(counts) - counts
    class_ids = jnp.arange(N_CLASSES, dtype=I32)
    dest = jnp.sum(jnp.where(cls[:, None] == class_ids[None], pad_start[None], 0), axis=1) + rank
    _, order, g_lo, g_hi = lax.sort(
        (dest, jnp.arange(n_tok, dtype=I32), route_f[0], route_f[1]), num_keys=1)
    order = jnp.pad(order, (0, MOE_ROWS)) * ROW_TILES
    gates_sorted = jnp.pad(jnp.stack([g_lo, g_hi], axis=1), ((0, MOE_ROWS), (0, LANES - 2)))

    n_blocks = n_tok // MOE_ROWS + N_CLASSES
    blk_ids = jnp.arange(n_blocks + 1, dtype=I32)
    blk_start = blk_ids * MOE_ROWS
    nblk = pad_end[-1] // MOE_ROWS
    blk_cls = jnp.minimum(jnp.sum(pad_end[None] <= blk_start[:, None], axis=1), N_CLASSES - 1)
    blk_cls = jnp.where(blk_ids < nblk, blk_cls, blk_cls[jnp.maximum(nblk - 1, 0)]).astype(I32)
    in_cls = blk_start - pad_start[blk_cls]
    active = blk_ids < nblk
    blk_off = jnp.where(active, start[blk_cls] + in_cls, 0).astype(I32)
    nvalid = jnp.where(active, jnp.clip(counts[blk_cls] - in_cls, 0, MOE_ROWS), 0).astype(I32)
    trash = ((n_tok + blk_start - blk_off - nvalid) * ROW_TILES).astype(I32)
    pair_lo = jnp.array(PAIR_LO, I32)
    pair_hi = jnp.array(PAIR_HI, I32)
    blk_e1 = (blk_cls // N_PAIRS) * EXPERTS_PER_GROUP + pair_lo[blk_cls % N_PAIRS]
    blk_e2 = (blk_cls // N_PAIRS) * EXPERTS_PER_GROUP + pair_hi[blk_cls % N_PAIRS]
    lead = lambda a, v: jnp.concatenate([jnp.full((1,), v, I32), a])
    plan = (order, lead(blk_off, 0), lead(nvalid, 0),
            lead(trash, n_blocks * MOE_ROWS * ROW_TILES),
            blk_e1, blk_e2, nblk.astype(I32).reshape(1))
    return plan, gates_sorted


def kernel(x, mem, conv_w_in, conv_w, mlstm_w_in, mlstm_gate_b, mlstm_norm_g, w_kv_mem, w_out,
           ln1_g, ln1_b, ln2_g, ln2_b, router_w, router_b, w_gate, w_up, w_down):
    nb, seq, d = x.shape
    n_tok = nb * seq
    mem_len = mem.shape[1]
    assert d == D_MODEL and seq % TM_CONV == 0 and seq % TM_MLSTM == 0 and TM_MLSTM % CHUNK == 0

    w_kv_all = jnp.transpose(w_kv_mem, (1, 0, 2)).reshape(d, DEPTH * 2 * XATT_W).astype(BF16)
    kv = _matmul(mem.reshape(nb * mem_len, d), w_kv_all, 512, 1024)
    kv = kv.reshape(nb, mem_len, DEPTH, 2, XATT_HEADS, XATT_DH)
    eye = jnp.eye(XATT_HEADS, dtype=bool)

    perm = lambda a: jnp.swapaxes(a.reshape(N_GROUPS, EXPERTS_PER_GROUP, -1), 0, 1)
    rwt = perm(router_w.T).reshape(N_EXPERTS, d).astype(BF16)
    rb = perm(router_b).reshape(N_EXPERTS, 1)

    xf = x.reshape(n_tok, d)
    for i in range(DEPTH):
        j = i // 2
        k = kv[:, :, i, 0] * (XATT_DH ** -0.5)
        v = kv[:, :, i, 1]
        kbd = jnp.where(eye[None, :, None, :, None],
                        jnp.transpose(k, (0, 2, 3, 1))[:, :, :, None, :], 0.0)
        kbd = kbd.reshape(nb, XATT_W, XATT_HEADS * mem_len).astype(BF16)
        vbd = jnp.where(eye[None, :, None, :, None],
                        jnp.transpose(v, (0, 2, 1, 3))[:, :, :, None, :], 0.0)
        vbd = vbd.reshape(nb, XATT_HEADS * mem_len, XATT_W).astype(BF16)
        wout = w_out[i].astype(BF16)
        if i % 2 == 0:
            win = conv_w_in[j].astype(BF16)
            outs = _mixer_half(_conv_half_kernel, TM_CONV, xf, nb, seq, [win, conv_w[j]],
                               kbd, vbd, wout, ln1_g[i], ln1_b[i], rwt, rb,
                               [pltpu.VMEM((8, MIX_W), F32)], "conv_half")
        else:
            w = mlstm_w_in[j]
            o0 = 4 * MIX_W
            w_gates = jnp.pad(w[:, o0:o0 + 2 * ML_HEADS], ((0, 0), (0, GATE_PAD - 2 * ML_HEADS)))
            win = jnp.concatenate([w[:, :o0], w[:, o0 + 2 * ML_HEADS:], w_gates], axis=1).astype(BF16)
            gate_b = jnp.pad(mlstm_gate_b[j], (0, GATE_PAD - 2 * ML_HEADS)).reshape(1, GATE_PAD)
            norm_g = mlstm_norm_g[j].reshape(1, MIX_W)
            outs = _mixer_half(_mlstm_half_kernel, TM_MLSTM, xf, nb, seq, [win, gate_b, norm_g],
                               kbd, vbd, wout, ln1_g[i], ln1_b[i], rwt, rb,
                               [pltpu.VMEM((ML_HEADS, ML_DH, 2 * ML_DH), F32),
                                pltpu.VMEM((ML_HEADS, 1, LANES), F32),
                                pltpu.VMEM((TM_MLSTM, MIX_W), F32)], "mlstm_half")
        x1, route_f, route_i, counts_f = outs
        plan, gates_sorted = _plan_blocks(route_f, route_i, counts_f, n_tok)
        xf = _moe_half(i, x1, plan, gates_sorted, w_gate, w_up, w_down, ln2_g[i], ln2_b[i])
    return xf[:n_tok * ROW_TILES].reshape(nb, seq, d)
```

```python
import jax
import jax.numpy as jnp
from jax import lax
from jax.experimental import pallas as pl
from jax.experimental.pallas import tpu as pltpu

F32 = jnp.float32
BF16 = jnp.bfloat16
I32 = jnp.int32

D_MODEL = 1024
DEPTH = 4
MIX_W = 768
XATT_HEADS = 4
XATT_DH = 64
XATT_W = XATT_HEADS * XATT_DH
CONV_K = 3
ML_HEADS = 6
ML_DH = MIX_W // ML_HEADS
CHUNK = 128
N_EXPERTS = 32
N_GROUPS = 8
EXPERTS_PER_GROUP = N_EXPERTS // N_GROUPS
D_FF = 512
ALPHA = (2.0 * DEPTH) ** 0.25
LN_EPS = 1e-5

LANES = 128
SUBLANES = 8
ROW_TILES = D_MODEL // LANES
GATE_PAD = LANES
MOE_ROWS = 128
RING = 3
PAIR_LO = (0, 0, 1, 1, 2, 0)
PAIR_HI = (1, 2, 2, 3, 3, 3)
N_PAIRS = len(PAIR_LO)
N_CLASSES = N_GROUPS * N_PAIRS
CLASS_PAD = 64
VMEM_LIMIT = 52 * 1024 * 1024

TM_CONV = 512
TM_MLSTM = 256


def _layer_norm(z, g, b):
    mu = jnp.mean(z, axis=-1, keepdims=True)
    zc = z - mu
    var = jnp.mean(zc * zc, axis=-1, keepdims=True)
    return zc * lax.rsqrt(var + LN_EPS) * g + b


def _load_rows(ref, n):
    return jnp.concatenate([ref[pl.ds(j, n, stride=ROW_TILES), :] for j in range(ROW_TILES)],
                           axis=-1)


def _load_x(x_ref):
    if x_ref.shape[1] == D_MODEL:
        return x_ref[...]
    return _load_rows(x_ref, x_ref.shape[0] // ROW_TILES)


def _store_rows(ref, val):
    for j in range(ROW_TILES):
        ref[pl.ds(j, val.shape[0], stride=ROW_TILES), :] = val[:, j * LANES:(j + 1) * LANES]


def _dot(a, b):
    return jnp.dot(a, b, preferred_element_type=F32)


def _dot_nt(a, b):
    return lax.dot_general(a, b, (((1,), (1,)), ((), ())), preferred_element_type=F32)


def _matmul_kernel(a_ref, b_ref, o_ref):
    o_ref[...] = _dot(a_ref[...].astype(BF16), b_ref[...])


def _matmul(a, b_bf16, tm, tn):
    m, k = a.shape
    n = b_bf16.shape[1]
    return pl.pallas_call(
        _matmul_kernel,
        grid=(m // tm, n // tn),
        in_specs=[pl.BlockSpec((tm, k), lambda i, j: (i, 0)),
                  pl.BlockSpec((k, tn), lambda i, j: (0, j))],
        out_specs=pl.BlockSpec((tm, tn), lambda i, j: (i, j)),
        out_shape=jax.ShapeDtypeStruct((m, n), F32),
        compiler_params=pltpu.CompilerParams(
            dimension_semantics=("arbitrary", "arbitrary"), vmem_limit_bytes=VMEM_LIMIT),
        name="kv_proj",
    )(a, b_bf16)


def _route_tile(x1b, rwt_ref, rb_ref, tri_ref, base_ref, rf_ref, ri_ref, cnt_ref):
    tm = x1b.shape[0]
    scores = jax.nn.sigmoid(_dot_nt(rwt_ref[...], x1b))
    sel = scores + rb_ref[...]
    s = [sel[SUBLANES * j:SUBLANES * (j + 1)] for j in range(EXPERTS_PER_GROUP)]
    p = [scores[SUBLANES * j:SUBLANES * (j + 1)] for j in range(EXPERTS_PER_GROUP)]
    hi1, lo1 = jnp.maximum(s[0], s[1]), jnp.minimum(s[0], s[1])
    hi2, lo2 = jnp.maximum(s[2], s[3]), jnp.minimum(s[2], s[3])
    gs = jnp.maximum(hi1, hi2) + jnp.maximum(jnp.minimum(hi1, hi2), jnp.maximum(lo1, lo2))
    sub = lax.broadcasted_iota(I32, (N_GROUPS, tm), 0)
    gi = jnp.min(jnp.where(gs == jnp.max(gs, axis=0, keepdims=True), sub, N_GROUPS),
                 axis=0, keepdims=True)
    gm = sub == gi
    v = [jnp.sum(jnp.where(gm, sj, 0.0), axis=0, keepdims=True) for sj in s]
    pv = [jnp.sum(jnp.where(gm, pj, 0.0), axis=0, keepdims=True) for pj in p]
    best, i1 = v[0], jnp.zeros((1, tm), I32)
    for j in range(1, EXPERTS_PER_GROUP):
        u = v[j] > best
        best, i1 = jnp.where(u, v[j], best), jnp.where(u, j, i1)
    best2, i2 = jnp.full((1, tm), -jnp.inf, F32), jnp.zeros((1, tm), I32)
    for j in range(EXPERTS_PER_GROUP):
        cand = jnp.where(i1 == j, -jnp.inf, v[j])
        u = cand > best2
        best2, i2 = jnp.where(u, cand, best2), jnp.where(u, j, i2)

    def pick(idx):
        return jnp.where(idx == 0, pv[0], jnp.where(idx == 1, pv[1],
                                                    jnp.where(idx == 2, pv[2], pv[3])))

    p1, p2 = pick(i1), pick(i2)
    tot = p1 + p2
    first_lo = i1 < i2
    g_lo = jnp.where(first_lo, p1, p2) / tot
    g_hi = jnp.where(first_lo, p2, p1) / tot
    lo, hi = jnp.minimum(i1, i2), jnp.maximum(i1, i2)
    pair = jnp.where(hi == 1, 0, jnp.where(hi == 2, jnp.where(lo == 0, 1, 2),
                                           jnp.where(lo == 1, 3, jnp.where(lo == 2, 4, 5))))
    cls = gi * N_PAIRS + pair

    oh = lax.broadcasted_iota(I32, (CLASS_PAD, tm), 0) == cls
    pref = _dot(oh.astype(F32).astype(BF16), tri_ref[...])
    base = base_ref[...]
    rank = jnp.sum(jnp.where(oh, pref + base[:, 0:1] - 1.0, 0.0), axis=0, keepdims=True)
    base = base + pref[:, tm - 1:tm]
    base_ref[...] = base
    cnt_ref[...] = base

    sub8 = lax.broadcasted_iota(I32, (SUBLANES, tm), 0)
    rf_ref[...] = jnp.where(sub8 == 0, g_lo, jnp.where(sub8 == 1, g_hi, 0.0))
    ri_ref[...] = jnp.where(sub8 == 0, cls, jnp.where(sub8 == 1, rank.astype(I32), 0))


def _mixer_tail(x, h_loc, q_mem, tail_refs):
    (kbd_ref, vbd_ref, wout_ref, g_ref, b_ref, rwt_ref, rb_ref, tri_ref,
     x1_ref, rf_ref, ri_ref, cnt_ref, base_ref) = tail_refs

    @pl.when((pl.program_id(0) == 0) & (pl.program_id(1) == 0))
    def _():
        base_ref[...] = jnp.zeros_like(base_ref)

    s = _dot(q_mem.astype(BF16), kbd_ref[0])
    mem_len = s.shape[1] // XATT_HEADS
    ps = []
    for h in range(XATT_HEADS):
        sh = s[:, h * mem_len:(h + 1) * mem_len]
        e = jnp.exp(sh - jnp.max(sh, axis=-1, keepdims=True))
        ps.append((e / jnp.sum(e, axis=-1, keepdims=True)).astype(BF16))
    h_mem = _dot(jnp.concatenate(ps, axis=-1), vbd_ref[0])
    hcat = jnp.concatenate([h_loc.astype(BF16), h_mem.astype(BF16)], axis=-1)
    y = _dot(hcat, wout_ref[...])
    x1 = _layer_norm(ALPHA * x + y, g_ref[...], b_ref[...])
    _store_rows(x1_ref, x1)
    _route_tile(x1.astype(BF16), rwt_ref, rb_ref, tri_ref, base_ref, rf_ref, ri_ref, cnt_ref)


def _conv_half_kernel(x_ref, win_ref, cw_ref, *refs):
    tail_refs, carry_ref = refs[:-1], refs[-1]

    @pl.when(pl.program_id(1) == 0)
    def _():
        carry_ref[...] = jnp.zeros_like(carry_ref)

    x = _load_x(x_ref)
    tm = x.shape[0]
    proj = _dot(x.astype(BF16), win_ref[...])
    bg = proj[:, 0:MIX_W]
    cg = proj[:, MIX_W:2 * MIX_W]
    xin = proj[:, 2 * MIX_W:3 * MIX_W]
    q_mem = proj[:, 3 * MIX_W:]
    u = cg * xin
    prev = carry_ref[...]
    row = lax.broadcasted_iota(I32, u.shape, 0)
    u1 = jnp.where(row == 0, prev[7:8], pltpu.roll(u, 1, 0))
    u2 = jnp.where(row == 0, prev[6:7],
                   jnp.where(row == 1, prev[7:8], pltpu.roll(u, 2, 0)))
    carry_ref[...] = u[tm - 8:tm]
    cw = cw_ref[...]
    c = cw[0:1] * u2 + cw[1:2] * u1 + cw[2:3] * u
    _mixer_tail(x, bg * c, q_mem, tail_refs)


def _log_sigmoid(z):
    return -(jnp.maximum(-z, 0.0) + jnp.log1p(jnp.exp(-jnp.abs(z))))


def _mlstm_half_kernel(x_ref, win_ref, gb_ref, ng_ref, *refs):
    tail_refs, (state_ref, m_ref, h_ref) = refs[:-3], refs[-3:]

    @pl.when(pl.program_id(1) == 0)
    def _():
        state_ref[...] = jnp.zeros_like(state_ref)
        m_ref[...] = jnp.zeros_like(m_ref)

    x = _load_x(x_ref)
    tm = x.shape[0]
    proj = _dot(x.astype(BF16), win_ref[...])
    g0 = 4 * MIX_W + XATT_W
    gates = proj[:, g0:g0 + GATE_PAD] + gb_ref[...]
    logf = _log_sigmoid(gates)
    row = lax.broadcasted_iota(I32, (CHUNK, GATE_PAD), 0)
    lane = lax.broadcasted_iota(I32, (CHUNK, GATE_PAD), 1)
    causal = (lax.broadcasted_iota(I32, (CHUNK, CHUNK), 0)
              >= lax.broadcasted_iota(I32, (CHUNK, CHUNK), 1))
    ones_col = (lax.broadcasted_iota(I32, (CHUNK, ML_DH), 1) == 0).astype(BF16)

    for c in range(tm // CHUNK):
        r0 = c * CHUNK
        bc = logf[r0:r0 + CHUNK]
        sh = 1
        while sh < CHUNK:
            bc = bc + jnp.where(row >= sh, pltpu.roll(bc, sh, 0), 0.0)
            sh *= 2
        mct = jnp.where(lane < ML_HEADS, gates[r0:r0 + CHUNK], bc).T
        for h in range(ML_HEADS):
            q = proj[r0:r0 + CHUNK, h * ML_DH:(h + 1) * ML_DH]
            k = proj[r0:r0 + CHUNK, MIX_W + h * ML_DH:MIX_W + (h + 1) * ML_DH] * (ML_DH ** -0.5)
            v = proj[r0:r0 + CHUNK, 2 * MIX_W + h * ML_DH:2 * MIX_W + (h + 1) * ML_DH]
            qb, kb, vb = q.astype(BF16), k.astype(BF16), v.astype(BF16)
            b_col = bc[:, ML_HEADS + h:ML_HEADS + h + 1]
            b_row = mct[ML_HEADS + h:ML_HEADS + h + 1, :]
            li_row = mct[h:h + 1, :]
            m_prev = m_ref[h][:, 0:1]
            state = state_ref[h]

            d = jnp.where(causal, b_col - b_row + li_row, -jnp.inf)
            inter = b_col + m_prev
            m_t = jnp.maximum(jnp.max(d, axis=-1, keepdims=True), inter)
            s = _dot_nt(qb, kb) * jnp.exp(d - m_t)
            w_inter = jnp.exp(inter - m_t)
            qs = _dot(qb, state.astype(BF16))
            num = _dot(s.astype(BF16), vb) + w_inter * qs[:, 0:ML_DH]
            den = jnp.sum(s, axis=-1, keepdims=True) + w_inter * qs[:, ML_DH:ML_DH + 1]
            hh = num / jnp.maximum(jnp.abs(den), jnp.exp(-m_t))
            mu = jnp.mean(hh, axis=-1, keepdims=True)
            hc = hh - mu
            var = jnp.mean(hc * hc, axis=-1, keepdims=True)
            h_ref[r0:r0 + CHUNK, h * ML_DH:(h + 1) * ML_DH] = hc * lax.rsqrt(var + LN_EPS)

            g = b_row[:, CHUNK - 1:CHUNK]
            a_row = g - b_row + li_row
            m_new = jnp.maximum(g + m_prev, jnp.max(a_row, axis=-1, keepdims=True))
            decay = jnp.exp(g + m_prev - m_new)
            w_row = jnp.exp(a_row - m_new)
            ktw = (k.T * w_row).astype(BF16)
            v_aug = jnp.concatenate([vb, ones_col], axis=-1)
            state_ref[h] = decay * state + _dot(ktw, v_aug)
            m_ref[h] = jnp.broadcast_to(m_new, (1, LANES))

    o_gate = proj[:, 3 * MIX_W:4 * MIX_W]
    h_loc = jax.nn.sigmoid(o_gate) * (h_ref[...] * ng_ref[...])
    q_mem = proj[:, 4 * MIX_W:4 * MIX_W + XATT_W]
    _mixer_tail(x, h_loc, q_mem, tail_refs)


def _mixer_half(kernel_fn, tm, x, nb, seq, mixer_args, kbd, vbd, wout, ln_g, ln_b, rwt, rb,
                scratch, name):
    d = D_MODEL
    n_tok = nb * seq
    spb = seq // tm
    const2 = lambda b, s: (0, 0)
    tri = jnp.triu(jnp.ones((tm, tm), BF16))
    tail_args = [kbd, vbd, wout, ln_g.reshape(1, d), ln_b.reshape(1, d), rwt, rb, tri]
    row_map = lambda b, s: (b * spb + s, 0)
    x_spec = (pl.BlockSpec((tm, d), row_map) if x.shape[1] == d
              else pl.BlockSpec((tm * ROW_TILES, LANES), row_map))
    in_specs = ([x_spec]
                + [pl.BlockSpec(a.shape, const2) for a in mixer_args]
                + [pl.BlockSpec((1,) + kbd.shape[1:], lambda b, s: (b, 0, 0)),
                   pl.BlockSpec((1,) + vbd.shape[1:], lambda b, s: (b, 0, 0))]
                + [pl.BlockSpec(a.shape, const2) for a in tail_args[2:]])
    tok_spec = lambda rows: pl.BlockSpec((rows, tm), lambda b, s: (0, b * spb + s))
    return pl.pallas_call(
        kernel_fn,
        grid=(nb, spb),
        in_specs=in_specs,
        out_specs=[pl.BlockSpec((tm * ROW_TILES, LANES), row_map),
                   tok_spec(SUBLANES), tok_spec(SUBLANES),
                   pl.BlockSpec((CLASS_PAD, LANES), const2)],
        out_shape=[jax.ShapeDtypeStruct((n_tok * ROW_TILES, LANES), F32),
                   jax.ShapeDtypeStruct((SUBLANES, n_tok), F32),
                   jax.ShapeDtypeStruct((SUBLANES, n_tok), I32),
                   jax.ShapeDtypeStruct((CLASS_PAD, LANES), F32)],
        scratch_shapes=[pltpu.VMEM((CLASS_PAD, LANES), F32)] + scratch,
        compiler_params=pltpu.CompilerParams(
            dimension_semantics=("arbitrary", "arbitrary"), vmem_limit_bytes=VMEM_LIMIT),
        name=name,
    )(x, *mixer_args, *tail_args)


def _moe_kernel(order_ref, off_ref, nvalid_ref, trash_ref, e1_ref, e2_ref, nblk_ref,
                x_hbm, g_hbm, wg1_ref, wu1_ref, wd1_ref, wg2_ref, wu2_ref, wd2_ref,
                lng_ref, lnb_ref, out_hbm,
                xbuf, gbuf, obuf, zbuf, acc, bg1, bu1, bd1, bg2, bu2, bd2, gsem, ssem, zsem):
    b = pl.program_id(0)
    nblk = nblk_ref[0]
    slot = b % RING
    ring = lambda j: (j + RING) % RING

    def gate_copy(blk, slt):
        return pltpu.make_async_copy(g_hbm.at[pl.ds(off_ref[blk + 1], MOE_ROWS)], gbuf.at[slt],
                                     gsem.at[slt])

    def tile_row(r):
        return r * ROW_TILES if isinstance(r, int) else pl.multiple_of(r * ROW_TILES, ROW_TILES)

    def gather_row(base, r, slt):
        t8 = pl.multiple_of(order_ref[base + r], ROW_TILES)
        pltpu.make_async_copy(x_hbm.at[pl.ds(t8, ROW_TILES)],
                              xbuf.at[slt, pl.ds(tile_row(r), ROW_TILES)], gsem.at[slt]).start()

    def issue_gather(blk, slt, unrolled=True):
        base = off_ref[blk + 1]
        if unrolled:
            for r in range(MOE_ROWS):
                gather_row(base, r, slt)
        else:
            lax.fori_loop(0, MOE_ROWS, lambda r, c: (gather_row(base, r, slt), c)[1], 0)
        gate_copy(blk, slt).start()

    def wait_gather(slt):
        pltpu.make_async_copy(x_hbm.at[pl.ds(0, MOE_ROWS * ROW_TILES)], xbuf.at[slt],
                              gsem.at[slt]).wait()
        gate_copy(0, slt).wait()

    def scatter_row(base, nvalid, trash8, r, slt):
        t8 = jnp.where(r < nvalid, order_ref[base + r], trash8 + r * ROW_TILES)
        pltpu.make_async_copy(
            obuf.at[slt, pl.ds(tile_row(r), ROW_TILES)],
            out_hbm.at[pl.ds(pl.multiple_of(t8, ROW_TILES), ROW_TILES)], ssem.at[slt]).start()

    def issue_scatter(blk, slt, unrolled=True):
        args = (off_ref[blk + 1], nvalid_ref[blk + 1], trash_ref[blk + 1])
        if unrolled:
            for r in range(MOE_ROWS):
                scatter_row(*args, r, slt)
        else:
            lax.fori_loop(0, MOE_ROWS, lambda r, c: (scatter_row(*args, r, slt), c)[1], 0)

    def wait_scatter(slt):
        pltpu.make_async_copy(obuf.at[slt], out_hbm.at[pl.ds(0, MOE_ROWS * ROW_TILES)],
                              ssem.at[slt]).wait()

    def ffn(xb, wg, wu, wd):
        hg = _dot(xb, wg[...])
        hu = _dot(xb, wu[...])
        return _dot((hg * jax.nn.sigmoid(hg) * hu).astype(BF16), wd[...])

    last = nblk - 1

    @pl.when(b == 0)
    def _():
        issue_gather(0, 0, unrolled=False)
        issue_gather(jnp.minimum(1, last), 1, unrolled=False)
        obuf[RING - 1] = jnp.zeros(obuf.shape[1:], F32)

    @pl.when(b < nblk)
    def _():
        prev = jnp.maximum(b - 1, 0)

        @pl.when((b == 0) | (e1_ref[b] != e1_ref[prev]))
        def _():
            bg1[...] = wg1_ref[0, 0].astype(BF16)
            bu1[...] = wu1_ref[0, 0].astype(BF16)
            bd1[...] = wd1_ref[0, 0].astype(BF16)

        wait_gather(slot)
        xb = _load_rows(xbuf.at[slot], MOE_ROWS).astype(BF16)
        issue_gather(jnp.minimum(b + 2, last), ring(b + 2))
        acc[...] = gbuf[slot][:, 0:1] * ffn(xb, bg1, bu1, bd1)

        @pl.when((b == 0) | (e2_ref[b] != e2_ref[prev]))
        def _():
            bg2[...] = wg2_ref[0, 0].astype(BF16)
            bu2[...] = wu2_ref[0, 0].astype(BF16)
            bd2[...] = wd2_ref[0, 0].astype(BF16)

        @pl.when(b >= RING - 1)
        def _():
            wait_scatter(slot)

        x = _load_rows(xbuf.at[slot], MOE_ROWS)
        m = acc[...] + gbuf[slot][:, 1:2] * ffn(x.astype(BF16), bg2, bu2, bd2)
        o = _layer_norm(ALPHA * x + m, lng_ref[...], lnb_ref[...])
        issue_scatter(b - 1, ring(b - 1))
        _store_rows(obuf.at[slot], o)

    @pl.when(b == nblk)
    def _():
        issue_scatter(last, ring(last), unrolled=False)

    @pl.when((b >= nblk) & (b < pl.num_programs(0) - 1))
    def _():
        zbuf[...] = jnp.zeros_like(zbuf)
        fill = pltpu.make_async_copy(
            zbuf, out_hbm.at[pl.ds(b * MOE_ROWS * ROW_TILES, MOE_ROWS * ROW_TILES)], zsem)
        fill.start()
        fill.wait()

    @pl.when(b == pl.num_programs(0) - 1)
    def _():
        for slt in range(RING):
            wait_scatter(slt)
        wait_gather(ring(nblk))
        wait_gather(ring(nblk + 1))


def _moe_half(layer, x1, plan, gates_sorted, w_gate, w_up, w_down, ln_g, ln_b):
    n_steps = plan[4].shape[0]
    n_rows = n_steps * MOE_ROWS
    d, f = D_MODEL, D_FF
    e1_map = lambda b, order, off, nv, tr, e1, e2, nb: (layer, e1[b], 0, 0)
    e2_map = lambda b, order, off, nv, tr, e1, e2, nb: (layer, e2[b], 0, 0)
    const_spec = pl.BlockSpec((1, d), lambda b, *_: (0, 0))
    any_spec = pl.BlockSpec(memory_space=pl.ANY)
    w_spec = pl.BlockSpec
    grid_spec = pltpu.PrefetchScalarGridSpec(
        num_scalar_prefetch=7,
        grid=(n_steps,),
        in_specs=[any_spec, any_spec,
                  w_spec((1, 1, d, f), e1_map), w_spec((1, 1, d, f), e1_map),
                  w_spec((1, 1, f, d), e1_map),
                  w_spec((1, 1, d, f), e2_map), w_spec((1, 1, d, f), e2_map),
                  w_spec((1, 1, f, d), e2_map),
                  const_spec, const_spec],
        out_specs=any_spec,
        scratch_shapes=[
            pltpu.VMEM((RING, MOE_ROWS * ROW_TILES, LANES), F32),
            pltpu.VMEM((RING, MOE_ROWS, LANES), F32),
            pltpu.VMEM((RING, MOE_ROWS * ROW_TILES, LANES), F32),
            pltpu.VMEM((MOE_ROWS * ROW_TILES, LANES), F32),
            pltpu.VMEM((MOE_ROWS, d), F32),
            pltpu.VMEM((d, f), BF16), pltpu.VMEM((d, f), BF16), pltpu.VMEM((f, d), BF16),
            pltpu.VMEM((d, f), BF16), pltpu.VMEM((d, f), BF16), pltpu.VMEM((f, d), BF16),
            pltpu.SemaphoreType.DMA((RING,)),
            pltpu.SemaphoreType.DMA((RING,)),
            pltpu.SemaphoreType.DMA(()),
        ],
    )
    return pl.pallas_call(
        _moe_kernel,
        grid_spec=grid_spec,
        out_shape=jax.ShapeDtypeStruct((n_rows * ROW_TILES, LANES), F32),
        compiler_params=pltpu.CompilerParams(
            dimension_semantics=("arbitrary",), vmem_limit_bytes=VMEM_LIMIT),
        name="moe_half",
    )(*plan, x1, gates_sorted, w_gate, w_up, w_down, w_gate, w_up, w_down,
      ln_g.reshape(1, d), ln_b.reshape(1, d))


def _plan_blocks(route_f, route_i, counts_f, n_tok):
    cls, rank = route_i[0], route_i[1]
    counts = counts_f[:N_CLASSES, 0].astype(I32)
    padded = (counts + MOE_ROWS - 1) // MOE_ROWS * MOE_ROWS
    pad_end = jnp.cumsum(padded)
    pad_start = pad_end - padded
    start = jnp.cumsum(counts) - counts
    class_ids = jnp.arange(N_CLASSES, dtype=I32)
    dest = jnp.sum(jnp.where(cls[:, None] == class_ids[None], pad_start[None], 0), axis=1) + rank
    _, order, g_lo, g_hi = lax.sort(
        (dest, jnp.arange(n_tok, dtype=I32), route_f[0], route_f[1]), num_keys=1)
    order = jnp.pad(order, (0, MOE_ROWS)) * ROW_TILES
    gates_sorted = jnp.pad(jnp.stack([g_lo, g_hi], axis=1), ((0, MOE_ROWS), (0, LANES - 2)))

    n_blocks = n_tok // MOE_ROWS + N_CLASSES
    blk_ids = jnp.arange(n_blocks + 1, dtype=I32)
    blk_start = blk_ids * MOE_ROWS
    nblk = pad_end[-1] // MOE_ROWS
    blk_cls = jnp.minimum(jnp.sum(pad_end[None] <= blk_start[:, None], axis=1), N_CLASSES - 1)
    blk_cls = jnp.where(blk_ids < nblk, blk_cls, blk_cls[jnp.maximum(nblk - 1, 0)]).astype(I32)
    in_cls = blk_start - pad_start[blk_cls]
    active = blk_ids < nblk
    blk_off = jnp.where(active, start[blk_cls] + in_cls, 0).astype(I32)
    nvalid = jnp.where(active, jnp.clip(counts[blk_cls] - in_cls, 0, MOE_ROWS), 0).astype(I32)
    trash = ((n_tok + blk_start - blk_off - nvalid) * ROW_TILES).astype(I32)
    pair_lo = jnp.array(PAIR_LO, I32)
    pair_hi = jnp.array(PAIR_HI, I32)
    blk_e1 = (blk_cls // N_PAIRS) * EXPERTS_PER_GROUP + pair_lo[blk_cls % N_PAIRS]
    blk_e2 = (blk_cls // N_PAIRS) * EXPERTS_PER_GROUP + pair_hi[blk_cls % N_PAIRS]
    lead = lambda a, v: jnp.concatenate([jnp.full((1,), v, I32), a])
    plan = (order, lead(blk_off, 0), lead(nvalid, 0),
            lead(trash, n_blocks * MOE_ROWS * ROW_TILES),
            blk_e1, blk_e2, nblk.astype(I32).reshape(1))
    return plan, gates_sorted


def kernel(x, mem, conv_w_in, conv_w, mlstm_w_in, mlstm_gate_b, mlstm_norm_g, w_kv_mem, w_out,
           ln1_g, ln1_b, ln2_g, ln2_b, router_w, router_b, w_gate, w_up, w_down):
    nb, seq, d = x.shape
    n_tok = nb * seq
    mem_len = mem.shape[1]
    assert d == D_MODEL and seq % TM_CONV == 0 and seq % TM_MLSTM == 0 and TM_MLSTM % CHUNK == 0
    assert n_tok // MOE_ROWS >= RING

    w_kv_all = jnp.transpose(w_kv_mem, (1, 0, 2)).reshape(d, DEPTH * 2 * XATT_W).astype(BF16)
    kv = _matmul(mem.reshape(nb * mem_len, d), w_kv_all, 512, 1024)
    kv = kv.reshape(nb, mem_len, DEPTH, 2, XATT_HEADS, XATT_DH)
    eye = jnp.eye(XATT_HEADS, dtype=bool)

    perm = lambda a: jnp.swapaxes(a.reshape(N_GROUPS, EXPERTS_PER_GROUP, -1), 0, 1)
    rwt = perm(router_w.T).reshape(N_EXPERTS, d).astype(BF16)
    rb = perm(router_b).reshape(N_EXPERTS, 1)

    xf = x.reshape(n_tok, d)
    for i in range(DEPTH):
        j = i // 2
        k = kv[:, :, i, 0] * (XATT_DH ** -0.5)
        v = kv[:, :, i, 1]
        kbd = jnp.where(eye[None, :, None, :, None],
                        jnp.transpose(k, (0, 2, 3, 1))[:, :, :, None, :], 0.0)
        kbd = kbd.reshape(nb, XATT_W, XATT_HEADS * mem_len).astype(BF16)
        vbd = jnp.where(eye[None, :, None, :, None],
                        jnp.transpose(v, (0, 2, 1, 3))[:, :, :, None, :], 0.0)
        vbd = vbd.reshape(nb, XATT_HEADS * mem_len, XATT_W).astype(BF16)
        wout = w_out[i].astype(BF16)
        if i % 2 == 0:
            win = conv_w_in[j].astype(BF16)
            outs = _mixer_half(_conv_half_kernel, TM_CONV, xf, nb, seq, [win, conv_w[j]],
                               kbd, vbd, wout, ln1_g[i], ln1_b[i], rwt, rb,
                               [pltpu.VMEM((8, MIX_W), F32)], "conv_half")
        else:
            w = mlstm_w_in[j]
            o0 = 4 * MIX_W
            w_gates = jnp.pad(w[:, o0:o0 + 2 * ML_HEADS], ((0, 0), (0, GATE_PAD - 2 * ML_HEADS)))
            win = jnp.concatenate([w[:, :o0], w[:, o0 + 2 * ML_HEADS:], w_gates], axis=1).astype(BF16)
            gate_b = jnp.pad(mlstm_gate_b[j], (0, GATE_PAD - 2 * ML_HEADS)).reshape(1, GATE_PAD)
            norm_g = mlstm_norm_g[j].reshape(1, MIX_W)
            outs = _mixer_half(_mlstm_half_kernel, TM_MLSTM, xf, nb, seq, [win, gate_b, norm_g],
                               kbd, vbd, wout, ln1_g[i], ln1_b[i], rwt, rb,
                               [pltpu.VMEM((ML_HEADS, ML_DH, 2 * ML_DH), F32),
                                pltpu.VMEM((ML_HEADS, 1, LANES), F32),
                                pltpu.VMEM((TM_MLSTM, MIX_W), F32)], "mlstm_half")
        x1, route_f, route_i, counts_f = outs
        plan, gates_sorted = _plan_blocks(route_f, route_i, counts_f, n_tok)
        xf = _moe_half(i, x1, plan, gates_sorted, w_gate, w_up, w_down, ln2_g[i], ln2_b[i])
    return xf[:n_tok * ROW_TILES].reshape(nb, seq, d)
```

```python
import functools

import jax
import jax.numpy as jnp
from jax import lax
from jax.experimental import pallas as pl
from jax.experimental.pallas import tpu as pltpu

F32 = jnp.float32
BF16 = jnp.bfloat16
I32 = jnp.int32

D_MODEL = 1024
DEPTH = 4
MIX_W = 768
XATT_HEADS = 4
XATT_DH = 64
XATT_W = XATT_HEADS * XATT_DH
CONV_K = 3
ML_HEADS = 6
ML_DH = MIX_W // ML_HEADS
CHUNK = 128
N_EXPERTS = 32
N_GROUPS = 8
EXPERTS_PER_GROUP = N_EXPERTS // N_GROUPS
D_FF = 512
ALPHA = (2.0 * DEPTH) ** 0.25
LN_EPS = 1e-5

LANES = 128
SUBLANES = 8
ROW_TILES = D_MODEL // LANES
GATE_PAD = LANES
MOE_ROWS = 128
RING = 3
W_SETS = 2
W_STAGE = 2
PAIR_LO = (0, 0, 1, 1, 2, 0)
PAIR_HI = (1, 2, 2, 3, 3, 3)
N_PAIRS = len(PAIR_LO)
N_CLASSES = N_GROUPS * N_PAIRS
CLASS_PAD = 64
VMEM_LIMIT = 52 * 1024 * 1024

TM_CONV = 512
TM_MLSTM = 256


def _layer_norm(z, g, b):
    mu = jnp.mean(z, axis=-1, keepdims=True)
    zc = z - mu
    var = jnp.mean(zc * zc, axis=-1, keepdims=True)
    return zc * lax.rsqrt(var + LN_EPS) * g + b


def _load_rows(ref, n):
    return jnp.concatenate([ref[pl.ds(j, n, stride=ROW_TILES), :] for j in range(ROW_TILES)],
                           axis=-1)


def _load_x(x_ref):
    if x_ref.shape[1] == D_MODEL:
        return x_ref[...]
    return _load_rows(x_ref, x_ref.shape[0] // ROW_TILES)


def _store_rows(ref, val):
    for j in range(ROW_TILES):
        ref[pl.ds(j, val.shape[0], stride=ROW_TILES), :] = val[:, j * LANES:(j + 1) * LANES]


def _dot(a, b):
    return jnp.dot(a, b, preferred_element_type=F32)


def _dot_nt(a, b):
    return lax.dot_general(a, b, (((1,), (1,)), ((), ())), preferred_element_type=F32)


def _matmul_kernel(a_ref, b_ref, o_ref):
    o_ref[...] = _dot(a_ref[...].astype(BF16), b_ref[...])


def _matmul(a, b_bf16, tm, tn):
    m, k = a.shape
    n = b_bf16.shape[1]
    return pl.pallas_call(
        _matmul_kernel,
        grid=(m // tm, n // tn),
        in_specs=[pl.BlockSpec((tm, k), lambda i, j: (i, 0)),
                  pl.BlockSpec((k, tn), lambda i, j: (0, j))],
        out_specs=pl.BlockSpec((tm, tn), lambda i, j: (i, j)),
        out_shape=jax.ShapeDtypeStruct((m, n), F32),
        compiler_params=pltpu.CompilerParams(
            dimension_semantics=("arbitrary", "arbitrary"), vmem_limit_bytes=VMEM_LIMIT),
        name="kv_proj",
    )(a, b_bf16)


def _route_tile(x1b, rwt_ref, rb_ref, tri_ref, base_ref, rf_ref, ri_ref, cnt_ref):
    tm = x1b.shape[0]
    scores = jax.nn.sigmoid(_dot_nt(rwt_ref[...], x1b))
    sel = scores + rb_ref[...]
    s = [sel[SUBLANES * j:SUBLANES * (j + 1)] for j in range(EXPERTS_PER_GROUP)]
    p = [scores[SUBLANES * j:SUBLANES * (j + 1)] for j in range(EXPERTS_PER_GROUP)]
    hi1, lo1 = jnp.maximum(s[0], s[1]), jnp.minimum(s[0], s[1])
    hi2, lo2 = jnp.maximum(s[2], s[3]), jnp.minimum(s[2], s[3])
    gs = jnp.maximum(hi1, hi2) + jnp.maximum(jnp.minimum(hi1, hi2), jnp.maximum(lo1, lo2))
    sub = lax.broadcasted_iota(I32, (N_GROUPS, tm), 0)
    gi = jnp.min(jnp.where(gs == jnp.max(gs, axis=0, keepdims=True), sub, N_GROUPS),
                 axis=0, keepdims=True)
    gm = sub == gi
    v = [jnp.sum(jnp.where(gm, sj, 0.0), axis=0, keepdims=True) for sj in s]
    pv = [jnp.sum(jnp.where(gm, pj, 0.0), axis=0, keepdims=True) for pj in p]
    best, i1 = v[0], jnp.zeros((1, tm), I32)
    for j in range(1, EXPERTS_PER_GROUP):
        u = v[j] > best
        best, i1 = jnp.where(u, v[j], best), jnp.where(u, j, i1)
    best2, i2 = jnp.full((1, tm), -jnp.inf, F32), jnp.zeros((1, tm), I32)
    for j in range(EXPERTS_PER_GROUP):
        cand = jnp.where(i1 == j, -jnp.inf, v[j])
        u = cand > best2
        best2, i2 = jnp.where(u, cand, best2), jnp.where(u, j, i2)

    def pick(idx):
        return jnp.where(idx == 0, pv[0], jnp.where(idx == 1, pv[1],
                                                    jnp.where(idx == 2, pv[2], pv[3])))

    p1, p2 = pick(i1), pick(i2)
    tot = p1 + p2
    first_lo = i1 < i2
    g_lo = jnp.where(first_lo, p1, p2) / tot
    g_hi = jnp.where(first_lo, p2, p1) / tot
    lo, hi = jnp.minimum(i1, i2), jnp.maximum(i1, i2)
    pair = jnp.where(hi == 1, 0, jnp.where(hi == 2, jnp.where(lo == 0, 1, 2),
                                           jnp.where(lo == 1, 3, jnp.where(lo == 2, 4, 5))))
    cls = gi * N_PAIRS + pair

    oh = lax.broadcasted_iota(I32, (CLASS_PAD, tm), 0) == cls
    pref = _dot(oh.astype(F32).astype(BF16), tri_ref[...])
    base = base_ref[...]
    rank = jnp.sum(jnp.where(oh, pref + base[:, 0:1] - 1.0, 0.0), axis=0, keepdims=True)
    base = base + pref[:, tm - 1:tm]
    base_ref[...] = base
    cnt_ref[...] = base

    sub8 = lax.broadcasted_iota(I32, (SUBLANES, tm), 0)
    rf_ref[...] = jnp.where(sub8 == 0, g_lo, jnp.where(sub8 == 1, g_hi, 0.0))
    ri_ref[...] = jnp.where(sub8 == 0, cls, jnp.where(sub8 == 1, rank.astype(I32), 0))


def _mixer_tail(x, h_loc, q_mem, tail_refs):
    (kbd_ref, vbd_ref, wout_ref, g_ref, b_ref, rwt_ref, rb_ref, tri_ref,
     x1_ref, rf_ref, ri_ref, cnt_ref, base_ref) = tail_refs

    @pl.when((pl.program_id(0) == 0) & (pl.program_id(1) == 0))
    def _():
        base_ref[...] = jnp.zeros_like(base_ref)

    s = _dot(q_mem.astype(BF16), kbd_ref[0])
    mem_len = s.shape[1] // XATT_HEADS
    ps = []
    for h in range(XATT_HEADS):
        sh = s[:, h * mem_len:(h + 1) * mem_len]
        e = jnp.exp(sh - jnp.max(sh, axis=-1, keepdims=True))
        ps.append((e / jnp.sum(e, axis=-1, keepdims=True)).astype(BF16))
    h_mem = _dot(jnp.concatenate(ps, axis=-1), vbd_ref[0])
    hcat = jnp.concatenate([h_loc.astype(BF16), h_mem.astype(BF16)], axis=-1)
    y = _dot(hcat, wout_ref[...])
    x1 = _layer_norm(ALPHA * x + y, g_ref[...], b_ref[...])
    _store_rows(x1_ref, x1)
    _route_tile(x1.astype(BF16), rwt_ref, rb_ref, tri_ref, base_ref, rf_ref, ri_ref, cnt_ref)


def _conv_half_kernel(x_ref, win_ref, cw_ref, *refs):
    tail_refs, carry_ref = refs[:-1], refs[-1]

    @pl.when(pl.program_id(1) == 0)
    def _():
        carry_ref[...] = jnp.zeros_like(carry_ref)

    x = _load_x(x_ref)
    tm = x.shape[0]
    proj = _dot(x.astype(BF16), win_ref[...])
    bg = proj[:, 0:MIX_W]
    cg = proj[:, MIX_W:2 * MIX_W]
    xin = proj[:, 2 * MIX_W:3 * MIX_W]
    q_mem = proj[:, 3 * MIX_W:]
    u = cg * xin
    prev = carry_ref[...]
    row = lax.broadcasted_iota(I32, u.shape, 0)
    u1 = jnp.where(row == 0, prev[7:8], pltpu.roll(u, 1, 0))
    u2 = jnp.where(row == 0, prev[6:7],
                   jnp.where(row == 1, prev[7:8], pltpu.roll(u, 2, 0)))
    carry_ref[...] = u[tm - 8:tm]
    cw = cw_ref[...]
    c = cw[0:1] * u2 + cw[1:2] * u1 + cw[2:3] * u
    _mixer_tail(x, bg * c, q_mem, tail_refs)


def _log_sigmoid(z):
    return -(jnp.maximum(-z, 0.0) + jnp.log1p(jnp.exp(-jnp.abs(z))))


def _mlstm_half_kernel(x_ref, win_ref, gb_ref, ng_ref, *refs):
    tail_refs, (state_ref, m_ref, h_ref) = refs[:-3], refs[-3:]

    @pl.when(pl.program_id(1) == 0)
    def _():
        state_ref[...] = jnp.zeros_like(state_ref)
        m_ref[...] = jnp.zeros_like(m_ref)

    x = _load_x(x_ref)
    tm = x.shape[0]
    proj = _dot(x.astype(BF16), win_ref[...])
    g0 = 4 * MIX_W + XATT_W
    gates = proj[:, g0:g0 + GATE_PAD] + gb_ref[...]
    logf = _log_sigmoid(gates)
    row = lax.broadcasted_iota(I32, (CHUNK, GATE_PAD), 0)
    lane = lax.broadcasted_iota(I32, (CHUNK, GATE_PAD), 1)
    causal = (lax.broadcasted_iota(I32, (CHUNK, CHUNK), 0)
              >= lax.broadcasted_iota(I32, (CHUNK, CHUNK), 1))
    ones_col = (lax.broadcasted_iota(I32, (CHUNK, ML_DH), 1) == 0).astype(BF16)

    for c in range(tm // CHUNK):
        r0 = c * CHUNK
        bc = logf[r0:r0 + CHUNK]
        sh = 1
        while sh < CHUNK:
            bc = bc + jnp.where(row >= sh, pltpu.roll(bc, sh, 0), 0.0)
            sh *= 2
        mct = jnp.where(lane < ML_HEADS, gates[r0:r0 + CHUNK], bc).T
        for h in range(ML_HEADS):
            q = proj[r0:r0 + CHUNK, h * ML_DH:(h + 1) * ML_DH]
            k = proj[r0:r0 + CHUNK, MIX_W + h * ML_DH:MIX_W + (h + 1) * ML_DH] * (ML_DH ** -0.5)
            v = proj[r0:r0 + CHUNK, 2 * MIX_W + h * ML_DH:2 * MIX_W + (h + 1) * ML_DH]
            qb, kb, vb = q.astype(BF16), k.astype(BF16), v.astype(BF16)
            b_col = bc[:, ML_HEADS + h:ML_HEADS + h + 1]
            b_row = mct[ML_HEADS + h:ML_HEADS + h + 1, :]
            li_row = mct[h:h + 1, :]
            m_prev = m_ref[h][:, 0:1]
            state = state_ref[h]

            d = jnp.where(causal, b_col - b_row + li_row, -jnp.inf)
            inter = b_col + m_prev
            m_t = jnp.maximum(jnp.max(d, axis=-1, keepdims=True), inter)
            s = _dot_nt(qb, kb) * jnp.exp(d - m_t)
            w_inter = jnp.exp(inter - m_t)
            qs = _dot(qb, state.astype(BF16))
            num = _dot(s.astype(BF16), vb) + w_inter * qs[:, 0:ML_DH]
            den = jnp.sum(s, axis=-1, keepdims=True) + w_inter * qs[:, ML_DH:ML_DH + 1]
            hh = num / jnp.maximum(jnp.abs(den), jnp.exp(-m_t))
            mu = jnp.mean(hh, axis=-1, keepdims=True)
            hc = hh - mu
            var = jnp.mean(hc * hc, axis=-1, keepdims=True)
            h_ref[r0:r0 + CHUNK, h * ML_DH:(h + 1) * ML_DH] = hc * lax.rsqrt(var + LN_EPS)

            g = b_row[:, CHUNK - 1:CHUNK]
            a_row = g - b_row + li_row
            m_new = jnp.maximum(g + m_prev, jnp.max(a_row, axis=-1, keepdims=True))
            decay = jnp.exp(g + m_prev - m_new)
            w_row = jnp.exp(a_row - m_new)
            ktw = (k.T * w_row).astype(BF16)
            v_aug = jnp.concatenate([vb, ones_col], axis=-1)
            state_ref[h] = decay * state + _dot(ktw, v_aug)
            m_ref[h] = jnp.broadcast_to(m_new, (1, LANES))

    o_gate = proj[:, 3 * MIX_W:4 * MIX_W]
    h_loc = jax.nn.sigmoid(o_gate) * (h_ref[...] * ng_ref[...])
    q_mem = proj[:, 4 * MIX_W:4 * MIX_W + XATT_W]
    _mixer_tail(x, h_loc, q_mem, tail_refs)


def _mixer_half(kernel_fn, tm, x, nb, seq, mixer_args, kbd, vbd, wout, ln_g, ln_b, rwt, rb,
                scratch, name):
    d = D_MODEL
    n_tok = nb * seq
    spb = seq // tm
    const2 = lambda b, s: (0, 0)
    tri = jnp.triu(jnp.ones((tm, tm), BF16))
    tail_args = [kbd, vbd, wout, ln_g.reshape(1, d), ln_b.reshape(1, d), rwt, rb, tri]
    row_map = lambda b, s: (b * spb + s, 0)
    x_spec = (pl.BlockSpec((tm, d), row_map) if x.shape[1] == d
              else pl.BlockSpec((tm * ROW_TILES, LANES), row_map))
    in_specs = ([x_spec]
                + [pl.BlockSpec(a.shape, const2) for a in mixer_args]
                + [pl.BlockSpec((1,) + kbd.shape[1:], lambda b, s: (b, 0, 0)),
                   pl.BlockSpec((1,) + vbd.shape[1:], lambda b, s: (b, 0, 0))]
                + [pl.BlockSpec(a.shape, const2) for a in tail_args[2:]])
    tok_spec = lambda rows: pl.BlockSpec((rows, tm), lambda b, s: (0, b * spb + s))
    return pl.pallas_call(
        kernel_fn,
        grid=(nb, spb),
        in_specs=in_specs,
        out_specs=[pl.BlockSpec((tm * ROW_TILES, LANES), row_map),
                   tok_spec(SUBLANES), tok_spec(SUBLANES),
                   pl.BlockSpec((CLASS_PAD, LANES), const2)],
        out_shape=[jax.ShapeDtypeStruct((n_tok * ROW_TILES, LANES), F32),
                   jax.ShapeDtypeStruct((SUBLANES, n_tok), F32),
                   jax.ShapeDtypeStruct((SUBLANES, n_tok), I32),
                   jax.ShapeDtypeStruct((CLASS_PAD, LANES), F32)],
        scratch_shapes=[pltpu.VMEM((CLASS_PAD, LANES), F32)] + scratch,
        compiler_params=pltpu.CompilerParams(
            dimension_semantics=("arbitrary", "arbitrary"), vmem_limit_bytes=VMEM_LIMIT),
        name=name,
    )(x, *mixer_args, *tail_args)


def _moe_kernel(layer, n_blocks,
                order_ref, off_ref, nvalid_ref, trash_ref, e1_ref, e2_ref, nblk_ref,
                x_hbm, g_hbm, wg_hbm, wu_hbm, wd_hbm, lng_ref, lnb_ref, out_hbm,
                xbuf, gbuf, obuf, zbuf, acc, sg, su, sd, wg_bf, wu_bf, wd_bf,
                gsem, ssem, zsem, wsem):
    nblk = nblk_ref[0]
    last = nblk - 1
    ring = lambda j: (j + RING) % RING
    n_res = W_SETS * EXPERTS_PER_GROUP

    def gate_copy(blk, slt):
        return pltpu.make_async_copy(g_hbm.at[pl.ds(off_ref[blk + 1], MOE_ROWS)], gbuf.at[slt],
                                     gsem.at[slt])

    def tile_row(r):
        return r * ROW_TILES if isinstance(r, int) else pl.multiple_of(r * ROW_TILES, ROW_TILES)

    def gather_row(base, r, slt):
        t8 = pl.multiple_of(order_ref[base + r], ROW_TILES)
        pltpu.make_async_copy(x_hbm.at[pl.ds(t8, ROW_TILES)],
                              xbuf.at[slt, pl.ds(tile_row(r), ROW_TILES)], gsem.at[slt]).start()

    def issue_gather(blk, slt, unrolled=True):
        base = off_ref[blk + 1]
        if unrolled:
            for r in range(MOE_ROWS):
                gather_row(base, r, slt)
        else:
            lax.fori_loop(0, MOE_ROWS, lambda r, c: (gather_row(base, r, slt), c)[1], 0)
        gate_copy(blk, slt).start()

    def wait_gather(slt):
        pltpu.make_async_copy(x_hbm.at[pl.ds(0, MOE_ROWS * ROW_TILES)], xbuf.at[slt],
                              gsem.at[slt]).wait()
        gate_copy(0, slt).wait()

    def scatter_row(base, nvalid, trash8, r, slt):
        t8 = jnp.where(r < nvalid, order_ref[base + r], trash8 + r * ROW_TILES)
        pltpu.make_async_copy(
            obuf.at[slt, pl.ds(tile_row(r), ROW_TILES)],
            out_hbm.at[pl.ds(pl.multiple_of(t8, ROW_TILES), ROW_TILES)], ssem.at[slt]).start()

    def issue_scatter(blk, slt, unrolled=True):
        args = (off_ref[blk + 1], nvalid_ref[blk + 1], trash_ref[blk + 1])
        if unrolled:
            for r in range(MOE_ROWS):
                scatter_row(*args, r, slt)
        else:
            lax.fori_loop(0, MOE_ROWS, lambda r, c: (scatter_row(*args, r, slt), c)[1], 0)

    def wait_scatter(slt):
        pltpu.make_async_copy(obuf.at[slt], out_hbm.at[pl.ds(0, MOE_ROWS * ROW_TILES)],
                              ssem.at[slt]).wait()

    def weight_copies(k, stage):
        return [pltpu.make_async_copy(src.at[layer, k], dst.at[stage], wsem.at[stage])
                for src, dst in ((wg_hbm, sg), (wu_hbm, su), (wd_hbm, sd))]

    def load_start(k):
        for cp in weight_copies(k, k % W_STAGE):
            cp.start()

    def load_wait(k):
        for cp in weight_copies(k, k % W_STAGE):
            cp.wait()

    def load_finish(k):
        load_wait(k)
        stage, res = k % W_STAGE, k % n_res
        wg_bf[res] = sg[stage].astype(BF16)
        wu_bf[res] = su[stage].astype(BF16)
        wd_bf[res] = sd[stage].astype(BF16)

    def start_if(pred, kd):
        @pl.when(pred)
        def _():
            load_start(kd)
        return kd + pred.astype(I32)

    def service_loader(group, kd, kc):
        cap = jnp.minimum((group + W_SETS) * EXPERTS_PER_GROUP, N_EXPERTS)
        need = (group + 1) * EXPERTS_PER_GROUP

        def catch_up(c):
            kd, kc = c
            kd = start_if(kd <= kc, kd)
            kd = start_if((kd < kc + W_STAGE) & (kd < cap), kd)
            load_finish(kc)
            return kd, kc + 1
        kd, kc = lax.while_loop(lambda c: c[1] < need, catch_up, (kd, kc))

        ready = (kc < kd) & (kc < cap)

        @pl.when(ready)
        def _():
            load_finish(kc)
        kc = kc + ready.astype(I32)
        kd = start_if((kd < kc + W_STAGE) & (kd < cap), kd)
        return kd, kc

    def ffn(xb, res):
        hg = _dot(xb, wg_bf[res])
        hu = _dot(xb, wu_bf[res])
        return _dot((hg * jax.nn.sigmoid(hg) * hu).astype(BF16), wd_bf[res])

    issue_gather(0, 0, unrolled=False)
    issue_gather(jnp.minimum(1, last), 1, unrolled=False)
    obuf[RING - 1] = jnp.zeros(obuf.shape[1:], F32)

    def block(b, loader):
        slot = b % RING
        e1, e2 = e1_ref[b], e2_ref[b]
        loader = service_loader(e1 // EXPERTS_PER_GROUP, *loader)

        wait_gather(slot)
        xb = _load_rows(xbuf.at[slot], MOE_ROWS).astype(BF16)
        issue_gather(jnp.minimum(b + 2, last), ring(b + 2))
        acc[...] = gbuf[slot][:, 0:1] * ffn(xb, e1 % n_res)

        @pl.when(b >= RING - 1)
        def _():
            wait_scatter(slot)

        x = _load_rows(xbuf.at[slot], MOE_ROWS)
        m = acc[...] + gbuf[slot][:, 1:2] * ffn(x.astype(BF16), e2 % n_res)
        o = _layer_norm(ALPHA * x + m, lng_ref[...], lnb_ref[...])
        issue_scatter(b - 1, ring(b - 1))
        _store_rows(obuf.at[slot], o)
        return loader

    first = (e1_ref[0] // EXPERTS_PER_GROUP) * EXPERTS_PER_GROUP
    kd, kc = lax.fori_loop(0, nblk, block, (first, first))

    lax.fori_loop(kc, kd, lambda k, c: (load_wait(k), c)[1], 0)
    issue_scatter(last, ring(last), unrolled=False)
    for slt in range(RING):
        wait_scatter(slt)
    wait_gather(ring(nblk))
    wait_gather(ring(nblk + 1))

    zbuf[...] = jnp.zeros_like(zbuf)

    def fill_copy(j):
        return pltpu.make_async_copy(
            zbuf, out_hbm.at[pl.ds(j * MOE_ROWS * ROW_TILES, MOE_ROWS * ROW_TILES)], zsem)
    lax.fori_loop(nblk, n_blocks, lambda j, c: (fill_copy(j).start(), c)[1], 0)
    lax.fori_loop(nblk, n_blocks, lambda j, c: (fill_copy(j).wait(), c)[1], 0)


def _moe_half(layer, x1, plan, gates_sorted, w_gate, w_up, w_down, ln_g, ln_b):
    n_blocks = plan[4].shape[0]
    n_rows = (n_blocks + 1) * MOE_ROWS
    d, f = D_MODEL, D_FF
    n_res = W_SETS * EXPERTS_PER_GROUP
    const_spec = pl.BlockSpec((1, d), lambda i, *_: (0, 0))
    any_spec = pl.BlockSpec(memory_space=pl.ANY)
    grid_spec = pltpu.PrefetchScalarGridSpec(
        num_scalar_prefetch=7,
        grid=(1,),
        in_specs=[any_spec, any_spec, any_spec, any_spec, any_spec, const_spec, const_spec],
        out_specs=any_spec,
        scratch_shapes=[
            pltpu.VMEM((RING, MOE_ROWS * ROW_TILES, LANES), F32),
            pltpu.VMEM((RING, MOE_ROWS, LANES), F32),
            pltpu.VMEM((RING, MOE_ROWS * ROW_TILES, LANES), F32),
            pltpu.VMEM((MOE_ROWS * ROW_TILES, LANES), F32),
            pltpu.VMEM((MOE_ROWS, d), F32),
            pltpu.VMEM((W_STAGE, d, f), F32), pltpu.VMEM((W_STAGE, d, f), F32),
            pltpu.VMEM((W_STAGE, f, d), F32),
            pltpu.VMEM((n_res, d, f), BF16), pltpu.VMEM((n_res, d, f), BF16),
            pltpu.VMEM((n_res, f, d), BF16),
            pltpu.SemaphoreType.DMA((RING,)),
            pltpu.SemaphoreType.DMA((RING,)),
            pltpu.SemaphoreType.DMA(()),
            pltpu.SemaphoreType.DMA((W_STAGE,)),
        ],
    )
    return pl.pallas_call(
        functools.partial(_moe_kernel, layer, n_blocks),
        grid_spec=grid_spec,
        out_shape=jax.ShapeDtypeStruct((n_rows * ROW_TILES, LANES), F32),
        compiler_params=pltpu.CompilerParams(
            dimension_semantics=("arbitrary",), vmem_limit_bytes=VMEM_LIMIT),
        name="moe_half",
    )(*plan, x1, gates_sorted, w_gate, w_up, w_down, ln_g.reshape(1, d), ln_b.reshape(1, d))


def _plan_blocks(route_f, route_i, counts_f, n_tok):
    cls, rank = route_i[0], route_i[1]
    counts = counts_f[:N_CLASSES, 0].astype(I32)
    padded = (counts + MOE_ROWS - 1) // MOE_ROWS * MOE_ROWS
    pad_end = jnp.cumsum(padded)
    pad_start = pad_end - padded
    start = jnp.cumsum(counts) - counts
    class_ids = jnp.arange(N_CLASSES, dtype=I32)
    dest = jnp.sum(jnp.where(cls[:, None] == class_ids[None], pad_start[None], 0), axis=1) + rank
    _, order, g_lo, g_hi = lax.sort(
        (dest, jnp.arange(n_tok, dtype=I32), route_f[0], route_f[1]), num_keys=1)
    order = jnp.pad(order, (0, MOE_ROWS)) * ROW_TILES
    gates_sorted = jnp.pad(jnp.stack([g_lo, g_hi], axis=1), ((0, MOE_ROWS), (0, LANES - 2)))

    n_blocks = n_tok // MOE_ROWS + N_CLASSES
    blk_ids = jnp.arange(n_blocks, dtype=I32)
    blk_start = blk_ids * MOE_ROWS
    nblk = pad_end[-1] // MOE_ROWS
    blk_cls = jnp.minimum(jnp.sum(pad_end[None] <= blk_start[:, None], axis=1), N_CLASSES - 1)
    blk_cls = jnp.where(blk_ids < nblk, blk_cls, blk_cls[jnp.maximum(nblk - 1, 0)]).astype(I32)
    in_cls = blk_start - pad_start[blk_cls]
    active = blk_ids < nblk
    blk_off = jnp.where(active, start[blk_cls] + in_cls, 0).astype(I32)
    nvalid = jnp.where(active, jnp.clip(counts[blk_cls] - in_cls, 0, MOE_ROWS), 0).astype(I32)
    trash = ((n_tok + blk_start - blk_off - nvalid) * ROW_TILES).astype(I32)
    pair_lo = jnp.array(PAIR_LO, I32)
    pair_hi = jnp.array(PAIR_HI, I32)
    blk_e1 = (blk_cls // N_PAIRS) * EXPERTS_PER_GROUP + pair_lo[blk_cls % N_PAIRS]
    blk_e2 = (blk_cls // N_PAIRS) * EXPERTS_PER_GROUP + pair_hi[blk_cls % N_PAIRS]
    lead = lambda a, v: jnp.concatenate([jnp.full((1,), v, I32), a])
    plan = (order, lead(blk_off, 0), lead(nvalid, 0),
            lead(trash, n_blocks * MOE_ROWS * ROW_TILES),
            blk_e1, blk_e2, nblk.astype(I32).reshape(1))
    return plan, gates_sorted


def kernel(x, mem, conv_w_in, conv_w, mlstm_w_in, mlstm_gate_b, mlstm_norm_g, w_kv_mem, w_out,
           ln1_g, ln1_b, ln2_g, ln2_b, router_w, router_b, w_gate, w_up, w_down):
    nb, seq, d = x.shape
    n_tok = nb * seq
    mem_len = mem.shape[1]
    assert d == D_MODEL and seq % TM_CONV == 0 and seq % TM_MLSTM == 0 and TM_MLSTM % CHUNK == 0
    assert n_tok // MOE_ROWS >= RING

    w_kv_all = jnp.transpose(w_kv_mem, (1, 0, 2)).reshape(d, DEPTH * 2 * XATT_W).astype(BF16)
    kv = _matmul(mem.reshape(nb * mem_len, d), w_kv_all, 512, 1024)
    kv = kv.reshape(nb, mem_len, DEPTH, 2, XATT_HEADS, XATT_DH)
    eye = jnp.eye(XATT_HEADS, dtype=bool)

    perm = lambda a: jnp.swapaxes(a.reshape(N_GROUPS, EXPERTS_PER_GROUP, -1), 0, 1)
    rwt = perm(router_w.T).reshape(N_EXPERTS, d).astype(BF16)
    rb = perm(router_b).reshape(N_EXPERTS, 1)

    xf = x.reshape(n_tok, d)
    for i in range(DEPTH):
        j = i // 2
        k = kv[:, :, i, 0] * (XATT_DH ** -0.5)
        v = kv[:, :, i, 1]
        kbd = jnp.where(eye[None, :, None, :, None],
                        jnp.transpose(k, (0, 2, 3, 1))[:, :, :, None, :], 0.0)
        kbd = kbd.reshape(nb, XATT_W, XATT_HEADS * mem_len).astype(BF16)
        vbd = jnp.where(eye[None, :, None, :, None],
                        jnp.transpose(v, (0, 2, 1, 3))[:, :, :, None, :], 0.0)
        vbd = vbd.reshape(nb, XATT_HEADS * mem_len, XATT_W).astype(BF16)
        wout = w_out[i].astype(BF16)
        if i % 2 == 0:
            win = conv_w_in[j].astype(BF16)
            outs = _mixer_half(_conv_half_kernel, TM_CONV, xf, nb, seq, [win, conv_w[j]],
                               kbd, vbd, wout, ln1_g[i], ln1_b[i], rwt, rb,
                               [pltpu.VMEM((8, MIX_W), F32)], "conv_half")
        else:
            w = mlstm_w_in[j]
            o0 = 4 * MIX_W
            w_gates = jnp.pad(w[:, o0:o0 + 2 * ML_HEADS], ((0, 0), (0, GATE_PAD - 2 * ML_HEADS)))
            win = jnp.concatenate([w[:, :o0], w[:, o0 + 2 * ML_HEADS:], w_gates], axis=1).astype(BF16)
            gate_b = jnp.pad(mlstm_gate_b[j], (0, GATE_PAD - 2 * ML_HEADS)).reshape(1, GATE_PAD)
            norm_g = mlstm_norm_g[j].reshape(1, MIX_W)
            outs = _mixer_half(_mlstm_half_kernel, TM_MLSTM, xf, nb, seq, [win, gate_b, norm_g],
                               kbd, vbd, wout, ln1_g[i], ln1_b[i], rwt, rb,
                               [pltpu.VMEM((ML_HEADS, ML_DH, 2 * ML_DH), F32),
                                pltpu.VMEM((ML_HEADS, 1, LANES), F32),
                                pltpu.VMEM((TM_MLSTM, MIX_W), F32)], "mlstm_half")
        x1, route_f, route_i, counts_f = outs
        plan, gates_sorted = _plan_blocks(route_f, route_i, counts_f, n_tok)
        xf = _moe_half(i, x1, plan, gates_sorted, w_gate, w_up, w_down, ln2_g[i], ln2_b[i])
    return xf[:n_tok * ROW_TILES].reshape(nb, seq, d)
```

```python
import functools

import jax
import jax.numpy as jnp
from jax import lax
from jax.experimental import pallas as pl
from jax.experimental.pallas import tpu as pltpu

F32 = jnp.float32
BF16 = jnp.bfloat16
I32 = jnp.int32

D_MODEL = 1024
DEPTH = 4
MIX_W = 768
XATT_HEADS = 4
XATT_DH = 64
XATT_W = XATT_HEADS * XATT_DH
CONV_K = 3
ML_HEADS = 6
ML_DH = MIX_W // ML_HEADS
CHUNK = 128
N_EXPERTS = 32
N_GROUPS = 8
EXPERTS_PER_GROUP = N_EXPERTS // N_GROUPS
D_FF = 512
ALPHA = (2.0 * DEPTH) ** 0.25
LN_EPS = 1e-5

LANES = 128
SUBLANES = 8
ROW_TILES = D_MODEL // LANES
GATE_PAD = LANES
MOE_ROWS = 128
RING = 3
W_SETS = 2
W_STAGE = 2
PAIR_LO = (0, 0, 1, 1, 2, 0)
PAIR_HI = (1, 2, 2, 3, 3, 3)
N_PAIRS = len(PAIR_LO)
N_CLASSES = N_GROUPS * N_PAIRS
CLASS_PAD = 64
VMEM_LIMIT = 52 * 1024 * 1024

TM_CONV = 512
TM_MLSTM = 256


def _layer_norm(z, g, b):
    mu = jnp.mean(z, axis=-1, keepdims=True)
    zc = z - mu
    var = jnp.mean(zc * zc, axis=-1, keepdims=True)
    return zc * lax.rsqrt(var + LN_EPS) * g + b


def _load_rows(ref, n):
    return jnp.concatenate([ref[pl.ds(j, n, stride=ROW_TILES), :] for j in range(ROW_TILES)],
                           axis=-1)


def _load_x(x_ref):
    if x_ref.shape[1] == D_MODEL:
        return x_ref[...]
    return _load_rows(x_ref, x_ref.shape[0] // ROW_TILES)


def _store_rows(ref, val):
    for j in range(ROW_TILES):
        ref[pl.ds(j, val.shape[0], stride=ROW_TILES), :] = val[:, j * LANES:(j + 1) * LANES]


def _dot(a, b):
    return jnp.dot(a, b, preferred_element_type=F32)


def _dot_nt(a, b):
    return lax.dot_general(a, b, (((1,), (1,)), ((), ())), preferred_element_type=F32)


def _matmul_kernel(a_ref, b_ref, o_ref):
    o_ref[...] = _dot(a_ref[...].astype(BF16), b_ref[...])


def _matmul(a, b_bf16, tm, tn):
    m, k = a.shape
    n = b_bf16.shape[1]
    return pl.pallas_call(
        _matmul_kernel,
        grid=(m // tm, n // tn),
        in_specs=[pl.BlockSpec((tm, k), lambda i, j: (i, 0)),
                  pl.BlockSpec((k, tn), lambda i, j: (0, j))],
        out_specs=pl.BlockSpec((tm, tn), lambda i, j: (i, j)),
        out_shape=jax.ShapeDtypeStruct((m, n), F32),
        compiler_params=pltpu.CompilerParams(
            dimension_semantics=("arbitrary", "arbitrary"), vmem_limit_bytes=VMEM_LIMIT),
        name="kv_proj",
    )(a, b_bf16)


def _route_tile(x1b, rwt_ref, rb_ref, tri_ref, base_ref, rf_ref, ri_ref, cnt_ref):
    tm = x1b.shape[0]
    scores = jax.nn.sigmoid(_dot_nt(rwt_ref[...], x1b))
    sel = scores + rb_ref[...]
    s = [sel[SUBLANES * j:SUBLANES * (j + 1)] for j in range(EXPERTS_PER_GROUP)]
    p = [scores[SUBLANES * j:SUBLANES * (j + 1)] for j in range(EXPERTS_PER_GROUP)]
    hi1, lo1 = jnp.maximum(s[0], s[1]), jnp.minimum(s[0], s[1])
    hi2, lo2 = jnp.maximum(s[2], s[3]), jnp.minimum(s[2], s[3])
    gs = jnp.maximum(hi1, hi2) + jnp.maximum(jnp.minimum(hi1, hi2), jnp.maximum(lo1, lo2))
    sub = lax.broadcasted_iota(I32, (N_GROUPS, tm), 0)
    gi = jnp.min(jnp.where(gs == jnp.max(gs, axis=0, keepdims=True), sub, N_GROUPS),
                 axis=0, keepdims=True)
    gm = sub == gi
    v = [jnp.sum(jnp.where(gm, sj, 0.0), axis=0, keepdims=True) for sj in s]
    pv = [jnp.sum(jnp.where(gm, pj, 0.0), axis=0, keepdims=True) for pj in p]
    best, i1 = v[0], jnp.zeros((1, tm), I32)
    for j in range(1, EXPERTS_PER_GROUP):
        u = v[j] > best
        best, i1 = jnp.where(u, v[j], best), jnp.where(u, j, i1)
    best2, i2 = jnp.full((1, tm), -jnp.inf, F32), jnp.zeros((1, tm), I32)
    for j in range(EXPERTS_PER_GROUP):
        cand = jnp.where(i1 == j, -jnp.inf, v[j])
        u = cand > best2
        best2, i2 = jnp.where(u, cand, best2), jnp.where(u, j, i2)

    def pick(idx):
        return jnp.where(idx == 0, pv[0], jnp.where(idx == 1, pv[1],
                                                    jnp.where(idx == 2, pv[2], pv[3])))

    p1, p2 = pick(i1), pick(i2)
    tot = p1 + p2
    first_lo = i1 < i2
    g_lo = jnp.where(first_lo, p1, p2) / tot
    g_hi = jnp.where(first_lo, p2, p1) / tot
    lo, hi = jnp.minimum(i1, i2), jnp.maximum(i1, i2)
    pair = jnp.where(hi == 1, 0, jnp.where(hi == 2, jnp.where(lo == 0, 1, 2),
                                           jnp.where(lo == 1, 3, jnp.where(lo == 2, 4, 5))))
    cls = gi * N_PAIRS + pair

    oh = lax.broadcasted_iota(I32, (CLASS_PAD, tm), 0) == cls
    pref = _dot(oh.astype(F32).astype(BF16), tri_ref[...])
    base = base_ref[...]
    rank = jnp.sum(jnp.where(oh, pref + base[:, 0:1] - 1.0, 0.0), axis=0, keepdims=True)
    base = base + pref[:, tm - 1:tm]
    base_ref[...] = base
    cnt_ref[...] = base

    sub8 = lax.broadcasted_iota(I32, (SUBLANES, tm), 0)
    rf_ref[...] = jnp.where(sub8 == 0, g_lo, jnp.where(sub8 == 1, g_hi, 0.0))
    ri_ref[...] = jnp.where(sub8 == 0, cls, jnp.where(sub8 == 1, rank.astype(I32), 0))


def _mixer_tail(x, h_loc, q_mem, tail_refs):
    (kbd_ref, vbd_ref, wout_ref, g_ref, b_ref, rwt_ref, rb_ref, tri_ref,
     x1_ref, rf_ref, ri_ref, cnt_ref, base_ref) = tail_refs

    @pl.when((pl.program_id(0) == 0) & (pl.program_id(1) == 0))
    def _():
        base_ref[...] = jnp.zeros_like(base_ref)

    s = _dot(q_mem.astype(BF16), kbd_ref[0])
    mem_len = s.shape[1] // XATT_HEADS
    sh = [s[:, h * mem_len:(h + 1) * mem_len] for h in range(XATT_HEADS)]
    mx = [jnp.max(t, axis=-1, keepdims=True) for t in sh]
    e = [jnp.exp(t - m) for t, m in zip(sh, mx)]
    tot = [jnp.sum(t, axis=-1, keepdims=True) for t in e]
    ps = [(t / n).astype(BF16) for t, n in zip(e, tot)]
    h_mem = _dot(jnp.concatenate(ps, axis=-1), vbd_ref[0])
    hcat = jnp.concatenate([h_loc.astype(BF16), h_mem.astype(BF16)], axis=-1)
    y = _dot(hcat, wout_ref[...])
    x1 = _layer_norm(ALPHA * x + y, g_ref[...], b_ref[...])
    _store_rows(x1_ref, x1)
    _route_tile(x1.astype(BF16), rwt_ref, rb_ref, tri_ref, base_ref, rf_ref, ri_ref, cnt_ref)


def _conv_half_kernel(x_ref, win_ref, cw_ref, *refs):
    tail_refs, carry_ref = refs[:-1], refs[-1]

    @pl.when(pl.program_id(1) == 0)
    def _():
        carry_ref[...] = jnp.zeros_like(carry_ref)

    x = _load_x(x_ref)
    tm = x.shape[0]
    proj = _dot(x.astype(BF16), win_ref[...])
    bg = proj[:, 0:MIX_W]
    cg = proj[:, MIX_W:2 * MIX_W]
    xin = proj[:, 2 * MIX_W:3 * MIX_W]
    q_mem = proj[:, 3 * MIX_W:]
    u = cg * xin
    prev = carry_ref[...]
    row = lax.broadcasted_iota(I32, u.shape, 0)
    u1 = jnp.where(row == 0, prev[7:8], pltpu.roll(u, 1, 0))
    u2 = jnp.where(row == 0, prev[6:7],
                   jnp.where(row == 1, prev[7:8], pltpu.roll(u, 2, 0)))
    carry_ref[...] = u[tm - 8:tm]
    cw = cw_ref[...]
    c = cw[0:1] * u2 + cw[1:2] * u1 + cw[2:3] * u
    _mixer_tail(x, bg * c, q_mem, tail_refs)


def _log_sigmoid(z):
    return -(jnp.maximum(-z, 0.0) + jnp.log1p(jnp.exp(-jnp.abs(z))))


def _mlstm_half_kernel(x_ref, win_ref, gb_ref, ng_ref, *refs):
    tail_refs, (state_ref, m_ref, h_ref) = refs[:-3], refs[-3:]

    @pl.when(pl.program_id(1) == 0)
    def _():
        state_ref[...] = jnp.zeros_like(state_ref)
        m_ref[...] = jnp.zeros_like(m_ref)

    x = _load_x(x_ref)
    tm = x.shape[0]
    proj = _dot(x.astype(BF16), win_ref[...])
    g0 = 4 * MIX_W + XATT_W
    gates = proj[:, g0:g0 + GATE_PAD] + gb_ref[...]
    row = lax.broadcasted_iota(I32, (CHUNK, GATE_PAD), 0)
    head_lane = lax.broadcasted_iota(I32, (tm, GATE_PAD), 1) < ML_HEADS
    log_i = jnp.where(head_lane, gates, 0.0)
    log_f = jnp.where(head_lane, pltpu.roll(_log_sigmoid(gates), GATE_PAD - ML_HEADS, 1), 0.0)
    causal = (lax.broadcasted_iota(I32, (CHUNK, CHUNK), 0)
              >= lax.broadcasted_iota(I32, (CHUNK, CHUNK), 1))
    ones_col = (lax.broadcasted_iota(I32, (CHUNK, ML_DH), 1) == 0).astype(BF16)

    def scan_rows(a, op, fill):
        sh = 1
        while sh < CHUNK:
            a = op(a, jnp.where(row >= sh, pltpu.roll(a, sh, 0), fill))
            sh *= 2
        return a

    chunks, heads = range(tm // CHUNK), range(ML_HEADS)
    pairs = [(c, h) for c in chunks for h in heads]

    m_prev = m_ref[...]
    gate = []
    for c in chunks:
        li = log_i[c * CHUNK:(c + 1) * CHUNK]
        b = scan_rows(log_f[c * CHUNK:(c + 1) * CHUNK], jnp.add, 0.0)
        wd = li - b
        m_t = b + jnp.maximum(scan_rows(wd, jnp.maximum, -jnp.inf), m_prev)
        g = b[CHUNK - 1:CHUNK, :]
        a = g + wd
        m_new = jnp.maximum(g + m_prev, jnp.max(a, axis=0, keepdims=True))
        gate.append(dict(u=b - m_t, w_inter=jnp.exp(b + m_prev - m_t), clamp=jnp.exp(-m_t),
                         decay=jnp.exp(g + m_prev - m_new), wd_t=wd.T,
                         wn_t=jnp.exp(a - m_new).T))
        m_prev = m_new
    m_ref[...] = m_prev

    col = lambda c, h, o=0: proj[c * CHUNK:(c + 1) * CHUNK, o + h * ML_DH:o + (h + 1) * ML_DH]
    qb = {p: col(*p).astype(BF16) for p in pairs}
    kf = {p: col(*p, MIX_W) * (ML_DH ** -0.5) for p in pairs}
    kb = {p: kf[p].astype(BF16) for p in pairs}
    vb = {p: col(*p, 2 * MIX_W).astype(BF16) for p in pairs}
    qk = {p: _dot_nt(qb[p], kb[p]) for p in pairs}
    s = {(c, h): qk[c, h] * jnp.exp(jnp.where(
        causal, gate[c]["u"][:, h:h + 1] + gate[c]["wd_t"][h:h + 1, :], -jnp.inf))
        for c, h in pairs}
    sv = {p: _dot(s[p].astype(BF16), vb[p]) for p in pairs}
    s_sum = {p: jnp.sum(s[p], axis=-1, keepdims=True) for p in pairs}
    ktw = {(c, h): (kf[c, h].T * gate[c]["wn_t"][h:h + 1, :]).astype(BF16) for c, h in pairs}
    kv = {p: _dot(ktw[p], jnp.concatenate([vb[p], ones_col], axis=-1)) for p in pairs}
    state = [state_ref[h] for h in heads]
    qs = {}
    for c in chunks:
        for h in heads:
            qs[c, h] = _dot(qb[c, h], state[h].astype(BF16))
        state = [gate[c]["decay"][:, h:h + 1] * state[h] + kv[c, h] for h in heads]
    for h in heads:
        state_ref[h] = state[h]
    wi = {(c, h): gate[c]["w_inter"][:, h:h + 1] for c, h in pairs}
    den = {p: s_sum[p] + wi[p] * qs[p][:, ML_DH:ML_DH + 1] for p in pairs}
    hh = {(c, h): (sv[c, h] + wi[c, h] * qs[c, h][:, 0:ML_DH])
          / jnp.maximum(jnp.abs(den[c, h]), gate[c]["clamp"][:, h:h + 1]) for c, h in pairs}
    hc = {p: hh[p] - jnp.mean(hh[p], axis=-1, keepdims=True) for p in pairs}
    var = {p: jnp.mean(hc[p] * hc[p], axis=-1, keepdims=True) for p in pairs}
    for c, h in pairs:
        h_ref[c * CHUNK:(c + 1) * CHUNK, h * ML_DH:(h + 1) * ML_DH] = (
            hc[c, h] * lax.rsqrt(var[c, h] + LN_EPS))

    o_gate = proj[:, 3 * MIX_W:4 * MIX_W]
    h_loc = jax.nn.sigmoid(o_gate) * (h_ref[...] * ng_ref[...])
    q_mem = proj[:, 4 * MIX_W:4 * MIX_W + XATT_W]
    _mixer_tail(x, h_loc, q_mem, tail_refs)


def _mixer_half(kernel_fn, tm, x, nb, seq, mixer_args, kbd, vbd, wout, ln_g, ln_b, rwt, rb,
                scratch, name):
    d = D_MODEL
    n_tok = nb * seq
    spb = seq // tm
    const2 = lambda b, s: (0, 0)
    tri = jnp.triu(jnp.ones((tm, tm), BF16))
    tail_args = [kbd, vbd, wout, ln_g.reshape(1, d), ln_b.reshape(1, d), rwt, rb, tri]
    row_map = lambda b, s: (b * spb + s, 0)
    x_spec = (pl.BlockSpec((tm, d), row_map) if x.shape[1] == d
              else pl.BlockSpec((tm * ROW_TILES, LANES), row_map))
    in_specs = ([x_spec]
                + [pl.BlockSpec(a.shape, const2) for a in mixer_args]
                + [pl.BlockSpec((1,) + kbd.shape[1:], lambda b, s: (b, 0, 0)),
                   pl.BlockSpec((1,) + vbd.shape[1:], lambda b, s: (b, 0, 0))]
                + [pl.BlockSpec(a.shape, const2) for a in tail_args[2:]])
    tok_spec = lambda rows: pl.BlockSpec((rows, tm), lambda b, s: (0, b * spb + s))
    return pl.pallas_call(
        kernel_fn,
        grid=(nb, spb),
        in_specs=in_specs,
        out_specs=[pl.BlockSpec((tm * ROW_TILES, LANES), row_map),
                   tok_spec(SUBLANES), tok_spec(SUBLANES),
                   pl.BlockSpec((CLASS_PAD, LANES), const2)],
        out_shape=[jax.ShapeDtypeStruct((n_tok * ROW_TILES, LANES), F32),
                   jax.ShapeDtypeStruct((SUBLANES, n_tok), F32),
                   jax.ShapeDtypeStruct((SUBLANES, n_tok), I32),
                   jax.ShapeDtypeStruct((CLASS_PAD, LANES), F32)],
        scratch_shapes=[pltpu.VMEM((CLASS_PAD, LANES), F32)] + scratch,
        compiler_params=pltpu.CompilerParams(
            dimension_semantics=("arbitrary", "arbitrary"), vmem_limit_bytes=VMEM_LIMIT),
        name=name,
    )(x, *mixer_args, *tail_args)


def _moe_kernel(layer, n_blocks,
                order_ref, off_ref, nvalid_ref, trash_ref, e1_ref, e2_ref, nblk_ref,
                x_hbm, g_hbm, wg_hbm, wu_hbm, wd_hbm, lng_ref, lnb_ref, out_hbm,
                xbuf, gbuf, obuf, zbuf, acc, sg, su, sd, wg_bf, wu_bf, wd_bf,
                gsem, ssem, zsem, wsem):
    nblk = nblk_ref[0]
    last = nblk - 1
    ring = lambda j: (j + RING) % RING
    n_res = W_SETS * EXPERTS_PER_GROUP

    def gate_copy(blk, slt):
        return pltpu.make_async_copy(g_hbm.at[pl.ds(off_ref[blk + 1], MOE_ROWS)], gbuf.at[slt],
                                     gsem.at[slt])

    def tile_row(r):
        return r * ROW_TILES if isinstance(r, int) else pl.multiple_of(r * ROW_TILES, ROW_TILES)

    def gather_row(base, r, slt):
        t8 = pl.multiple_of(order_ref[base + r], ROW_TILES)
        pltpu.make_async_copy(x_hbm.at[pl.ds(t8, ROW_TILES)],
                              xbuf.at[slt, pl.ds(tile_row(r), ROW_TILES)], gsem.at[slt]).start()

    def issue_gather(blk, slt, unrolled=True):
        base = off_ref[blk + 1]
        if unrolled:
            for r in range(MOE_ROWS):
                gather_row(base, r, slt)
        else:
            lax.fori_loop(0, MOE_ROWS, lambda r, c: (gather_row(base, r, slt), c)[1], 0)
        gate_copy(blk, slt).start()

    def wait_gather(slt):
        pltpu.make_async_copy(x_hbm.at[pl.ds(0, MOE_ROWS * ROW_TILES)], xbuf.at[slt],
                              gsem.at[slt]).wait()
        gate_copy(0, slt).wait()

    def scatter_row(base, nvalid, trash8, r, slt):
        t8 = jnp.where(r < nvalid, order_ref[base + r], trash8 + r * ROW_TILES)
        pltpu.make_async_copy(
            obuf.at[slt, pl.ds(tile_row(r), ROW_TILES)],
            out_hbm.at[pl.ds(pl.multiple_of(t8, ROW_TILES), ROW_TILES)], ssem.at[slt]).start()

    def issue_scatter(blk, slt, unrolled=True):
        args = (off_ref[blk + 1], nvalid_ref[blk + 1], trash_ref[blk + 1])
        if unrolled:
            for r in range(MOE_ROWS):
                scatter_row(*args, r, slt)
        else:
            lax.fori_loop(0, MOE_ROWS, lambda r, c: (scatter_row(*args, r, slt), c)[1], 0)

    def wait_scatter(slt):
        pltpu.make_async_copy(obuf.at[slt], out_hbm.at[pl.ds(0, MOE_ROWS * ROW_TILES)],
                              ssem.at[slt]).wait()

    def weight_copies(k, stage):
        return [pltpu.make_async_copy(src.at[layer, k], dst.at[stage], wsem.at[stage])
                for src, dst in ((wg_hbm, sg), (wu_hbm, su), (wd_hbm, sd))]

    def load_start(k):
        for cp in weight_copies(k, k % W_STAGE):
            cp.start()

    def load_wait(k):
        for cp in weight_copies(k, k % W_STAGE):
            cp.wait()

    def load_finish(k):
        load_wait(k)
        stage, res = k % W_STAGE, k % n_res
        wg_bf[res] = sg[stage].astype(BF16)
        wu_bf[res] = su[stage].astype(BF16)
        wd_bf[res] = sd[stage].astype(BF16)

    def start_if(pred, kd):
        @pl.when(pred)
        def _():
            load_start(kd)
        return kd + pred.astype(I32)

    def service_loader(group, kd, kc):
        cap = jnp.minimum((group + W_SETS) * EXPERTS_PER_GROUP, N_EXPERTS)
        need = (group + 1) * EXPERTS_PER_GROUP

        def catch_up(c):
            kd, kc = c
            kd = start_if(kd <= kc, kd)
            kd = start_if((kd < kc + W_STAGE) & (kd < cap), kd)
            load_finish(kc)
            return kd, kc + 1
        kd, kc = lax.while_loop(lambda c: c[1] < need, catch_up, (kd, kc))

        ready = (kc < kd) & (kc < cap)

        @pl.when(ready)
        def _():
            load_finish(kc)
        kc = kc + ready.astype(I32)
        kd = start_if((kd < kc + W_STAGE) & (kd < cap), kd)
        return kd, kc

    def ffn(xb, res):
        hg = _dot(xb, wg_bf[res])
        hu = _dot(xb, wu_bf[res])
        return _dot((hg * jax.nn.sigmoid(hg) * hu).astype(BF16), wd_bf[res])

    issue_gather(0, 0, unrolled=False)
    issue_gather(jnp.minimum(1, last), 1, unrolled=False)
    obuf[RING - 1] = jnp.zeros(obuf.shape[1:], F32)

    def block(b, loader):
        slot = b % RING
        e1, e2 = e1_ref[b], e2_ref[b]
        loader = service_loader(e1 // EXPERTS_PER_GROUP, *loader)

        wait_gather(slot)
        xb = _load_rows(xbuf.at[slot], MOE_ROWS).astype(BF16)
        issue_gather(jnp.minimum(b + 2, last), ring(b + 2))
        acc[...] = gbuf[slot][:, 0:1] * ffn(xb, e1 % n_res)

        @pl.when(b >= RING - 1)
        def _():
            wait_scatter(slot)

        x = _load_rows(xbuf.at[slot], MOE_ROWS)
        m = acc[...] + gbuf[slot][:, 1:2] * ffn(x.astype(BF16), e2 % n_res)
        o = _layer_norm(ALPHA * x + m, lng_ref[...], lnb_ref[...])
        issue_scatter(b - 1, ring(b - 1))
        _store_rows(obuf.at[slot], o)
        return loader

    first = (e1_ref[0] // EXPERTS_PER_GROUP) * EXPERTS_PER_GROUP
    kd, kc = lax.fori_loop(0, nblk, block, (first, first))

    lax.fori_loop(kc, kd, lambda k, c: (load_wait(k), c)[1], 0)
    issue_scatter(last, ring(last), unrolled=False)
    for slt in range(RING):
        wait_scatter(slt)
    wait_gather(ring(nblk))
    wait_gather(ring(nblk + 1))

    zbuf[...] = jnp.zeros_like(zbuf)

    def fill_copy(j):
        return pltpu.make_async_copy(
            zbuf, out_hbm.at[pl.ds(j * MOE_ROWS * ROW_TILES, MOE_ROWS * ROW_TILES)], zsem)
    lax.fori_loop(nblk, n_blocks, lambda j, c: (fill_copy(j).start(), c)[1], 0)
    lax.fori_loop(nblk, n_blocks, lambda j, c: (fill_copy(j).wait(), c)[1], 0)


def _moe_half(layer, x1, plan, gates_sorted, w_gate, w_up, w_down, ln_g, ln_b):
    n_blocks = plan[4].shape[0]
    n_rows = (n_blocks + 1) * MOE_ROWS
    d, f = D_MODEL, D_FF
    n_res = W_SETS * EXPERTS_PER_GROUP
    const_spec = pl.BlockSpec((1, d), lambda i, *_: (0, 0))
    any_spec = pl.BlockSpec(memory_space=pl.ANY)
    grid_spec = pltpu.PrefetchScalarGridSpec(
        num_scalar_prefetch=7,
        grid=(1,),
        in_specs=[any_spec, any_spec, any_spec, any_spec, any_spec, const_spec, const_spec],
        out_specs=any_spec,
        scratch_shapes=[
            pltpu.VMEM((RING, MOE_ROWS * ROW_TILES, LANES), F32),
            pltpu.VMEM((RING, MOE_ROWS, LANES), F32),
            pltpu.VMEM((RING, MOE_ROWS * ROW_TILES, LANES), F32),
            pltpu.VMEM((MOE_ROWS * ROW_TILES, LANES), F32),
            pltpu.VMEM((MOE_ROWS, d), F32),
            pltpu.VMEM((W_STAGE, d, f), F32), pltpu.VMEM((W_STAGE, d, f), F32),
            pltpu.VMEM((W_STAGE, f, d), F32),
            pltpu.VMEM((n_res, d, f), BF16), pltpu.VMEM((n_res, d, f), BF16),
            pltpu.VMEM((n_res, f, d), BF16),
            pltpu.SemaphoreType.DMA((RING,)),
            pltpu.SemaphoreType.DMA((RING,)),
            pltpu.SemaphoreType.DMA(()),
            pltpu.SemaphoreType.DMA((W_STAGE,)),
        ],
    )
    return pl.pallas_call(
        functools.partial(_moe_kernel, layer, n_blocks),
        grid_spec=grid_spec,
        out_shape=jax.ShapeDtypeStruct((n_rows * ROW_TILES, LANES), F32),
        compiler_params=pltpu.CompilerParams(
            dimension_semantics=("arbitrary",), vmem_limit_bytes=VMEM_LIMIT),
        name="moe_half",
    )(*plan, x1, gates_sorted, w_gate, w_up, w_down, ln_g.reshape(1, d), ln_b.reshape(1, d))


def _plan_blocks(route_f, route_i, counts_f, n_tok):
    cls, rank = route_i[0], route_i[1]
    counts = counts_f[:N_CLASSES, 0].astype(I32)
    padded = (counts + MOE_ROWS - 1) // MOE_ROWS * MOE_ROWS
    pad_end = jnp.cumsum(padded)
    pad_start = pad_end - padded
    start = jnp.cumsum(counts) - counts
    class_ids = jnp.arange(N_CLASSES, dtype=I32)
    dest = jnp.sum(jnp.where(cls[:, None] == class_ids[None], pad_start[None], 0), axis=1) + rank
    _, order, g_lo, g_hi = lax.sort(
        (dest, jnp.arange(n_tok, dtype=I32), route_f[0], route_f[1]), num_keys=1)
    order = jnp.pad(order, (0, MOE_ROWS)) * ROW_TILES
    gates_sorted = jnp.pad(jnp.stack([g_lo, g_hi], axis=1), ((0, MOE_ROWS), (0, LANES - 2)))

    n_blocks = n_tok // MOE_ROWS + N_CLASSES
    blk_ids = jnp.arange(n_blocks, dtype=I32)
    blk_start = blk_ids * MOE_ROWS
    nblk = pad_end[-1] // MOE_ROWS
    blk_cls = jnp.minimum(jnp.sum(pad_end[None] <= blk_start[:, None], axis=1), N_CLASSES - 1)
    blk_cls = jnp.where(blk_ids < nblk, blk_cls, blk_cls[jnp.maximum(nblk - 1, 0)]).astype(I32)
    in_cls = blk_start - pad_start[blk_cls]
    active = blk_ids < nblk
    blk_off = jnp.where(active, start[blk_cls] + in_cls, 0).astype(I32)
    nvalid = jnp.where(active, jnp.clip(counts[blk_cls] - in_cls, 0, MOE_ROWS), 0).astype(I32)
    trash = ((n_tok + blk_start - blk_off - nvalid) * ROW_TILES).astype(I32)
    pair_lo = jnp.array(PAIR_LO, I32)
    pair_hi = jnp.array(PAIR_HI, I32)
    blk_e1 = (blk_cls // N_PAIRS) * EXPERTS_PER_GROUP + pair_lo[blk_cls % N_PAIRS]
    blk_e2 = (blk_cls // N_PAIRS) * EXPERTS_PER_GROUP + pair_hi[blk_cls % N_PAIRS]
    lead = lambda a, v: jnp.concatenate([jnp.full((1,), v, I32), a])
    plan = (order, lead(blk_off, 0), lead(nvalid, 0),
            lead(trash, n_blocks * MOE_ROWS * ROW_TILES),
            blk_e1, blk_e2, nblk.astype(I32).reshape(1))
    return plan, gates_sorted


def kernel(x, mem, conv_w_in, conv_w, mlstm_w_in, mlstm_gate_b, mlstm_norm_g, w_kv_mem, w_out,
           ln1_g, ln1_b, ln2_g, ln2_b, router_w, router_b, w_gate, w_up, w_down):
    nb, seq, d = x.shape
    n_tok = nb * seq
    mem_len = mem.shape[1]
    assert d == D_MODEL and seq % TM_CONV == 0 and seq % TM_MLSTM == 0 and TM_MLSTM % CHUNK == 0
    assert n_tok // MOE_ROWS >= RING

    w_kv_all = jnp.transpose(w_kv_mem, (1, 0, 2)).reshape(d, DEPTH * 2 * XATT_W).astype(BF16)
    kv = _matmul(mem.reshape(nb * mem_len, d), w_kv_all, 512, 1024)
    kv = kv.reshape(nb, mem_len, DEPTH, 2, XATT_HEADS, XATT_DH)
    eye = jnp.eye(XATT_HEADS, dtype=bool)

    perm = lambda a: jnp.swapaxes(a.reshape(N_GROUPS, EXPERTS_PER_GROUP, -1), 0, 1)
    rwt = perm(router_w.T).reshape(N_EXPERTS, d).astype(BF16)
    rb = perm(router_b).reshape(N_EXPERTS, 1)

    xf = x.reshape(n_tok, d)
    for i in range(DEPTH):
        j = i // 2
        k = kv[:, :, i, 0] * (XATT_DH ** -0.5)
        v = kv[:, :, i, 1]
        kbd = jnp.where(eye[None, :, None, :, None],
                        jnp.transpose(k, (0, 2, 3, 1))[:, :, :, None, :], 0.0)
        kbd = kbd.reshape(nb, XATT_W, XATT_HEADS * mem_len).astype(BF16)
        vbd = jnp.where(eye[None, :, None, :, None],
                        jnp.transpose(v, (0, 2, 1, 3))[:, :, :, None, :], 0.0)
        vbd = vbd.reshape(nb, XATT_HEADS * mem_len, XATT_W).astype(BF16)
        wout = w_out[i].astype(BF16)
        if i % 2 == 0:
            win = conv_w_in[j].astype(BF16)
            outs = _mixer_half(_conv_half_kernel, TM_CONV, xf, nb, seq, [win, conv_w[j]],
                               kbd, vbd, wout, ln1_g[i], ln1_b[i], rwt, rb,
                               [pltpu.VMEM((8, MIX_W), F32)], "conv_half")
        else:
            w = mlstm_w_in[j]
            o0 = 4 * MIX_W
            w_gates = jnp.pad(w[:, o0:o0 + 2 * ML_HEADS], ((0, 0), (0, GATE_PAD - 2 * ML_HEADS)))
            win = jnp.concatenate([w[:, :o0], w[:, o0 + 2 * ML_HEADS:], w_gates], axis=1).astype(BF16)
            gate_b = jnp.pad(mlstm_gate_b[j], (0, GATE_PAD - 2 * ML_HEADS)).reshape(1, GATE_PAD)
            norm_g = mlstm_norm_g[j].reshape(1, MIX_W)
            outs = _mixer_half(_mlstm_half_kernel, TM_MLSTM, xf, nb, seq, [win, gate_b, norm_g],
                               kbd, vbd, wout, ln1_g[i], ln1_b[i], rwt, rb,
                               [pltpu.VMEM((ML_HEADS, ML_DH, 2 * ML_DH), F32),
                                pltpu.VMEM((1, LANES), F32),
                                pltpu.VMEM((TM_MLSTM, MIX_W), F32)], "mlstm_half")
        x1, route_f, route_i, counts_f = outs
        plan, gates_sorted = _plan_blocks(route_f, route_i, counts_f, n_tok)
        xf = _moe_half(i, x1, plan, gates_sorted, w_gate, w_up, w_down, ln2_g[i], ln2_b[i])
    return xf[:n_tok * ROW_TILES].reshape(nb, seq, d)
```

```python
import functools

import jax
import jax.numpy as jnp
from jax import lax
from jax.experimental import pallas as pl
from jax.experimental.pallas import tpu as pltpu

F32 = jnp.float32
BF16 = jnp.bfloat16
I32 = jnp.int32

D_MODEL = 1024
DEPTH = 4
MIX_W = 768
XATT_HEADS = 4
XATT_DH = 64
XATT_W = XATT_HEADS * XATT_DH
CONV_K = 3
ML_HEADS = 6
ML_DH = MIX_W // ML_HEADS
CHUNK = 128
N_EXPERTS = 32
N_GROUPS = 8
EXPERTS_PER_GROUP = N_EXPERTS // N_GROUPS
D_FF = 512
ALPHA = (2.0 * DEPTH) ** 0.25
LN_EPS = 1e-5

LANES = 128
SUBLANES = 8
ROW_TILES = D_MODEL // LANES
GATE_PAD = LANES
MOE_ROWS = 128
RING = 3
W_SETS = 2
W_STAGE = 2
PAIR_LO = (0, 0, 1, 1, 2, 0)
PAIR_HI = (1, 2, 2, 3, 3, 3)
N_PAIRS = len(PAIR_LO)
N_CLASSES = N_GROUPS * N_PAIRS
CLASS_PAD = 64
VMEM_LIMIT = 52 * 1024 * 1024

TM_CONV = 512
TM_MLSTM = 256


def _layer_norm(z, g, b):
    mu = jnp.mean(z, axis=-1, keepdims=True)
    zc = z - mu
    var = jnp.mean(zc * zc, axis=-1, keepdims=True)
    return zc * lax.rsqrt(var + LN_EPS) * g + b


def _load_rows(ref, n):
    return jnp.concatenate([ref[pl.ds(j, n, stride=ROW_TILES), :] for j in range(ROW_TILES)],
                           axis=-1)


def _load_x(x_ref):
    if x_ref.shape[1] == D_MODEL:
        return x_ref[...]
    return _load_rows(x_ref, x_ref.shape[0] // ROW_TILES)


def _store_rows(ref, val):
    for j in range(ROW_TILES):
        ref[pl.ds(j, val.shape[0], stride=ROW_TILES), :] = val[:, j * LANES:(j + 1) * LANES]


def _dot(a, b):
    return jnp.dot(a, b, preferred_element_type=F32)


def _dot_nt(a, b):
    return lax.dot_general(a, b, (((1,), (1,)), ((), ())), preferred_element_type=F32)


def _matmul_kernel(a_ref, b_ref, o_ref):
    o_ref[...] = _dot(a_ref[...].astype(BF16), b_ref[...])


def _matmul(a, b_bf16, tm, tn):
    m, k = a.shape
    n = b_bf16.shape[1]
    return pl.pallas_call(
        _matmul_kernel,
        grid=(m // tm, n // tn),
        in_specs=[pl.BlockSpec((tm, k), lambda i, j: (i, 0)),
                  pl.BlockSpec((k, tn), lambda i, j: (0, j))],
        out_specs=pl.BlockSpec((tm, tn), lambda i, j: (i, j)),
        out_shape=jax.ShapeDtypeStruct((m, n), F32),
        compiler_params=pltpu.CompilerParams(
            dimension_semantics=("arbitrary", "arbitrary"), vmem_limit_bytes=VMEM_LIMIT),
        name="kv_proj",
    )(a, b_bf16)


def _route_tile(x1b, rwt_ref, rb_ref, tri_ref, base_ref, rf_ref, ri_ref, cnt_ref):
    tm = x1b.shape[0]
    scores = jax.nn.sigmoid(_dot_nt(rwt_ref[...], x1b))
    sel = scores + rb_ref[...]
    s = [sel[SUBLANES * j:SUBLANES * (j + 1)] for j in range(EXPERTS_PER_GROUP)]
    p = [scores[SUBLANES * j:SUBLANES * (j + 1)] for j in range(EXPERTS_PER_GROUP)]
    hi1, lo1 = jnp.maximum(s[0], s[1]), jnp.minimum(s[0], s[1])
    hi2, lo2 = jnp.maximum(s[2], s[3]), jnp.minimum(s[2], s[3])
    gs = jnp.maximum(hi1, hi2) + jnp.maximum(jnp.minimum(hi1, hi2), jnp.maximum(lo1, lo2))
    sub = lax.broadcasted_iota(I32, (N_GROUPS, tm), 0)
    gi = jnp.min(jnp.where(gs == jnp.max(gs, axis=0, keepdims=True), sub, N_GROUPS),
                 axis=0, keepdims=True)
    gm = sub == gi
    v = [jnp.sum(jnp.where(gm, sj, 0.0), axis=0, keepdims=True) for sj in s]
    pv = [jnp.sum(jnp.where(gm, pj, 0.0), axis=0, keepdims=True) for pj in p]
    best, i1 = v[0], jnp.zeros((1, tm), I32)
    for j in range(1, EXPERTS_PER_GROUP):
        u = v[j] > best
        best, i1 = jnp.where(u, v[j], best), jnp.where(u, j, i1)
    best2, i2 = jnp.full((1, tm), -jnp.inf, F32), jnp.zeros((1, tm), I32)
    for j in range(EXPERTS_PER_GROUP):
        cand = jnp.where(i1 == j, -jnp.inf, v[j])
        u = cand > best2
        best2, i2 = jnp.where(u, cand, best2), jnp.where(u, j, i2)

    def pick(idx):
        return jnp.where(idx == 0, pv[0], jnp.where(idx == 1, pv[1],
                                                    jnp.where(idx == 2, pv[2], pv[3])))

    p1, p2 = pick(i1), pick(i2)
    tot = p1 + p2
    first_lo = i1 < i2
    g_lo = jnp.where(first_lo, p1, p2) / tot
    g_hi = jnp.where(first_lo, p2, p1) / tot
    lo, hi = jnp.minimum(i1, i2), jnp.maximum(i1, i2)
    pair = jnp.where(hi == 1, 0, jnp.where(hi == 2, jnp.where(lo == 0, 1, 2),
                                           jnp.where(lo == 1, 3, jnp.where(lo == 2, 4, 5))))
    cls = gi * N_PAIRS + pair

    oh = lax.broadcasted_iota(I32, (CLASS_PAD, tm), 0) == cls
    pref = _dot(oh.astype(F32).astype(BF16), tri_ref[...])
    base = base_ref[...]
    rank = jnp.sum(jnp.where(oh, pref + base[:, 0:1] - 1.0, 0.0), axis=0, keepdims=True)
    base = base + pref[:, tm - 1:tm]
    base_ref[...] = base
    cnt_ref[...] = base

    sub8 = lax.broadcasted_iota(I32, (SUBLANES, tm), 0)
    rf_ref[...] = jnp.where(sub8 == 0, g_lo, jnp.where(sub8 == 1, g_hi, 0.0))
    ri_ref[...] = jnp.where(sub8 == 0, cls, jnp.where(sub8 == 1, rank.astype(I32), 0))


def _mixer_tail(x, h_loc, q_mem, tail_refs):
    (kbd_ref, vbd_ref, wout_ref, g_ref, b_ref, rwt_ref, rb_ref, tri_ref,
     x1_ref, rf_ref, ri_ref, cnt_ref, base_ref) = tail_refs

    @pl.when((pl.program_id(0) == 0) & (pl.program_id(1) == 0))
    def _():
        base_ref[...] = jnp.zeros_like(base_ref)

    s = _dot(q_mem.astype(BF16), kbd_ref[0])
    mem_len = s.shape[1] // XATT_HEADS
    sh = [s[:, h * mem_len:(h + 1) * mem_len] for h in range(XATT_HEADS)]
    mx = [jnp.max(t, axis=-1, keepdims=True) for t in sh]
    e = [jnp.exp(t - m) for t, m in zip(sh, mx)]
    tot = [jnp.sum(t, axis=-1, keepdims=True) for t in e]
    ps = [(t / n).astype(BF16) for t, n in zip(e, tot)]
    h_mem = _dot(jnp.concatenate(ps, axis=-1), vbd_ref[0])
    hcat = jnp.concatenate([h_loc.astype(BF16), h_mem.astype(BF16)], axis=-1)
    y = _dot(hcat, wout_ref[...])
    x1 = _layer_norm(ALPHA * x + y, g_ref[...], b_ref[...])
    _store_rows(x1_ref, x1)
    _route_tile(x1.astype(BF16), rwt_ref, rb_ref, tri_ref, base_ref, rf_ref, ri_ref, cnt_ref)


def _conv_half_kernel(x_ref, win_ref, cw_ref, *refs):
    tail_refs, carry_ref = refs[:-1], refs[-1]

    @pl.when(pl.program_id(1) == 0)
    def _():
        carry_ref[...] = jnp.zeros_like(carry_ref)

    x = _load_x(x_ref)
    tm = x.shape[0]
    proj = _dot(x.astype(BF16), win_ref[...])
    bg = proj[:, 0:MIX_W]
    cg = proj[:, MIX_W:2 * MIX_W]
    xin = proj[:, 2 * MIX_W:3 * MIX_W]
    q_mem = proj[:, 3 * MIX_W:]
    u = cg * xin
    prev = carry_ref[...]
    row = lax.broadcasted_iota(I32, u.shape, 0)
    u1 = jnp.where(row == 0, prev[7:8], pltpu.roll(u, 1, 0))
    u2 = jnp.where(row == 0, prev[6:7],
                   jnp.where(row == 1, prev[7:8], pltpu.roll(u, 2, 0)))
    carry_ref[...] = u[tm - 8:tm]
    cw = cw_ref[...]
    c = cw[0:1] * u2 + cw[1:2] * u1 + cw[2:3] * u
    _mixer_tail(x, bg * c, q_mem, tail_refs)


def _log_sigmoid(z):
    return -(jnp.maximum(-z, 0.0) + jnp.log1p(jnp.exp(-jnp.abs(z))))


def _mlstm_half_kernel(x_ref, win_ref, gb_ref, ng_ref, *refs):
    tail_refs, (state_ref, m_ref, h_ref) = refs[:-3], refs[-3:]

    @pl.when(pl.program_id(1) == 0)
    def _():
        state_ref[...] = jnp.zeros_like(state_ref)
        m_ref[...] = jnp.zeros_like(m_ref)

    x = _load_x(x_ref)
    tm = x.shape[0]
    proj = _dot(x.astype(BF16), win_ref[...])
    g0 = 4 * MIX_W + XATT_W
    gates = proj[:, g0:g0 + GATE_PAD] + gb_ref[...]
    row = lax.broadcasted_iota(I32, (CHUNK, GATE_PAD), 0)
    head_lane = lax.broadcasted_iota(I32, (tm, GATE_PAD), 1) < ML_HEADS
    log_i = jnp.where(head_lane, gates, 0.0)
    log_f = jnp.where(head_lane, pltpu.roll(_log_sigmoid(gates), GATE_PAD - ML_HEADS, 1), 0.0)
    causal = (lax.broadcasted_iota(I32, (CHUNK, CHUNK), 0)
              >= lax.broadcasted_iota(I32, (CHUNK, CHUNK), 1))
    ones_col = (lax.broadcasted_iota(I32, (CHUNK, ML_DH), 1) == 0).astype(BF16)

    def scan_rows(a, op, fill):
        sh = 1
        while sh < CHUNK:
            a = op(a, jnp.where(row >= sh, pltpu.roll(a, sh, 0), fill))
            sh *= 2
        return a

    chunks, heads = range(tm // CHUNK), range(ML_HEADS)
    pairs = [(c, h) for c in chunks for h in heads]

    m_prev = m_ref[...]
    gate = []
    for c in chunks:
        li = log_i[c * CHUNK:(c + 1) * CHUNK]
        b = scan_rows(log_f[c * CHUNK:(c + 1) * CHUNK], jnp.add, 0.0)
        wd = li - b
        m_t = b + jnp.maximum(scan_rows(wd, jnp.maximum, -jnp.inf), m_prev)
        g = b[CHUNK - 1:CHUNK, :]
        a = g + wd
        m_new = jnp.maximum(g + m_prev, jnp.max(a, axis=0, keepdims=True))
        gate.append(dict(u=b - m_t, w_inter=jnp.exp(b + m_prev - m_t), clamp=jnp.exp(-m_t),
                         decay=jnp.exp(g + m_prev - m_new), wd_t=wd.T,
                         wn_t=jnp.exp(a - m_new).T))
        m_prev = m_new
    m_ref[...] = m_prev

    col = lambda c, h, o=0: proj[c * CHUNK:(c + 1) * CHUNK, o + h * ML_DH:o + (h + 1) * ML_DH]
    qb = {p: col(*p).astype(BF16) for p in pairs}
    kf = {p: col(*p, MIX_W) * (ML_DH ** -0.5) for p in pairs}
    kb = {p: kf[p].astype(BF16) for p in pairs}
    vb = {p: col(*p, 2 * MIX_W).astype(BF16) for p in pairs}
    qk = {p: _dot_nt(qb[p], kb[p]) for p in pairs}
    s = {(c, h): qk[c, h] * jnp.exp(jnp.where(
        causal, gate[c]["u"][:, h:h + 1] + gate[c]["wd_t"][h:h + 1, :], -jnp.inf))
        for c, h in pairs}
    sv = {p: _dot(s[p].astype(BF16), vb[p]) for p in pairs}
    s_sum = {p: jnp.sum(s[p], axis=-1, keepdims=True) for p in pairs}
    ktw = {(c, h): (kf[c, h].T * gate[c]["wn_t"][h:h + 1, :]).astype(BF16) for c, h in pairs}
    kv = {p: _dot(ktw[p], jnp.concatenate([vb[p], ones_col], axis=-1)) for p in pairs}
    state = [state_ref[h] for h in heads]
    qs = {}
    for c in chunks:
        for h in heads:
            qs[c, h] = _dot(qb[c, h], state[h].astype(BF16))
        state = [gate[c]["decay"][:, h:h + 1] * state[h] + kv[c, h] for h in heads]
    for h in heads:
        state_ref[h] = state[h]
    wi = {(c, h): gate[c]["w_inter"][:, h:h + 1] for c, h in pairs}
    den = {p: s_sum[p] + wi[p] * qs[p][:, ML_DH:ML_DH + 1] for p in pairs}
    hh = {(c, h): (sv[c, h] + wi[c, h] * qs[c, h][:, 0:ML_DH])
          / jnp.maximum(jnp.abs(den[c, h]), gate[c]["clamp"][:, h:h + 1]) for c, h in pairs}
    hc = {p: hh[p] - jnp.mean(hh[p], axis=-1, keepdims=True) for p in pairs}
    var = {p: jnp.mean(hc[p] * hc[p], axis=-1, keepdims=True) for p in pairs}
    for c, h in pairs:
        h_ref[c * CHUNK:(c + 1) * CHUNK, h * ML_DH:(h + 1) * ML_DH] = (
            hc[c, h] * lax.rsqrt(var[c, h] + LN_EPS))

    o_gate = proj[:, 3 * MIX_W:4 * MIX_W]
    h_loc = jax.nn.sigmoid(o_gate) * (h_ref[...] * ng_ref[...])
    q_mem = proj[:, 4 * MIX_W:4 * MIX_W + XATT_W]
    _mixer_tail(x, h_loc, q_mem, tail_refs)


def _mixer_half(kernel_fn, tm, x, nb, seq, mixer_args, kbd, vbd, wout, ln_g, ln_b, rwt, rb,
                scratch, name):
    d = D_MODEL
    n_tok = nb * seq
    spb = seq // tm
    const2 = lambda b, s: (0, 0)
    tri = jnp.triu(jnp.ones((tm, tm), BF16))
    tail_args = [kbd, vbd, wout, ln_g.reshape(1, d), ln_b.reshape(1, d), rwt, rb, tri]
    row_map = lambda b, s: (b * spb + s, 0)
    x_spec = (pl.BlockSpec((tm, d), row_map) if x.shape[1] == d
              else pl.BlockSpec((tm * ROW_TILES, LANES), row_map))
    in_specs = ([x_spec]
                + [pl.BlockSpec(a.shape, const2) for a in mixer_args]
                + [pl.BlockSpec((1,) + kbd.shape[1:], lambda b, s: (b, 0, 0)),
                   pl.BlockSpec((1,) + vbd.shape[1:], lambda b, s: (b, 0, 0))]
                + [pl.BlockSpec(a.shape, const2) for a in tail_args[2:]])
    tok_spec = lambda rows: pl.BlockSpec((rows, tm), lambda b, s: (0, b * spb + s))
    return pl.pallas_call(
        kernel_fn,
        grid=(nb, spb),
        in_specs=in_specs,
        out_specs=[pl.BlockSpec((tm * ROW_TILES, LANES), row_map),
                   tok_spec(SUBLANES), tok_spec(SUBLANES),
                   pl.BlockSpec((CLASS_PAD, LANES), const2)],
        out_shape=[jax.ShapeDtypeStruct((n_tok * ROW_TILES, LANES), F32),
                   jax.ShapeDtypeStruct((SUBLANES, n_tok), F32),
                   jax.ShapeDtypeStruct((SUBLANES, n_tok), I32),
                   jax.ShapeDtypeStruct((CLASS_PAD, LANES), F32)],
        scratch_shapes=[pltpu.VMEM((CLASS_PAD, LANES), F32)] + scratch,
        compiler_params=pltpu.CompilerParams(
            dimension_semantics=("arbitrary", "arbitrary"), vmem_limit_bytes=VMEM_LIMIT),
        name=name,
    )(x, *mixer_args, *tail_args)


def _moe_kernel(layer, n_blocks,
                order_ref, off_ref, nvalid_ref, trash_ref, e1_ref, e2_ref, nblk_ref,
                x_hbm, g_hbm, wg_hbm, wu_hbm, wd_hbm, lng_ref, lnb_ref, out_hbm,
                xbuf, gbuf, obuf, zbuf, acc, sg, su, sd, wg_bf, wu_bf, wd_bf,
                gsem, ssem, zsem, wsem):
    nblk = nblk_ref[0]
    last = nblk - 1
    ring = lambda j: (j + RING) % RING
    n_res = W_SETS * EXPERTS_PER_GROUP

    def gate_copy(blk, slt):
        return pltpu.make_async_copy(g_hbm.at[pl.ds(off_ref[blk + 1], MOE_ROWS)], gbuf.at[slt],
                                     gsem.at[slt])

    def tile_row(r):
        return r * ROW_TILES if isinstance(r, int) else pl.multiple_of(r * ROW_TILES, ROW_TILES)

    def gather_row(base, r, slt):
        t8 = pl.multiple_of(order_ref[base + r], ROW_TILES)
        pltpu.make_async_copy(x_hbm.at[pl.ds(t8, ROW_TILES)],
                              xbuf.at[slt, pl.ds(tile_row(r), ROW_TILES)], gsem.at[slt]).start()

    def issue_gather(blk, slt, unrolled=True):
        base = off_ref[blk + 1]
        if unrolled:
            for r in range(MOE_ROWS):
                gather_row(base, r, slt)
        else:
            lax.fori_loop(0, MOE_ROWS, lambda r, c: (gather_row(base, r, slt), c)[1], 0)
        gate_copy(blk, slt).start()

    def wait_gather(slt):
        pltpu.make_async_copy(x_hbm.at[pl.ds(0, MOE_ROWS * ROW_TILES)], xbuf.at[slt],
                              gsem.at[slt]).wait()
        gate_copy(0, slt).wait()

    def scatter_row(base, nvalid, trash8, r, slt):
        t8 = jnp.where(r < nvalid, order_ref[base + r], trash8 + r * ROW_TILES)
        pltpu.make_async_copy(
            obuf.at[slt, pl.ds(tile_row(r), ROW_TILES)],
            out_hbm.at[pl.ds(pl.multiple_of(t8, ROW_TILES), ROW_TILES)], ssem.at[slt]).start()

    def issue_scatter(blk, slt, unrolled=True):
        args = (off_ref[blk + 1], nvalid_ref[blk + 1], trash_ref[blk + 1])
        if unrolled:
            for r in range(MOE_ROWS):
                scatter_row(*args, r, slt)
        else:
            lax.fori_loop(0, MOE_ROWS, lambda r, c: (scatter_row(*args, r, slt), c)[1], 0)

    def wait_scatter(slt):
        pltpu.make_async_copy(obuf.at[slt], out_hbm.at[pl.ds(0, MOE_ROWS * ROW_TILES)],
                              ssem.at[slt]).wait()

    def weight_copies(k, stage):
        return [pltpu.make_async_copy(src.at[layer, k], dst.at[stage], wsem.at[stage])
                for src, dst in ((wg_hbm, sg), (wu_hbm, su), (wd_hbm, sd))]

    def load_start(k):
        for cp in weight_copies(k, k % W_STAGE):
            cp.start()

    def load_wait(k):
        for cp in weight_copies(k, k % W_STAGE):
            cp.wait()

    def load_finish(k):
        load_wait(k)
        stage, res = k % W_STAGE, k % n_res
        wg_bf[res] = sg[stage].astype(BF16)
        wu_bf[res] = su[stage].astype(BF16)
        wd_bf[res] = sd[stage].astype(BF16)

    def start_if(pred, kd):
        @pl.when(pred)
        def _():
            load_start(kd)
        return kd + pred.astype(I32)

    def service_loader(group, kd, kc):
        cap = jnp.minimum((group + W_SETS) * EXPERTS_PER_GROUP, N_EXPERTS)
        need = (group + 1) * EXPERTS_PER_GROUP

        def catch_up(c):
            kd, kc = c
            kd = start_if(kd <= kc, kd)
            kd = start_if((kd < kc + W_STAGE) & (kd < cap), kd)
            load_finish(kc)
            return kd, kc + 1
        kd, kc = lax.while_loop(lambda c: c[1] < need, catch_up, (kd, kc))

        ready = (kc < kd) & (kc < cap)

        @pl.when(ready)
        def _():
            load_finish(kc)
        kc = kc + ready.astype(I32)
        kd = start_if((kd < kc + W_STAGE) & (kd < cap), kd)
        return kd, kc

    def ffn(xb, res):
        hg = _dot(xb, wg_bf[res])
        hu = _dot(xb, wu_bf[res])
        return _dot((hg * jax.nn.sigmoid(hg) * hu).astype(BF16), wd_bf[res])

    issue_gather(0, 0, unrolled=False)
    issue_gather(jnp.minimum(1, last), 1, unrolled=False)
    obuf[RING - 1] = jnp.zeros(obuf.shape[1:], F32)

    def block(b, loader):
        slot = b % RING
        e1, e2 = e1_ref[b], e2_ref[b]
        loader = service_loader(e1 // EXPERTS_PER_GROUP, *loader)

        @pl.when(b >= RING - 1)
        def _():
            wait_scatter(slot)

        wait_gather(slot)
        x = _load_rows(xbuf.at[slot], MOE_ROWS)
        xb = x.astype(BF16)
        issue_gather(jnp.minimum(b + 2, last), ring(b + 2))
        issue_scatter(b - 1, ring(b - 1))
        res = (e1 % n_res, e2 % n_res)
        hg = [_dot(xb, wg_bf[r]) for r in res]
        hu = [_dot(xb, wu_bf[r]) for r in res]
        act = [(g * jax.nn.sigmoid(g) * u).astype(BF16) for g, u in zip(hg, hu)]
        y = [_dot(a, wd_bf[r]) for a, r in zip(act, res)]
        gates = gbuf[slot]
        m = gates[:, 0:1] * y[0] + gates[:, 1:2] * y[1]
        o = _layer_norm(ALPHA * x + m, lng_ref[...], lnb_ref[...])
        _store_rows(obuf.at[slot], o)
        return loader

    first = (e1_ref[0] // EXPERTS_PER_GROUP) * EXPERTS_PER_GROUP
    kd, kc = lax.fori_loop(0, nblk, block, (first, first))

    lax.fori_loop(kc, kd, lambda k, c: (load_wait(k), c)[1], 0)
    issue_scatter(last, ring(last), unrolled=False)
    for slt in range(RING):
        wait_scatter(slt)
    wait_gather(ring(nblk))
    wait_gather(ring(nblk + 1))

    zbuf[...] = jnp.zeros_like(zbuf)

    def fill_copy(j):
        return pltpu.make_async_copy(
            zbuf, out_hbm.at[pl.ds(j * MOE_ROWS * ROW_TILES, MOE_ROWS * ROW_TILES)], zsem)
    lax.fori_loop(nblk, n_blocks, lambda j, c: (fill_copy(j).start(), c)[1], 0)
    lax.fori_loop(nblk, n_blocks, lambda j, c: (fill_copy(j).wait(), c)[1], 0)


def _moe_half(layer, x1, plan, gates_sorted, w_gate, w_up, w_down, ln_g, ln_b):
    n_blocks = plan[4].shape[0]
    n_rows = (n_blocks + 1) * MOE_ROWS
    d, f = D_MODEL, D_FF
    n_res = W_SETS * EXPERTS_PER_GROUP
    const_spec = pl.BlockSpec((1, d), lambda i, *_: (0, 0))
    any_spec = pl.BlockSpec(memory_space=pl.ANY)
    grid_spec = pltpu.PrefetchScalarGridSpec(
        num_scalar_prefetch=7,
        grid=(1,),
        in_specs=[any_spec, any_spec, any_spec, any_spec, any_spec, const_spec, const_spec],
        out_specs=any_spec,
        scratch_shapes=[
            pltpu.VMEM((RING, MOE_ROWS * ROW_TILES, LANES), F32),
            pltpu.VMEM((RING, MOE_ROWS, LANES), F32),
            pltpu.VMEM((RING, MOE_ROWS * ROW_TILES, LANES), F32),
            pltpu.VMEM((MOE_ROWS * ROW_TILES, LANES), F32),
            pltpu.VMEM((MOE_ROWS, d), F32),
            pltpu.VMEM((W_STAGE, d, f), F32), pltpu.VMEM((W_STAGE, d, f), F32),
            pltpu.VMEM((W_STAGE, f, d), F32),
            pltpu.VMEM((n_res, d, f), BF16), pltpu.VMEM((n_res, d, f), BF16),
            pltpu.VMEM((n_res, f, d), BF16),
            pltpu.SemaphoreType.DMA((RING,)),
            pltpu.SemaphoreType.DMA((RING,)),
            pltpu.SemaphoreType.DMA(()),
            pltpu.SemaphoreType.DMA((W_STAGE,)),
        ],
    )
    return pl.pallas_call(
        functools.partial(_moe_kernel, layer, n_blocks),
        grid_spec=grid_spec,
        out_shape=jax.ShapeDtypeStruct((n_rows * ROW_TILES, LANES), F32),
        compiler_params=pltpu.CompilerParams(
            dimension_semantics=("arbitrary",), vmem_limit_bytes=VMEM_LIMIT),
        name="moe_half",
    )(*plan, x1, gates_sorted, w_gate, w_up, w_down, ln_g.reshape(1, d), ln_b.reshape(1, d))


def _plan_blocks(route_f, route_i, counts_f, n_tok):
    cls, rank = route_i[0], route_i[1]
    counts = counts_f[:N_CLASSES, 0].astype(I32)
    padded = (counts + MOE_ROWS - 1) // MOE_ROWS * MOE_ROWS
    pad_end = jnp.cumsum(padded)
    pad_start = pad_end - padded
    start = jnp.cumsum(counts) - counts
    class_ids = jnp.arange(N_CLASSES, dtype=I32)
    dest = jnp.sum(jnp.where(cls[:, None] == class_ids[None], pad_start[None], 0), axis=1) + rank
    _, order, g_lo, g_hi = lax.sort(
        (dest, jnp.arange(n_tok, dtype=I32), route_f[0], route_f[1]), num_keys=1)
    order = jnp.pad(order, (0, MOE_ROWS)) * ROW_TILES
    gates_sorted = jnp.pad(jnp.stack([g_lo, g_hi], axis=1), ((0, MOE_ROWS), (0, LANES - 2)))

    n_blocks = n_tok // MOE_ROWS + N_CLASSES
    blk_ids = jnp.arange(n_blocks, dtype=I32)
    blk_start = blk_ids * MOE_ROWS
    nblk = pad_end[-1] // MOE_ROWS
    blk_cls = jnp.minimum(jnp.sum(pad_end[None] <= blk_start[:, None], axis=1), N_CLASSES - 1)
    blk_cls = jnp.where(blk_ids < nblk, blk_cls, blk_cls[jnp.maximum(nblk - 1, 0)]).astype(I32)
    in_cls = blk_start - pad_start[blk_cls]
    active = blk_ids < nblk
    blk_off = jnp.where(active, start[blk_cls] + in_cls, 0).astype(I32)
    nvalid = jnp.where(active, jnp.clip(counts[blk_cls] - in_cls, 0, MOE_ROWS), 0).astype(I32)
    trash = ((n_tok + blk_start - blk_off - nvalid) * ROW_TILES).astype(I32)
    pair_lo = jnp.array(PAIR_LO, I32)
    pair_hi = jnp.array(PAIR_HI, I32)
    blk_e1 = (blk_cls // N_PAIRS) * EXPERTS_PER_GROUP + pair_lo[blk_cls % N_PAIRS]
    blk_e2 = (blk_cls // N_PAIRS) * EXPERTS_PER_GROUP + pair_hi[blk_cls % N_PAIRS]
    lead = lambda a, v: jnp.concatenate([jnp.full((1,), v, I32), a])
    plan = (order, lead(blk_off, 0), lead(nvalid, 0),
            lead(trash, n_blocks * MOE_ROWS * ROW_TILES),
            blk_e1, blk_e2, nblk.astype(I32).reshape(1))
    return plan, gates_sorted


def kernel(x, mem, conv_w_in, conv_w, mlstm_w_in, mlstm_gate_b, mlstm_norm_g, w_kv_mem, w_out,
           ln1_g, ln1_b, ln2_g, ln2_b, router_w, router_b, w_gate, w_up, w_down):
    nb, seq, d = x.shape
    n_tok = nb * seq
    mem_len = mem.shape[1]
    assert d == D_MODEL and seq % TM_CONV == 0 and seq % TM_MLSTM == 0 and TM_MLSTM % CHUNK == 0
    assert n_tok // MOE_ROWS >= RING

    w_kv_all = jnp.transpose(w_kv_mem, (1, 0, 2)).reshape(d, DEPTH * 2 * XATT_W).astype(BF16)
    kv = _matmul(mem.reshape(nb * mem_len, d), w_kv_all, 512, 1024)
    kv = kv.reshape(nb, mem_len, DEPTH, 2, XATT_HEADS, XATT_DH)
    eye = jnp.eye(XATT_HEADS, dtype=bool)

    perm = lambda a: jnp.swapaxes(a.reshape(N_GROUPS, EXPERTS_PER_GROUP, -1), 0, 1)
    rwt = perm(router_w.T).reshape(N_EXPERTS, d).astype(BF16)
    rb = perm(router_b).reshape(N_EXPERTS, 1)

    xf = x.reshape(n_tok, d)
    for i in range(DEPTH):
        j = i // 2
        k = kv[:, :, i, 0] * (XATT_DH ** -0.5)
        v = kv[:, :, i, 1]
        kbd = jnp.where(eye[None, :, None, :, None],
                        jnp.transpose(k, (0, 2, 3, 1))[:, :, :, None, :], 0.0)
        kbd = kbd.reshape(nb, XATT_W, XATT_HEADS * mem_len).astype(BF16)
        vbd = jnp.where(eye[None, :, None, :, None],
                        jnp.transpose(v, (0, 2, 1, 3))[:, :, :, None, :], 0.0)
        vbd = vbd.reshape(nb, XATT_HEADS * mem_len, XATT_W).astype(BF16)
        wout = w_out[i].astype(BF16)
        if i % 2 == 0:
            win = conv_w_in[j].astype(BF16)
            outs = _mixer_half(_conv_half_kernel, TM_CONV, xf, nb, seq, [win, conv_w[j]],
                               kbd, vbd, wout, ln1_g[i], ln1_b[i], rwt, rb,
                               [pltpu.VMEM((8, MIX_W), F32)], "conv_half")
        else:
            w = mlstm_w_in[j]
            o0 = 4 * MIX_W
            w_gates = jnp.pad(w[:, o0:o0 + 2 * ML_HEADS], ((0, 0), (0, GATE_PAD - 2 * ML_HEADS)))
            win = jnp.concatenate([w[:, :o0], w[:, o0 + 2 * ML_HEADS:], w_gates], axis=1).astype(BF16)
            gate_b = jnp.pad(mlstm_gate_b[j], (0, GATE_PAD - 2 * ML_HEADS)).reshape(1, GATE_PAD)
            norm_g = mlstm_norm_g[j].reshape(1, MIX_W)
            outs = _mixer_half(_mlstm_half_kernel, TM_MLSTM, xf, nb, seq, [win, gate_b, norm_g],
                               kbd, vbd, wout, ln1_g[i], ln1_b[i], rwt, rb,
                               [pltpu.VMEM((ML_HEADS, ML_DH, 2 * ML_DH), F32),
                                pltpu.VMEM((1, LANES), F32),
                                pltpu.VMEM((TM_MLSTM, MIX_W), F32)], "mlstm_half")
        x1, route_f, route_i, counts_f = outs
        plan, gates_sorted = _plan_blocks(route_f, route_i, counts_f, n_tok)
        xf = _moe_half(i, x1, plan, gates_sorted, w_gate, w_up, w_down, ln2_g[i], ln2_b[i])
    return xf[:n_tok * ROW_TILES].reshape(nb, seq, d)
```

```python
import functools

import jax
import jax.numpy as jnp
from jax import lax
from jax.experimental import pallas as pl
from jax.experimental.pallas import tpu as pltpu

F32 = jnp.float32
BF16 = jnp.bfloat16
I32 = jnp.int32

D_MODEL = 1024
DEPTH = 4
MIX_W = 768
XATT_HEADS = 4
XATT_DH = 64
XATT_W = XATT_HEADS * XATT_DH
CONV_K = 3
ML_HEADS = 6
ML_DH = MIX_W // ML_HEADS
CHUNK = 128
N_EXPERTS = 32
N_GROUPS = 8
EXPERTS_PER_GROUP = N_EXPERTS // N_GROUPS
D_FF = 512
ALPHA = (2.0 * DEPTH) ** 0.25
LN_EPS = 1e-5

LANES = 128
SUBLANES = 8
ROW_TILES = D_MODEL // LANES
GATE_PAD = LANES
MOE_ROWS = 128
RING = 3
W_SETS = 2
W_STAGE = 2
PAIR_LO = (0, 0, 1, 1, 2, 0)
PAIR_HI = (1, 2, 2, 3, 3, 3)
N_PAIRS = len(PAIR_LO)
N_CLASSES = N_GROUPS * N_PAIRS
CLASS_PAD = 64
VMEM_LIMIT = 52 * 1024 * 1024

TM_CONV = 512
TM_MLSTM = 256


def _layer_norm(z, g, b):
    mu = jnp.mean(z, axis=-1, keepdims=True)
    zc = z - mu
    var = jnp.mean(zc * zc, axis=-1, keepdims=True)
    return zc * lax.rsqrt(var + LN_EPS) * g + b


def _load_rows(ref, n):
    return jnp.concatenate([ref[pl.ds(j, n, stride=ROW_TILES), :] for j in range(ROW_TILES)],
                           axis=-1)


def _load_x(x_ref):
    if x_ref.shape[1] == D_MODEL:
        return x_ref[...]
    return _load_rows(x_ref, x_ref.shape[0] // ROW_TILES)


def _store_rows(ref, val):
    for j in range(ROW_TILES):
        ref[pl.ds(j, val.shape[0], stride=ROW_TILES), :] = val[:, j * LANES:(j + 1) * LANES]


def _dot(a, b):
    return jnp.dot(a, b, preferred_element_type=F32)


def _dot_nt(a, b):
    return lax.dot_general(a, b, (((1,), (1,)), ((), ())), preferred_element_type=F32)


def _kv_kernel(mem_ref, w_ref, kbd_ref, vbd_ref):
    kv = _dot(mem_ref[0].astype(BF16), w_ref[0].astype(BF16))
    mem_len = kv.shape[0]
    kt = (kv[:, :XATT_W] * (XATT_DH ** -0.5)).T
    v = kv[:, XATT_W:]
    row_head = lax.broadcasted_iota(I32, kt.shape, 0) // XATT_DH
    col_head = lax.broadcasted_iota(I32, v.shape, 1) // XATT_DH
    for h in range(XATT_HEADS):
        kbd_ref[0, 0, :, h * mem_len:(h + 1) * mem_len] = (
            jnp.where(row_head == h, kt, 0.0).astype(BF16))
        vbd_ref[0, 0, h * mem_len:(h + 1) * mem_len, :] = (
            jnp.where(col_head == h, v, 0.0).astype(BF16))


def _memory_kv(mem, w_kv_mem):
    nb, mem_len, d = mem.shape
    depth = w_kv_mem.shape[0]
    return pl.pallas_call(
        _kv_kernel,
        grid=(depth, nb),
        in_specs=[pl.BlockSpec((1, mem_len, d), lambda i, b: (b, 0, 0)),
                  pl.BlockSpec((1, d, 2 * XATT_W), lambda i, b: (i, 0, 0))],
        out_specs=[pl.BlockSpec((1, 1, XATT_W, XATT_HEADS * mem_len), lambda i, b: (i, b, 0, 0)),
                   pl.BlockSpec((1, 1, XATT_HEADS * mem_len, XATT_W), lambda i, b: (i, b, 0, 0))],
        out_shape=[jax.ShapeDtypeStruct((depth, nb, XATT_W, XATT_HEADS * mem_len), BF16),
                   jax.ShapeDtypeStruct((depth, nb, XATT_HEADS * mem_len, XATT_W), BF16)],
        compiler_params=pltpu.CompilerParams(
            dimension_semantics=("arbitrary", "arbitrary"), vmem_limit_bytes=VMEM_LIMIT),
        name="memory_kv",
    )(mem, w_kv_mem)


def _route_tile(x1b, rwt_ref, rb_ref, tri_ref, base_ref, rf_ref, ri_ref, cnt_ref):
    tm = x1b.shape[0]
    scores = jax.nn.sigmoid(_dot_nt(rwt_ref[...], x1b))
    sel = scores + rb_ref[...]
    s = [sel[SUBLANES * j:SUBLANES * (j + 1)] for j in range(EXPERTS_PER_GROUP)]
    p = [scores[SUBLANES * j:SUBLANES * (j + 1)] for j in range(EXPERTS_PER_GROUP)]
    hi1, lo1 = jnp.maximum(s[0], s[1]), jnp.minimum(s[0], s[1])
    hi2, lo2 = jnp.maximum(s[2], s[3]), jnp.minimum(s[2], s[3])
    gs = jnp.maximum(hi1, hi2) + jnp.maximum(jnp.minimum(hi1, hi2), jnp.maximum(lo1, lo2))
    sub = lax.broadcasted_iota(I32, (N_GROUPS, tm), 0)
    gi = jnp.min(jnp.where(gs == jnp.max(gs, axis=0, keepdims=True), sub, N_GROUPS),
                 axis=0, keepdims=True)
    gm = sub == gi
    v = [jnp.sum(jnp.where(gm, sj, 0.0), axis=0, keepdims=True) for sj in s]
    pv = [jnp.sum(jnp.where(gm, pj, 0.0), axis=0, keepdims=True) for pj in p]
    best, i1 = v[0], jnp.zeros((1, tm), I32)
    for j in range(1, EXPERTS_PER_GROUP):
        u = v[j] > best
        best, i1 = jnp.where(u, v[j], best), jnp.where(u, j, i1)
    best2, i2 = jnp.full((1, tm), -jnp.inf, F32), jnp.zeros((1, tm), I32)
    for j in range(EXPERTS_PER_GROUP):
        cand = jnp.where(i1 == j, -jnp.inf, v[j])
        u = cand > best2
        best2, i2 = jnp.where(u, cand, best2), jnp.where(u, j, i2)

    def pick(idx):
        return jnp.where(idx == 0, pv[0], jnp.where(idx == 1, pv[1],
                                                    jnp.where(idx == 2, pv[2], pv[3])))

    p1, p2 = pick(i1), pick(i2)
    tot = p1 + p2
    first_lo = i1 < i2
    g_lo = jnp.where(first_lo, p1, p2) / tot
    g_hi = jnp.where(first_lo, p2, p1) / tot
    lo, hi = jnp.minimum(i1, i2), jnp.maximum(i1, i2)
    pair = jnp.where(hi == 1, 0, jnp.where(hi == 2, jnp.where(lo == 0, 1, 2),
                                           jnp.where(lo == 1, 3, jnp.where(lo == 2, 4, 5))))
    cls = gi * N_PAIRS + pair

    oh = lax.broadcasted_iota(I32, (CLASS_PAD, tm), 0) == cls
    pref = _dot(oh.astype(F32).astype(BF16), tri_ref[...])
    base = base_ref[...]
    rank = jnp.sum(jnp.where(oh, pref + base[:, 0:1] - 1.0, 0.0), axis=0, keepdims=True)
    base = base + pref[:, tm - 1:tm]
    base_ref[...] = base
    cnt_ref[...] = base

    sub8 = lax.broadcasted_iota(I32, (SUBLANES, tm), 0)
    rf_ref[...] = jnp.where(sub8 == 0, g_lo, jnp.where(sub8 == 1, g_hi, 0.0))
    ri_ref[...] = jnp.where(sub8 == 0, cls, jnp.where(sub8 == 1, rank.astype(I32), 0))


def _init_tail(tail_refs):
    base_ref = tail_refs[-1]

    @pl.when((pl.program_id(0) == 0) & (pl.program_id(1) == 0))
    def _():
        base_ref[...] = jnp.zeros_like(base_ref)


def _cross_attention(q_mem, tail_refs):
    kbd_ref, vbd_ref = tail_refs[:2]
    s = _dot(q_mem.astype(BF16), kbd_ref[0, 0])
    mem_len = s.shape[1] // XATT_HEADS
    sh = [s[:, h * mem_len:(h + 1) * mem_len] for h in range(XATT_HEADS)]
    mx = [jnp.max(t, axis=-1, keepdims=True) for t in sh]
    e = [jnp.exp(t - m) for t, m in zip(sh, mx)]
    tot = [jnp.sum(t, axis=-1, keepdims=True) for t in e]
    ps = [(t / n).astype(BF16) for t, n in zip(e, tot)]
    return _dot(jnp.concatenate(ps, axis=-1), vbd_ref[0, 0])


def _mixer_tail(x, h_loc, h_mem, tail_refs):
    (_, _, wout_ref, g_ref, b_ref, rwt_ref, rb_ref, tri_ref,
     x1_ref, rf_ref, ri_ref, cnt_ref, base_ref) = tail_refs
    hcat = jnp.concatenate([h_loc.astype(BF16), h_mem.astype(BF16)], axis=-1)
    y = _dot(hcat, wout_ref[...])
    x1 = _layer_norm(ALPHA * x + y, g_ref[...], b_ref[...])
    _store_rows(x1_ref, x1)
    _route_tile(x1.astype(BF16), rwt_ref, rb_ref, tri_ref, base_ref, rf_ref, ri_ref, cnt_ref)


def _conv_half_kernel(x_ref, win_ref, cw_ref, *refs):
    tail_refs, carry_ref = refs[:-1], refs[-1]
    _init_tail(tail_refs)

    @pl.when(pl.program_id(1) == 0)
    def _():
        carry_ref[...] = jnp.zeros_like(carry_ref)

    x = _load_x(x_ref)
    tm = x.shape[0]
    proj = _dot(x.astype(BF16), win_ref[...])
    bg = proj[:, 0:MIX_W]
    cg = proj[:, MIX_W:2 * MIX_W]
    xin = proj[:, 2 * MIX_W:3 * MIX_W]
    h_mem = _cross_attention(proj[:, 3 * MIX_W:], tail_refs)
    u = cg * xin
    prev = carry_ref[...]
    row = lax.broadcasted_iota(I32, u.shape, 0)
    u1 = jnp.where(row == 0, prev[7:8], pltpu.roll(u, 1, 0))
    u2 = jnp.where(row == 0, prev[6:7],
                   jnp.where(row == 1, prev[7:8], pltpu.roll(u, 2, 0)))
    carry_ref[...] = u[tm - 8:tm]
    cw = cw_ref[...]
    c = cw[0:1] * u2 + cw[1:2] * u1 + cw[2:3] * u
    _mixer_tail(x, bg * c, h_mem, tail_refs)


def _log_sigmoid(z):
    return -(jnp.maximum(-z, 0.0) + jnp.log1p(jnp.exp(-jnp.abs(z))))


def _mlstm_half_kernel(x_ref, win_ref, gb_ref, ng_ref, *refs):
    tail_refs, (state_ref, m_ref, h_ref) = refs[:-3], refs[-3:]

    @pl.when(pl.program_id(1) == 0)
    def _():
        state_ref[...] = jnp.zeros_like(state_ref)
        m_ref[...] = jnp.zeros_like(m_ref)

    x = _load_x(x_ref)
    tm = x.shape[0]
    n_cell = 3 * MIX_W + GATE_PAD
    xb = x.astype(BF16)
    proj = _dot(xb, win_ref[:, :n_cell])
    gates = proj[:, 3 * MIX_W:n_cell] + gb_ref[...]
    row = lax.broadcasted_iota(I32, (CHUNK, GATE_PAD), 0)
    head_lane = lax.broadcasted_iota(I32, (tm, GATE_PAD), 1) < ML_HEADS
    log_i = jnp.where(head_lane, gates, 0.0)
    log_f = jnp.where(head_lane, pltpu.roll(_log_sigmoid(gates), GATE_PAD - ML_HEADS, 1), 0.0)
    causal = (lax.broadcasted_iota(I32, (CHUNK, CHUNK), 0)
              >= lax.broadcasted_iota(I32, (CHUNK, CHUNK), 1))
    ones_col = (lax.broadcasted_iota(I32, (CHUNK, ML_DH), 1) == 0).astype(BF16)

    def scan_rows(a, op, fill):
        sh = 1
        while sh < CHUNK:
            a = op(a, jnp.where(row >= sh, pltpu.roll(a, sh, 0), fill))
            sh *= 2
        return a

    chunks, heads = range(tm // CHUNK), range(ML_HEADS)
    pairs = [(c, h) for c in chunks for h in heads]

    m_prev = m_ref[...]
    gate = []
    for c in chunks:
        li = log_i[c * CHUNK:(c + 1) * CHUNK]
        b = scan_rows(log_f[c * CHUNK:(c + 1) * CHUNK], jnp.add, 0.0)
        wd = li - b
        m_t = b + jnp.maximum(scan_rows(wd, jnp.maximum, -jnp.inf), m_prev)
        g = b[CHUNK - 1:CHUNK, :]
        a = g + wd
        m_new = jnp.maximum(g + m_prev, jnp.max(a, axis=0, keepdims=True))
        gate.append(dict(u=b - m_t, w_inter=jnp.exp(b + m_prev - m_t), clamp=jnp.exp(-m_t),
                         decay=jnp.exp(g + m_prev - m_new), wd_t=wd.T,
                         wn_t=jnp.exp(a - m_new).T))
        m_prev = m_new
    m_ref[...] = m_prev

    col = lambda c, h, o=0: proj[c * CHUNK:(c + 1) * CHUNK, o + h * ML_DH:o + (h + 1) * ML_DH]
    qb = {p: col(*p).astype(BF16) for p in pairs}
    kf = {p: col(*p, MIX_W) * (ML_DH ** -0.5) for p in pairs}
    kb = {p: kf[p].astype(BF16) for p in pairs}
    vb = {p: col(*p, 2 * MIX_W).astype(BF16) for p in pairs}
    qk = {p: _dot_nt(qb[p], kb[p]) for p in pairs}
    proj_tail = _dot(xb, win_ref[:, n_cell:])
    s = {(c, h): qk[c, h] * jnp.exp(jnp.where(
        causal, gate[c]["u"][:, h:h + 1] + gate[c]["wd_t"][h:h + 1, :], -jnp.inf))
        for c, h in pairs}
    sv = {p: _dot(s[p].astype(BF16), vb[p]) for p in pairs}
    s_sum = {p: jnp.sum(s[p], axis=-1, keepdims=True) for p in pairs}
    ktw = {(c, h): (kf[c, h].T * gate[c]["wn_t"][h:h + 1, :]).astype(BF16) for c, h in pairs}
    kv = {p: _dot(ktw[p], jnp.concatenate([vb[p], ones_col], axis=-1)) for p in pairs}
    state = [state_ref[h] for h in heads]
    qs = {}
    for c in chunks:
        for h in heads:
            qs[c, h] = _dot(qb[c, h], state[h].astype(BF16))
        state = [gate[c]["decay"][:, h:h + 1] * state[h] + kv[c, h] for h in heads]
    for h in heads:
        state_ref[h] = state[h]
    wi = {(c, h): gate[c]["w_inter"][:, h:h + 1] for c, h in pairs}
    den = {p: s_sum[p] + wi[p] * qs[p][:, ML_DH:ML_DH + 1] for p in pairs}
    hh = {(c, h): (sv[c, h] + wi[c, h] * qs[c, h][:, 0:ML_DH])
          / jnp.maximum(jnp.abs(den[c, h]), gate[c]["clamp"][:, h:h + 1]) for c, h in pairs}
    hc = {p: hh[p] - jnp.mean(hh[p], axis=-1, keepdims=True) for p in pairs}
    var = {p: jnp.mean(hc[p] * hc[p], axis=-1, keepdims=True) for p in pairs}
    for c, h in pairs:
        h_ref[c * CHUNK:(c + 1) * CHUNK, h * ML_DH:(h + 1) * ML_DH] = (
            hc[c, h] * lax.rsqrt(var[c, h] + LN_EPS))

    h_loc = jax.nn.sigmoid(proj_tail[:, 0:MIX_W]) * (h_ref[...] * ng_ref[...])
    _init_tail(tail_refs)
    h_mem = _cross_attention(proj_tail[:, MIX_W:], tail_refs)
    _mixer_tail(x, h_loc, h_mem, tail_refs)


def _mixer_half(kernel_fn, tm, x, nb, seq, mixer_args, layer, kbd, vbd, wout, ln_g, ln_b, rwt, rb,
                scratch, name):
    d = D_MODEL
    n_tok = nb * seq
    spb = seq // tm
    const2 = lambda b, s: (0, 0)
    tri = jnp.triu(jnp.ones((tm, tm), BF16))
    tail_args = [kbd, vbd, wout, ln_g.reshape(1, d), ln_b.reshape(1, d), rwt, rb, tri]
    row_map = lambda b, s: (b * spb + s, 0)
    x_spec = (pl.BlockSpec((tm, d), row_map) if x.shape[1] == d
              else pl.BlockSpec((tm * ROW_TILES, LANES), row_map))
    in_specs = ([x_spec]
                + [pl.BlockSpec(a.shape, const2) for a in mixer_args]
                + [pl.BlockSpec((1, 1) + kbd.shape[2:], lambda b, s: (layer, b, 0, 0)),
                   pl.BlockSpec((1, 1) + vbd.shape[2:], lambda b, s: (layer, b, 0, 0))]
                + [pl.BlockSpec(a.shape, const2) for a in tail_args[2:]])
    tok_spec = lambda rows: pl.BlockSpec((rows, tm), lambda b, s: (0, b * spb + s))
    return pl.pallas_call(
        kernel_fn,
        grid=(nb, spb),
        in_specs=in_specs,
        out_specs=[pl.BlockSpec((tm * ROW_TILES, LANES), row_map),
                   tok_spec(SUBLANES), tok_spec(SUBLANES),
                   pl.BlockSpec((CLASS_PAD, LANES), const2)],
        out_shape=[jax.ShapeDtypeStruct((n_tok * ROW_TILES, LANES), F32),
                   jax.ShapeDtypeStruct((SUBLANES, n_tok), F32),
                   jax.ShapeDtypeStruct((SUBLANES, n_tok), I32),
                   jax.ShapeDtypeStruct((CLASS_PAD, LANES), F32)],
        scratch_shapes=[pltpu.VMEM((CLASS_PAD, LANES), F32)] + scratch,
        compiler_params=pltpu.CompilerParams(
            dimension_semantics=("arbitrary", "arbitrary"), vmem_limit_bytes=VMEM_LIMIT),
        name=name,
    )(x, *mixer_args, *tail_args)


def _moe_kernel(layer, n_blocks,
                order_ref, off_ref, nvalid_ref, trash_ref, e1_ref, e2_ref, nblk_ref,
                x_hbm, g_hbm, wg_hbm, wu_hbm, wd_hbm, lng_ref, lnb_ref, out_hbm,
                xbuf, gbuf, obuf, zbuf, sg, su, sd, wg_bf, wu_bf, wd_bf,
                gsem, ssem, zsem, wsem):
    nblk = nblk_ref[0]
    last = nblk - 1
    ring = lambda j: (j + RING) % RING
    n_res = W_SETS * EXPERTS_PER_GROUP

    def gate_copy(blk, slt):
        return pltpu.make_async_copy(g_hbm.at[pl.ds(off_ref[blk + 1], MOE_ROWS)], gbuf.at[slt],
                                     gsem.at[slt])

    def tile_row(r):
        return r * ROW_TILES if isinstance(r, int) else pl.multiple_of(r * ROW_TILES, ROW_TILES)

    def gather_row(base, r, slt):
        t8 = pl.multiple_of(order_ref[base + r], ROW_TILES)
        pltpu.make_async_copy(x_hbm.at[pl.ds(t8, ROW_TILES)],
                              xbuf.at[slt, pl.ds(tile_row(r), ROW_TILES)], gsem.at[slt]).start()

    def issue_gather(blk, slt, unrolled=True):
        base = off_ref[blk + 1]
        if unrolled:
            for r in range(MOE_ROWS):
                gather_row(base, r, slt)
        else:
            lax.fori_loop(0, MOE_ROWS, lambda r, c: (gather_row(base, r, slt), c)[1], 0)
        gate_copy(blk, slt).start()

    def wait_gather(slt):
        pltpu.make_async_copy(x_hbm.at[pl.ds(0, MOE_ROWS * ROW_TILES)], xbuf.at[slt],
                              gsem.at[slt]).wait()
        gate_copy(0, slt).wait()

    def scatter_row(base, nvalid, trash8, r, slt):
        t8 = jnp.where(r < nvalid, order_ref[base + r], trash8 + r * ROW_TILES)
        pltpu.make_async_copy(
            obuf.at[slt, pl.ds(tile_row(r), ROW_TILES)],
            out_hbm.at[pl.ds(pl.multiple_of(t8, ROW_TILES), ROW_TILES)], ssem.at[slt]).start()

    def issue_scatter(blk, slt, unrolled=True):
        args = (off_ref[blk + 1], nvalid_ref[blk + 1], trash_ref[blk + 1])
        if unrolled:
            for r in range(MOE_ROWS):
                scatter_row(*args, r, slt)
        else:
            lax.fori_loop(0, MOE_ROWS, lambda r, c: (scatter_row(*args, r, slt), c)[1], 0)

    def wait_scatter(slt):
        pltpu.make_async_copy(obuf.at[slt], out_hbm.at[pl.ds(0, MOE_ROWS * ROW_TILES)],
                              ssem.at[slt]).wait()

    def weight_copies(k, stage):
        return [pltpu.make_async_copy(src.at[layer, k], dst.at[stage], wsem.at[stage])
                for src, dst in ((wg_hbm, sg), (wu_hbm, su), (wd_hbm, sd))]

    def load_start(k):
        for cp in weight_copies(k, k % W_STAGE):
            cp.start()

    def load_wait(k):
        for cp in weight_copies(k, k % W_STAGE):
            cp.wait()

    def load_finish(k):
        load_wait(k)
        stage, res = k % W_STAGE, k % n_res
        wg_bf[res] = sg[stage].astype(BF16)
        wu_bf[res] = su[stage].astype(BF16)
        wd_bf[res] = sd[stage].astype(BF16)

    def start_if(pred, kd):
        @pl.when(pred)
        def _():
            load_start(kd)
        return kd + pred.astype(I32)

    def service_loader(group, kd, kc):
        cap = jnp.minimum((group + W_SETS) * EXPERTS_PER_GROUP, N_EXPERTS)
        need = (group + 1) * EXPERTS_PER_GROUP

        def catch_up(c):
            kd, kc = c
            kd = start_if(kd <= kc, kd)
            kd = start_if((kd < kc + W_STAGE) & (kd < cap), kd)
            load_finish(kc)
            return kd, kc + 1
        kd, kc = lax.while_loop(lambda c: c[1] < need, catch_up, (kd, kc))

        ready = (kc < kd) & (kc < cap)

        @pl.when(ready)
        def _():
            load_finish(kc)
        kc = kc + ready.astype(I32)
        kd = start_if((kd < kc + W_STAGE) & (kd < cap), kd)
        return kd, kc

    issue_gather(0, 0, unrolled=False)
    issue_gather(jnp.minimum(1, last), 1, unrolled=False)
    obuf[RING - 1] = jnp.zeros(obuf.shape[1:], F32)

    def block(b, loader):
        slot = b % RING
        e1, e2 = e1_ref[b], e2_ref[b]
        loader = service_loader(e1 // EXPERTS_PER_GROUP, *loader)

        @pl.when(b >= RING - 1)
        def _():
            wait_scatter(slot)

        wait_gather(slot)
        x = _load_rows(xbuf.at[slot], MOE_ROWS)
        xb = x.astype(BF16)
        issue_gather(jnp.minimum(b + 2, last), ring(b + 2))
        issue_scatter(b - 1, ring(b - 1))
        res = (e1 % n_res, e2 % n_res)
        hg = [_dot(xb, wg_bf[r]) for r in res]
        hu = [_dot(xb, wu_bf[r]) for r in res]
        act = [(g * jax.nn.sigmoid(g) * u).astype(BF16) for g, u in zip(hg, hu)]
        y = [_dot(a, wd_bf[r]) for a, r in zip(act, res)]
        gates = gbuf[slot]
        m = gates[:, 0:1] * y[0] + gates[:, 1:2] * y[1]
        o = _layer_norm(ALPHA * x + m, lng_ref[...], lnb_ref[...])
        _store_rows(obuf.at[slot], o)
        return loader

    first = (e1_ref[0] // EXPERTS_PER_GROUP) * EXPERTS_PER_GROUP
    kd, kc = lax.fori_loop(0, nblk, block, (first, first))

    lax.fori_loop(kc, kd, lambda k, c: (load_wait(k), c)[1], 0)
    issue_scatter(last, ring(last), unrolled=False)
    for slt in range(RING):
        wait_scatter(slt)
    wait_gather(ring(nblk))
    wait_gather(ring(nblk + 1))

    zbuf[...] = jnp.zeros_like(zbuf)

    def fill_copy(j):
        return pltpu.make_async_copy(
            zbuf, out_hbm.at[pl.ds(j * MOE_ROWS * ROW_TILES, MOE_ROWS * ROW_TILES)], zsem)
    lax.fori_loop(nblk, n_blocks, lambda j, c: (fill_copy(j).start(), c)[1], 0)
    lax.fori_loop(nblk, n_blocks, lambda j, c: (fill_copy(j).wait(), c)[1], 0)


def _moe_half(layer, x1, plan, gates_sorted, w_gate, w_up, w_down, ln_g, ln_b):
    n_blocks = plan[4].shape[0]
    n_rows = (n_blocks + 1) * MOE_ROWS
    d, f = D_MODEL, D_FF
    n_res = W_SETS * EXPERTS_PER_GROUP
    const_spec = pl.BlockSpec((1, d), lambda i, *_: (0, 0))
    any_spec = pl.BlockSpec(memory_space=pl.ANY)
    grid_spec = pltpu.PrefetchScalarGridSpec(
        num_scalar_prefetch=7,
        grid=(1,),
        in_specs=[any_spec, any_spec, any_spec, any_spec, any_spec, const_spec, const_spec],
        out_specs=any_spec,
        scratch_shapes=[
            pltpu.VMEM((RING, MOE_ROWS * ROW_TILES, LANES), F32),
            pltpu.VMEM((RING, MOE_ROWS, LANES), F32),
            pltpu.VMEM((RING, MOE_ROWS * ROW_TILES, LANES), F32),
            pltpu.VMEM((MOE_ROWS * ROW_TILES, LANES), F32),
            pltpu.VMEM((W_STAGE, d, f), F32), pltpu.VMEM((W_STAGE, d, f), F32),
            pltpu.VMEM((W_STAGE, f, d), F32),
            pltpu.VMEM((n_res, d, f), BF16), pltpu.VMEM((n_res, d, f), BF16),
            pltpu.VMEM((n_res, f, d), BF16),
            pltpu.SemaphoreType.DMA((RING,)),
            pltpu.SemaphoreType.DMA((RING,)),
            pltpu.SemaphoreType.DMA(()),
            pltpu.SemaphoreType.DMA((W_STAGE,)),
        ],
    )
    return pl.pallas_call(
        functools.partial(_moe_kernel, layer, n_blocks),
        grid_spec=grid_spec,
        out_shape=jax.ShapeDtypeStruct((n_rows * ROW_TILES, LANES), F32),
        compiler_params=pltpu.CompilerParams(
            dimension_semantics=("arbitrary",), vmem_limit_bytes=VMEM_LIMIT),
        name="moe_half",
    )(*plan, x1, gates_sorted, w_gate, w_up, w_down, ln_g.reshape(1, d), ln_b.reshape(1, d))


def _plan_blocks(route_f, route_i, counts_f, n_tok):
    cls, rank = route_i[0], route_i[1]
    counts = counts_f[:N_CLASSES, 0].astype(I32)
    padded = (counts + MOE_ROWS - 1) // MOE_ROWS * MOE_ROWS
    pad_end = jnp.cumsum(padded)
    pad_start = pad_end - padded
    start = jnp.cumsum(counts) - counts
    class_ids = jnp.arange(N_CLASSES, dtype=I32)
    dest = jnp.sum(jnp.where(cls[:, None] == class_ids[None], pad_start[None], 0), axis=1) + rank
    _, order, g_lo, g_hi = lax.sort(
        (dest, jnp.arange(n_tok, dtype=I32), route_f[0], route_f[1]), num_keys=1)
    order = jnp.pad(order, (0, MOE_ROWS)) * ROW_TILES
    gates_sorted = jnp.pad(jnp.stack([g_lo, g_hi], axis=1), ((0, MOE_ROWS), (0, LANES - 2)))

    n_blocks = n_tok // MOE_ROWS + N_CLASSES
    blk_ids = jnp.arange(n_blocks, dtype=I32)
    blk_start = blk_ids * MOE_ROWS
    nblk = pad_end[-1] // MOE_ROWS
    blk_cls = jnp.minimum(jnp.sum(pad_end[None] <= blk_start[:, None], axis=1), N_CLASSES - 1)
    blk_cls = jnp.where(blk_ids < nblk, blk_cls, blk_cls[jnp.maximum(nblk - 1, 0)]).astype(I32)
    in_cls = blk_start - pad_start[blk_cls]
    active = blk_ids < nblk
    blk_off = jnp.where(active, start[blk_cls] + in_cls, 0).astype(I32)
    nvalid = jnp.where(active, jnp.clip(counts[blk_cls] - in_cls, 0, MOE_ROWS), 0).astype(I32)
    trash = ((n_tok + blk_start - blk_off - nvalid) * ROW_TILES).astype(I32)
    pair_lo = jnp.array(PAIR_LO, I32)
    pair_hi = jnp.array(PAIR_HI, I32)
    blk_e1 = (blk_cls // N_PAIRS) * EXPERTS_PER_GROUP + pair_lo[blk_cls % N_PAIRS]
    blk_e2 = (blk_cls // N_PAIRS) * EXPERTS_PER_GROUP + pair_hi[blk_cls % N_PAIRS]
    lead = lambda a, v: jnp.concatenate([jnp.full((1,), v, I32), a])
    plan = (order, lead(blk_off, 0), lead(nvalid, 0),
            lead(trash, n_blocks * MOE_ROWS * ROW_TILES),
            blk_e1, blk_e2, nblk.astype(I32).reshape(1))
    return plan, gates_sorted


def kernel(x, mem, conv_w_in, conv_w, mlstm_w_in, mlstm_gate_b, mlstm_norm_g, w_kv_mem, w_out,
           ln1_g, ln1_b, ln2_g, ln2_b, router_w, router_b, w_gate, w_up, w_down):
    nb, seq, d = x.shape
    n_tok = nb * seq
    assert d == D_MODEL and seq % TM_CONV == 0 and seq % TM_MLSTM == 0 and TM_MLSTM % CHUNK == 0
    assert n_tok // MOE_ROWS >= RING

    kbd, vbd = _memory_kv(mem, w_kv_mem)

    perm = lambda a: jnp.swapaxes(a.reshape(N_GROUPS, EXPERTS_PER_GROUP, -1), 0, 1)
    rwt = perm(router_w.T).reshape(N_EXPERTS, d).astype(BF16)
    rb = perm(router_b).reshape(N_EXPERTS, 1)

    xf = x.reshape(n_tok, d)
    for i in range(DEPTH):
        j = i // 2
        wout = w_out[i].astype(BF16)
        if i % 2 == 0:
            win = conv_w_in[j].astype(BF16)
            outs = _mixer_half(_conv_half_kernel, TM_CONV, xf, nb, seq, [win, conv_w[j]],
                               i, kbd, vbd, wout, ln1_g[i], ln1_b[i], rwt, rb,
                               [pltpu.VMEM((8, MIX_W), F32)], "conv_half")
        else:
            w = mlstm_w_in[j]
            o0 = 4 * MIX_W
            w_gates = jnp.pad(w[:, o0:o0 + 2 * ML_HEADS], ((0, 0), (0, GATE_PAD - 2 * ML_HEADS)))
            win = jnp.concatenate([w[:, :3 * MIX_W], w_gates, w[:, 3 * MIX_W:o0],
                                   w[:, o0 + 2 * ML_HEADS:]], axis=1).astype(BF16)
            gate_b = jnp.pad(mlstm_gate_b[j], (0, GATE_PAD - 2 * ML_HEADS)).reshape(1, GATE_PAD)
            norm_g = mlstm_norm_g[j].reshape(1, MIX_W)
            outs = _mixer_half(_mlstm_half_kernel, TM_MLSTM, xf, nb, seq, [win, gate_b, norm_g],
                               i, kbd, vbd, wout, ln1_g[i], ln1_b[i], rwt, rb,
                               [pltpu.VMEM((ML_HEADS, ML_DH, 2 * ML_DH), F32),
                                pltpu.VMEM((1, LANES), F32),
                                pltpu.VMEM((TM_MLSTM, MIX_W), F32)], "mlstm_half")
        x1, route_f, route_i, counts_f = outs
        plan, gates_sorted = _plan_blocks(route_f, route_i, counts_f, n_tok)
        xf = _moe_half(i, x1, plan, gates_sorted, w_gate, w_up, w_down, ln2_g[i], ln2_b[i])
    return xf[:n_tok * ROW_TILES].reshape(nb, seq, d)
```

```python
import functools

import jax
import jax.numpy as jnp
from jax import lax
from jax.experimental import pallas as pl
from jax.experimental.pallas import tpu as pltpu

F32 = jnp.float32
BF16 = jnp.bfloat16
I32 = jnp.int32

D_MODEL = 1024
DEPTH = 4
MIX_W = 768
XATT_HEADS = 4
XATT_DH = 64
XATT_W = XATT_HEADS * XATT_DH
CONV_K = 3
ML_HEADS = 6
ML_DH = MIX_W // ML_HEADS
CHUNK = 128
N_EXPERTS = 32
N_GROUPS = 8
EXPERTS_PER_GROUP = N_EXPERTS // N_GROUPS
D_FF = 512
ALPHA = (2.0 * DEPTH) ** 0.25
LN_EPS = 1e-5

LANES = 128
SUBLANES = 8
ROW_TILES = D_MODEL // LANES
GATE_PAD = LANES
MOE_ROWS = 128
RING = 3
W_SETS = 2
W_STAGE = 2
PAIR_LO = (0, 0, 1, 1, 2, 0)
PAIR_HI = (1, 2, 2, 3, 3, 3)
N_PAIRS = len(PAIR_LO)
N_CLASSES = N_GROUPS * N_PAIRS
CLASS_PAD = 64
VMEM_LIMIT = 52 * 1024 * 1024

TM_CONV = 512
TM_MLSTM = 256


def _layer_norm(z, g, b):
    mu = jnp.mean(z, axis=-1, keepdims=True)
    zc = z - mu
    var = jnp.mean(zc * zc, axis=-1, keepdims=True)
    return zc * lax.rsqrt(var + LN_EPS) * g + b


def _load_rows(ref, n):
    return jnp.concatenate([ref[pl.ds(j, n, stride=ROW_TILES), :] for j in range(ROW_TILES)],
                           axis=-1)


def _load_x(x_ref):
    if x_ref.shape[1] == D_MODEL:
        return x_ref[...]
    return _load_rows(x_ref, x_ref.shape[0] // ROW_TILES)


def _store_rows(ref, val):
    for j in range(ROW_TILES):
        ref[pl.ds(j, val.shape[0], stride=ROW_TILES), :] = val[:, j * LANES:(j + 1) * LANES]


def _dot(a, b):
    return jnp.dot(a, b, preferred_element_type=F32)


def _dot_nt(a, b):
    return lax.dot_general(a, b, (((1,), (1,)), ((), ())), preferred_element_type=F32)


def _kv_kernel(mem_ref, w_ref, kbd_ref, vbd_ref):
    kv = _dot(mem_ref[0].astype(BF16), w_ref[0].astype(BF16))
    mem_len = kv.shape[0]
    kt = (kv[:, :XATT_W] * (XATT_DH ** -0.5)).T
    v = kv[:, XATT_W:]
    row_head = lax.broadcasted_iota(I32, kt.shape, 0) // XATT_DH
    col_head = lax.broadcasted_iota(I32, v.shape, 1) // XATT_DH
    for h in range(XATT_HEADS):
        kbd_ref[0, 0, :, h * mem_len:(h + 1) * mem_len] = (
            jnp.where(row_head == h, kt, 0.0).astype(BF16))
        vbd_ref[0, 0, h * mem_len:(h + 1) * mem_len, :] = (
            jnp.where(col_head == h, v, 0.0).astype(BF16))


def _memory_kv(mem, w_kv_mem):
    nb, mem_len, d = mem.shape
    depth = w_kv_mem.shape[0]
    return pl.pallas_call(
        _kv_kernel,
        grid=(depth, nb),
        in_specs=[pl.BlockSpec((1, mem_len, d), lambda i, b: (b, 0, 0)),
                  pl.BlockSpec((1, d, 2 * XATT_W), lambda i, b: (i, 0, 0))],
        out_specs=[pl.BlockSpec((1, 1, XATT_W, XATT_HEADS * mem_len), lambda i, b: (i, b, 0, 0)),
                   pl.BlockSpec((1, 1, XATT_HEADS * mem_len, XATT_W), lambda i, b: (i, b, 0, 0))],
        out_shape=[jax.ShapeDtypeStruct((depth, nb, XATT_W, XATT_HEADS * mem_len), BF16),
                   jax.ShapeDtypeStruct((depth, nb, XATT_HEADS * mem_len, XATT_W), BF16)],
        compiler_params=pltpu.CompilerParams(
            dimension_semantics=("arbitrary", "arbitrary"), vmem_limit_bytes=VMEM_LIMIT),
        name="memory_kv",
    )(mem, w_kv_mem)


def _route_tile(x1b, rwt_ref, rb_ref, tri_ref, base_ref, rf_ref, ri_ref, cnt_ref):
    tm = x1b.shape[0]
    scores = jax.nn.sigmoid(_dot_nt(rwt_ref[...], x1b))
    sel = scores + rb_ref[...]
    s = [sel[SUBLANES * j:SUBLANES * (j + 1)] for j in range(EXPERTS_PER_GROUP)]
    p = [scores[SUBLANES * j:SUBLANES * (j + 1)] for j in range(EXPERTS_PER_GROUP)]
    hi1, lo1 = jnp.maximum(s[0], s[1]), jnp.minimum(s[0], s[1])
    hi2, lo2 = jnp.maximum(s[2], s[3]), jnp.minimum(s[2], s[3])
    gs = jnp.maximum(hi1, hi2) + jnp.maximum(jnp.minimum(hi1, hi2), jnp.maximum(lo1, lo2))
    sub = lax.broadcasted_iota(I32, (N_GROUPS, tm), 0)
    gi = jnp.min(jnp.where(gs == jnp.max(gs, axis=0, keepdims=True), sub, N_GROUPS),
                 axis=0, keepdims=True)
    gm = sub == gi
    v = [jnp.sum(jnp.where(gm, sj, 0.0), axis=0, keepdims=True) for sj in s]
    pv = [jnp.sum(jnp.where(gm, pj, 0.0), axis=0, keepdims=True) for pj in p]
    best, i1 = v[0], jnp.zeros((1, tm), I32)
    for j in range(1, EXPERTS_PER_GROUP):
        u = v[j] > best
        best, i1 = jnp.where(u, v[j], best), jnp.where(u, j, i1)
    best2, i2 = jnp.full((1, tm), -jnp.inf, F32), jnp.zeros((1, tm), I32)
    for j in range(EXPERTS_PER_GROUP):
        cand = jnp.where(i1 == j, -jnp.inf, v[j])
        u = cand > best2
        best2, i2 = jnp.where(u, cand, best2), jnp.where(u, j, i2)

    def pick(idx):
        return jnp.where(idx == 0, pv[0], jnp.where(idx == 1, pv[1],
                                                    jnp.where(idx == 2, pv[2], pv[3])))

    p1, p2 = pick(i1), pick(i2)
    tot = p1 + p2
    first_lo = i1 < i2
    g_lo = jnp.where(first_lo, p1, p2) / tot
    g_hi = jnp.where(first_lo, p2, p1) / tot
    lo, hi = jnp.minimum(i1, i2), jnp.maximum(i1, i2)
    pair = jnp.where(hi == 1, 0, jnp.where(hi == 2, jnp.where(lo == 0, 1, 2),
                                           jnp.where(lo == 1, 3, jnp.where(lo == 2, 4, 5))))
    cls = gi * N_PAIRS + pair

    oh = lax.broadcasted_iota(I32, (CLASS_PAD, tm), 0) == cls
    pref = _dot(oh.astype(F32).astype(BF16), tri_ref[...])
    base = base_ref[...]
    rank = jnp.sum(jnp.where(oh, pref + base[:, 0:1] - 1.0, 0.0), axis=0, keepdims=True)
    base = base + pref[:, tm - 1:tm]
    base_ref[...] = base
    cnt_ref[...] = base

    sub8 = lax.broadcasted_iota(I32, (SUBLANES, tm), 0)
    rf_ref[...] = jnp.where(sub8 == 0, g_lo, jnp.where(sub8 == 1, g_hi, 0.0))
    ri_ref[...] = jnp.where(sub8 == 0, cls, jnp.where(sub8 == 1, rank.astype(I32), 0))


def _init_tail(tail_refs):
    base_ref = tail_refs[-1]

    @pl.when((pl.program_id(0) == 0) & (pl.program_id(1) == 0))
    def _():
        base_ref[...] = jnp.zeros_like(base_ref)


def _cross_attention(q_mem, tail_refs):
    kbd_ref, vbd_ref = tail_refs[:2]
    s = _dot(q_mem.astype(BF16), kbd_ref[0, 0])
    mem_len = s.shape[1] // XATT_HEADS
    sh = [s[:, h * mem_len:(h + 1) * mem_len] for h in range(XATT_HEADS)]
    mx = [jnp.max(t, axis=-1, keepdims=True) for t in sh]
    e = [jnp.exp(t - m) for t, m in zip(sh, mx)]
    tot = [jnp.sum(t, axis=-1, keepdims=True) for t in e]
    ps = [(t / n).astype(BF16) for t, n in zip(e, tot)]
    return _dot(jnp.concatenate(ps, axis=-1), vbd_ref[0, 0])


def _mixer_tail(x, h_loc, h_mem, tail_refs):
    (_, _, wout_ref, g_ref, b_ref, rwt_ref, rb_ref, tri_ref,
     x1_ref, rf_ref, ri_ref, cnt_ref, base_ref) = tail_refs
    hcat = jnp.concatenate([h_loc.astype(BF16), h_mem.astype(BF16)], axis=-1)
    y = _dot(hcat, wout_ref[...])
    x1 = _layer_norm(ALPHA * x + y, g_ref[...], b_ref[...])
    _store_rows(x1_ref, x1)
    _route_tile(x1.astype(BF16), rwt_ref, rb_ref, tri_ref, base_ref, rf_ref, ri_ref, cnt_ref)


def _conv_half_kernel(x_ref, win_ref, cw_ref, *refs):
    tail_refs, carry_ref = refs[:-1], refs[-1]
    _init_tail(tail_refs)

    @pl.when(pl.program_id(1) == 0)
    def _():
        carry_ref[...] = jnp.zeros_like(carry_ref)

    x = _load_x(x_ref)
    tm = x.shape[0]
    xb = x.astype(BF16)
    q_mem = _dot(xb, win_ref[:, 3 * MIX_W:])
    h_mem = _cross_attention(q_mem, tail_refs)
    proj = _dot(xb, win_ref[:, :3 * MIX_W])
    bg = proj[:, 0:MIX_W]
    cg = proj[:, MIX_W:2 * MIX_W]
    xin = proj[:, 2 * MIX_W:3 * MIX_W]
    u = cg * xin
    prev = carry_ref[...]
    row = lax.broadcasted_iota(I32, u.shape, 0)
    u1 = jnp.where(row == 0, prev[7:8], pltpu.roll(u, 1, 0))
    u2 = jnp.where(row == 0, prev[6:7],
                   jnp.where(row == 1, prev[7:8], pltpu.roll(u, 2, 0)))
    carry_ref[...] = u[tm - 8:tm]
    cw = cw_ref[...]
    c = cw[0:1] * u2 + cw[1:2] * u1 + cw[2:3] * u
    _mixer_tail(x, bg * c, h_mem, tail_refs)


def _log_sigmoid(z):
    return -(jnp.maximum(-z, 0.0) + jnp.log1p(jnp.exp(-jnp.abs(z))))


def _mlstm_half_kernel(x_ref, win_ref, gb_ref, ng_ref, *refs):
    tail_refs, (state_ref, m_ref, h_ref) = refs[:-3], refs[-3:]

    @pl.when(pl.program_id(1) == 0)
    def _():
        m_ref[...] = jnp.zeros_like(m_ref)

    x = _load_x(x_ref)
    tm = x.shape[0]
    n_cell = 3 * MIX_W + GATE_PAD
    xb = x.astype(BF16)
    proj = _dot(xb, win_ref[:, :n_cell])
    gates = proj[:, 3 * MIX_W:n_cell] + gb_ref[...]
    row = lax.broadcasted_iota(I32, (CHUNK, GATE_PAD), 0)
    head_lane = lax.broadcasted_iota(I32, (tm, GATE_PAD), 1) < ML_HEADS
    log_i = jnp.where(head_lane, gates, 0.0)
    log_f = jnp.where(head_lane, pltpu.roll(_log_sigmoid(gates), GATE_PAD - ML_HEADS, 1), 0.0)
    causal = (lax.broadcasted_iota(I32, (CHUNK, CHUNK), 0)
              >= lax.broadcasted_iota(I32, (CHUNK, CHUNK), 1))
    ones_col = (lax.broadcasted_iota(I32, (CHUNK, ML_DH), 1) == 0).astype(BF16)

    def scan_rows(a, op, fill):
        sh = 1
        while sh < CHUNK:
            a = op(a, jnp.where(row >= sh, pltpu.roll(a, sh, 0), fill))
            sh *= 2
        return a

    chunks, heads = range(tm // CHUNK), range(ML_HEADS)
    pairs = [(c, h) for c in chunks for h in heads]

    m_prev = m_ref[...]
    gate = []
    for c in chunks:
        li = log_i[c * CHUNK:(c + 1) * CHUNK]
        b = scan_rows(log_f[c * CHUNK:(c + 1) * CHUNK], jnp.add, 0.0)
        wd = li - b
        m_t = b + jnp.maximum(scan_rows(wd, jnp.maximum, -jnp.inf), m_prev)
        g = b[CHUNK - 1:CHUNK, :]
        a = g + wd
        m_new = jnp.maximum(g + m_prev, jnp.max(a, axis=0, keepdims=True))
        gate.append(dict(u=b - m_t, w_inter=jnp.exp(b + m_prev - m_t), clamp=jnp.exp(-m_t),
                         decay=jnp.exp(g + m_prev - m_new), wd_t=wd.T,
                         wn_t=jnp.exp(a - m_new).T))
        m_prev = m_new
    m_ref[...] = m_prev

    col = lambda c, h, o=0: proj[c * CHUNK:(c + 1) * CHUNK, o + h * ML_DH:o + (h + 1) * ML_DH]
    qb = {p: col(*p).astype(BF16) for p in pairs}
    kf = {p: col(*p, MIX_W) * (ML_DH ** -0.5) for p in pairs}
    kb = {p: kf[p].astype(BF16) for p in pairs}
    vb = {p: col(*p, 2 * MIX_W).astype(BF16) for p in pairs}
    qk = {p: _dot_nt(qb[p], kb[p]) for p in pairs}
    proj_tail = _dot(xb, win_ref[:, n_cell:])
    s = {(c, h): qk[c, h] * jnp.exp(jnp.where(
        causal, gate[c]["u"][:, h:h + 1] + gate[c]["wd_t"][h:h + 1, :], -jnp.inf))
        for c, h in pairs}
    sv = {p: _dot(s[p].astype(BF16), vb[p]) for p in pairs}
    s_sum = {p: jnp.sum(s[p], axis=-1, keepdims=True) for p in pairs}
    ktw = {(c, h): (kf[c, h].T * gate[c]["wn_t"][h:h + 1, :]).astype(BF16) for c, h in pairs}
    kv = {p: _dot(ktw[p], jnp.concatenate([vb[p], ones_col], axis=-1)) for p in pairs}
    @pl.when(pl.program_id(1) == 0)
    def _():
        state_ref[...] = jnp.zeros_like(state_ref)

    state = [state_ref[h] for h in heads]
    qs = {}
    for c in chunks:
        for h in heads:
            qs[c, h] = _dot(qb[c, h], state[h].astype(BF16))
        state = [gate[c]["decay"][:, h:h + 1] * state[h] + kv[c, h] for h in heads]
    for h in heads:
        state_ref[h] = state[h]
    wi = {(c, h): gate[c]["w_inter"][:, h:h + 1] for c, h in pairs}
    den = {p: s_sum[p] + wi[p] * qs[p][:, ML_DH:ML_DH + 1] for p in pairs}
    hh = {(c, h): (sv[c, h] + wi[c, h] * qs[c, h][:, 0:ML_DH])
          / jnp.maximum(jnp.abs(den[c, h]), gate[c]["clamp"][:, h:h + 1]) for c, h in pairs}
    hc = {p: hh[p] - jnp.mean(hh[p], axis=-1, keepdims=True) for p in pairs}
    var = {p: jnp.mean(hc[p] * hc[p], axis=-1, keepdims=True) for p in pairs}
    for c, h in pairs:
        h_ref[c * CHUNK:(c + 1) * CHUNK, h * ML_DH:(h + 1) * ML_DH] = (
            hc[c, h] * lax.rsqrt(var[c, h] + LN_EPS))

    h_loc = jax.nn.sigmoid(proj_tail[:, 0:MIX_W]) * (h_ref[...] * ng_ref[...])
    _init_tail(tail_refs)
    h_mem = _cross_attention(proj_tail[:, MIX_W:], tail_refs)
    _mixer_tail(x, h_loc, h_mem, tail_refs)


def _mixer_half(kernel_fn, tm, x, nb, seq, mixer_args, layer, kbd, vbd, wout, ln_g, ln_b, rwt, rb,
                scratch, name):
    d = D_MODEL
    n_tok = nb * seq
    spb = seq // tm
    const2 = lambda b, s: (0, 0)
    tri = jnp.triu(jnp.ones((tm, tm), BF16))
    tail_args = [kbd, vbd, wout, ln_g.reshape(1, d), ln_b.reshape(1, d), rwt, rb, tri]
    row_map = lambda b, s: (b * spb + s, 0)
    x_spec = (pl.BlockSpec((tm, d), row_map) if x.shape[1] == d
              else pl.BlockSpec((tm * ROW_TILES, LANES), row_map))
    in_specs = ([x_spec]
                + [pl.BlockSpec(a.shape, const2) for a in mixer_args]
                + [pl.BlockSpec((1, 1) + kbd.shape[2:], lambda b, s: (layer, b, 0, 0)),
                   pl.BlockSpec((1, 1) + vbd.shape[2:], lambda b, s: (layer, b, 0, 0))]
                + [pl.BlockSpec(a.shape, const2) for a in tail_args[2:]])
    tok_spec = lambda rows: pl.BlockSpec((rows, tm), lambda b, s: (0, b * spb + s))
    return pl.pallas_call(
        kernel_fn,
        grid=(nb, spb),
        in_specs=in_specs,
        out_specs=[pl.BlockSpec((tm * ROW_TILES, LANES), row_map),
                   tok_spec(SUBLANES), tok_spec(SUBLANES),
                   pl.BlockSpec((CLASS_PAD, LANES), const2)],
        out_shape=[jax.ShapeDtypeStruct((n_tok * ROW_TILES, LANES), F32),
                   jax.ShapeDtypeStruct((SUBLANES, n_tok), F32),
                   jax.ShapeDtypeStruct((SUBLANES, n_tok), I32),
                   jax.ShapeDtypeStruct((CLASS_PAD, LANES), F32)],
        scratch_shapes=[pltpu.VMEM((CLASS_PAD, LANES), F32)] + scratch,
        compiler_params=pltpu.CompilerParams(
            dimension_semantics=("arbitrary", "arbitrary"), vmem_limit_bytes=VMEM_LIMIT),
        name=name,
    )(x, *mixer_args, *tail_args)


def _moe_kernel(layer, n_blocks,
                order_ref, off_ref, nvalid_ref, trash_ref, e1_ref, e2_ref, nblk_ref,
                x_hbm, g_hbm, wg_hbm, wu_hbm, wd_hbm, lng_ref, lnb_ref, out_hbm,
                xbuf, gbuf, obuf, zbuf, sg, su, sd, wg_bf, wu_bf, wd_bf,
                gsem, ssem, zsem, wsem):
    nblk = nblk_ref[0]
    last = nblk - 1
    ring = lambda j: (j + RING) % RING
    n_res = W_SETS * EXPERTS_PER_GROUP

    def gate_copy(blk, slt):
        return pltpu.make_async_copy(g_hbm.at[pl.ds(off_ref[blk + 1], MOE_ROWS)], gbuf.at[slt],
                                     gsem.at[slt])

    def tile_row(r):
        return r * ROW_TILES if isinstance(r, int) else pl.multiple_of(r * ROW_TILES, ROW_TILES)

    def gather_row(base, r, slt):
        t8 = pl.multiple_of(order_ref[base + r], ROW_TILES)
        pltpu.make_async_copy(x_hbm.at[pl.ds(t8, ROW_TILES)],
                              xbuf.at[slt, pl.ds(tile_row(r), ROW_TILES)], gsem.at[slt]).start()

    def issue_gather(blk, slt, unrolled=True):
        base = off_ref[blk + 1]
        if unrolled:
            for r in range(MOE_ROWS):
                gather_row(base, r, slt)
        else:
            lax.fori_loop(0, MOE_ROWS, lambda r, c: (gather_row(base, r, slt), c)[1], 0)
        gate_copy(blk, slt).start()

    def wait_gather(slt):
        pltpu.make_async_copy(x_hbm.at[pl.ds(0, MOE_ROWS * ROW_TILES)], xbuf.at[slt],
                              gsem.at[slt]).wait()
        gate_copy(0, slt).wait()

    def scatter_row(base, nvalid, trash8, r, slt):
        t8 = jnp.where(r < nvalid, order_ref[base + r], trash8 + r * ROW_TILES)
        pltpu.make_async_copy(
            obuf.at[slt, pl.ds(tile_row(r), ROW_TILES)],
            out_hbm.at[pl.ds(pl.multiple_of(t8, ROW_TILES), ROW_TILES)], ssem.at[slt]).start()

    def issue_scatter(blk, slt, unrolled=True):
        args = (off_ref[blk + 1], nvalid_ref[blk + 1], trash_ref[blk + 1])
        if unrolled:
            for r in range(MOE_ROWS):
                scatter_row(*args, r, slt)
        else:
            lax.fori_loop(0, MOE_ROWS, lambda r, c: (scatter_row(*args, r, slt), c)[1], 0)

    def wait_scatter(slt):
        pltpu.make_async_copy(obuf.at[slt], out_hbm.at[pl.ds(0, MOE_ROWS * ROW_TILES)],
                              ssem.at[slt]).wait()

    def weight_copies(k, stage):
        return [pltpu.make_async_copy(src.at[layer, k], dst.at[stage], wsem.at[stage])
                for src, dst in ((wg_hbm, sg), (wu_hbm, su), (wd_hbm, sd))]

    def load_start(k):
        for cp in weight_copies(k, k % W_STAGE):
            cp.start()

    def load_wait(k):
        for cp in weight_copies(k, k % W_STAGE):
            cp.wait()

    def load_finish(k):
        load_wait(k)
        stage, res = k % W_STAGE, k % n_res
        wg_bf[res] = sg[stage].astype(BF16)
        wu_bf[res] = su[stage].astype(BF16)
        wd_bf[res] = sd[stage].astype(BF16)

    def start_if(pred, kd):
        @pl.when(pred)
        def _():
            load_start(kd)
        return kd + pred.astype(I32)

    def service_loader(group, kd, kc):
        cap = jnp.minimum((group + W_SETS) * EXPERTS_PER_GROUP, N_EXPERTS)
        need = (group + 1) * EXPERTS_PER_GROUP

        def catch_up(c):
            kd, kc = c
            kd = start_if(kd <= kc, kd)
            kd = start_if((kd < kc + W_STAGE) & (kd < cap), kd)
            load_finish(kc)
            return kd, kc + 1
        kd, kc = lax.while_loop(lambda c: c[1] < need, catch_up, (kd, kc))

        ready = (kc < kd) & (kc < cap)

        @pl.when(ready)
        def _():
            load_finish(kc)
        kc = kc + ready.astype(I32)
        kd = start_if((kd < kc + W_STAGE) & (kd < cap), kd)
        return kd, kc

    issue_gather(0, 0, unrolled=False)
    issue_gather(jnp.minimum(1, last), 1, unrolled=False)
    obuf[RING - 1] = jnp.zeros(obuf.shape[1:], F32)

    def block(b, loader):
        slot = b % RING
        e1, e2 = e1_ref[b], e2_ref[b]
        loader = service_loader(e1 // EXPERTS_PER_GROUP, *loader)

        @pl.when(b >= RING - 1)
        def _():
            wait_scatter(slot)

        wait_gather(slot)
        x = _load_rows(xbuf.at[slot], MOE_ROWS)
        xb = x.astype(BF16)
        issue_gather(jnp.minimum(b + 2, last), ring(b + 2))
        issue_scatter(b - 1, ring(b - 1))
        res = (e1 % n_res, e2 % n_res)
        hg = [_dot(xb, wg_bf[r]) for r in res]
        hu = [_dot(xb, wu_bf[r]) for r in res]
        act = [(g * jax.nn.sigmoid(g) * u).astype(BF16) for g, u in zip(hg, hu)]
        y = [_dot(a, wd_bf[r]) for a, r in zip(act, res)]
        gates = gbuf[slot]
        m = gates[:, 0:1] * y[0] + gates[:, 1:2] * y[1]
        o = _layer_norm(ALPHA * x + m, lng_ref[...], lnb_ref[...])
        _store_rows(obuf.at[slot], o)
        return loader

    first = (e1_ref[0] // EXPERTS_PER_GROUP) * EXPERTS_PER_GROUP
    kd, kc = lax.fori_loop(0, nblk, block, (first, first))

    lax.fori_loop(kc, kd, lambda k, c: (load_wait(k), c)[1], 0)
    issue_scatter(last, ring(last), unrolled=False)
    for slt in range(RING):
        wait_scatter(slt)
    wait_gather(ring(nblk))
    wait_gather(ring(nblk + 1))

    zbuf[...] = jnp.zeros_like(zbuf)

    def fill_copy(j):
        return pltpu.make_async_copy(
            zbuf, out_hbm.at[pl.ds(j * MOE_ROWS * ROW_TILES, MOE_ROWS * ROW_TILES)], zsem)
    lax.fori_loop(nblk, n_blocks, lambda j, c: (fill_copy(j).start(), c)[1], 0)
    lax.fori_loop(nblk, n_blocks, lambda j, c: (fill_copy(j).wait(), c)[1], 0)


def _moe_half(layer, x1, plan, gates_sorted, w_gate, w_up, w_down, ln_g, ln_b):
    n_blocks = plan[4].shape[0]
    n_rows = (n_blocks + 1) * MOE_ROWS
    d, f = D_MODEL, D_FF
    n_res = W_SETS * EXPERTS_PER_GROUP
    const_spec = pl.BlockSpec((1, d), lambda i, *_: (0, 0))
    any_spec = pl.BlockSpec(memory_space=pl.ANY)
    grid_spec = pltpu.PrefetchScalarGridSpec(
        num_scalar_prefetch=7,
        grid=(1,),
        in_specs=[any_spec, any_spec, any_spec, any_spec, any_spec, const_spec, const_spec],
        out_specs=any_spec,
        scratch_shapes=[
            pltpu.VMEM((RING, MOE_ROWS * ROW_TILES, LANES), F32),
            pltpu.VMEM((RING, MOE_ROWS, LANES), F32),
            pltpu.VMEM((RING, MOE_ROWS * ROW_TILES, LANES), F32),
            pltpu.VMEM((MOE_ROWS * ROW_TILES, LANES), F32),
            pltpu.VMEM((W_STAGE, d, f), F32), pltpu.VMEM((W_STAGE, d, f), F32),
            pltpu.VMEM((W_STAGE, f, d), F32),
            pltpu.VMEM((n_res, d, f), BF16), pltpu.VMEM((n_res, d, f), BF16),
            pltpu.VMEM((n_res, f, d), BF16),
            pltpu.SemaphoreType.DMA((RING,)),
            pltpu.SemaphoreType.DMA((RING,)),
            pltpu.SemaphoreType.DMA(()),
            pltpu.SemaphoreType.DMA((W_STAGE,)),
        ],
    )
    return pl.pallas_call(
        functools.partial(_moe_kernel, layer, n_blocks),
        grid_spec=grid_spec,
        out_shape=jax.ShapeDtypeStruct((n_rows * ROW_TILES, LANES), F32),
        compiler_params=pltpu.CompilerParams(
            dimension_semantics=("arbitrary",), vmem_limit_bytes=VMEM_LIMIT),
        name="moe_half",
    )(*plan, x1, gates_sorted, w_gate, w_up, w_down, ln_g.reshape(1, d), ln_b.reshape(1, d))


def _plan_blocks(route_f, route_i, counts_f, n_tok):
    cls, rank = route_i[0], route_i[1]
    counts = counts_f[:N_CLASSES, 0].astype(I32)
    padded = (counts + MOE_ROWS - 1) // MOE_ROWS * MOE_ROWS
    pad_end = jnp.cumsum(padded)
    pad_start = pad_end - padded
    start = jnp.cumsum(counts) - counts
    class_ids = jnp.arange(N_CLASSES, dtype=I32)
    dest = jnp.sum(jnp.where(cls[:, None] == class_ids[None], pad_start[None], 0), axis=1) + rank
    _, order, g_lo, g_hi = lax.sort(
        (dest, jnp.arange(n_tok, dtype=I32), route_f[0], route_f[1]), num_keys=1)
    order = jnp.pad(order, (0, MOE_ROWS)) * ROW_TILES
    gates_sorted = jnp.pad(jnp.stack([g_lo, g_hi], axis=1), ((0, MOE_ROWS), (0, LANES - 2)))

    n_blocks = n_tok // MOE_ROWS + N_CLASSES
    blk_ids = jnp.arange(n_blocks, dtype=I32)
    blk_start = blk_ids * MOE_ROWS
    nblk = pad_end[-1] // MOE_ROWS
    blk_cls = jnp.minimum(jnp.sum(pad_end[None] <= blk_start[:, None], axis=1), N_CLASSES - 1)
    blk_cls = jnp.where(blk_ids < nblk, blk_cls, blk_cls[jnp.maximum(nblk - 1, 0)]).astype(I32)
    in_cls = blk_start - pad_start[blk_cls]
    active = blk_ids < nblk
    blk_off = jnp.where(active, start[blk_cls] + in_cls, 0).astype(I32)
    nvalid = jnp.where(active, jnp.clip(counts[blk_cls] - in_cls, 0, MOE_ROWS), 0).astype(I32)
    trash = ((n_tok + blk_start - blk_off - nvalid) * ROW_TILES).astype(I32)
    pair_lo = jnp.array(PAIR_LO, I32)
    pair_hi = jnp.array(PAIR_HI, I32)
    blk_e1 = (blk_cls // N_PAIRS) * EXPERTS_PER_GROUP + pair_lo[blk_cls % N_PAIRS]
    blk_e2 = (blk_cls // N_PAIRS) * EXPERTS_PER_GROUP + pair_hi[blk_cls % N_PAIRS]
    lead = lambda a, v: jnp.concatenate([jnp.full((1,), v, I32), a])
    plan = (order, lead(blk_off, 0), lead(nvalid, 0),
            lead(trash, n_blocks * MOE_ROWS * ROW_TILES),
            blk_e1, blk_e2, nblk.astype(I32).reshape(1))
    return plan, gates_sorted


def kernel(x, mem, conv_w_in, conv_w, mlstm_w_in, mlstm_gate_b, mlstm_norm_g, w_kv_mem, w_out,
           ln1_g, ln1_b, ln2_g, ln2_b, router_w, router_b, w_gate, w_up, w_down):
    nb, seq, d = x.shape
    n_tok = nb * seq
    assert d == D_MODEL and seq % TM_CONV == 0 and seq % TM_MLSTM == 0 and TM_MLSTM % CHUNK == 0
    assert n_tok // MOE_ROWS >= RING

    kbd, vbd = _memory_kv(mem, w_kv_mem)

    perm = lambda a: jnp.swapaxes(a.reshape(N_GROUPS, EXPERTS_PER_GROUP, -1), 0, 1)
    rwt = perm(router_w.T).reshape(N_EXPERTS, d).astype(BF16)
    rb = perm(router_b).reshape(N_EXPERTS, 1)

    xf = x.reshape(n_tok, d)
    for i in range(DEPTH):
        j = i // 2
        wout = w_out[i].astype(BF16)
        if i % 2 == 0:
            win = conv_w_in[j].astype(BF16)
            outs = _mixer_half(_conv_half_kernel, TM_CONV, xf, nb, seq, [win, conv_w[j]],
                               i, kbd, vbd, wout, ln1_g[i], ln1_b[i], rwt, rb,
                               [pltpu.VMEM((8, MIX_W), F32)], "conv_half")
        else:
            w = mlstm_w_in[j]
            o0 = 4 * MIX_W
            w_gates = jnp.pad(w[:, o0:o0 + 2 * ML_HEADS], ((0, 0), (0, GATE_PAD - 2 * ML_HEADS)))
            win = jnp.concatenate([w[:, :3 * MIX_W], w_gates, w[:, 3 * MIX_W:o0],
                                   w[:, o0 + 2 * ML_HEADS:]], axis=1).astype(BF16)
            gate_b = jnp.pad(mlstm_gate_b[j], (0, GATE_PAD - 2 * ML_HEADS)).reshape(1, GATE_PAD)
            norm_g = mlstm_norm_g[j].reshape(1, MIX_W)
            outs = _mixer_half(_mlstm_half_kernel, TM_MLSTM, xf, nb, seq, [win, gate_b, norm_g],
                               i, kbd, vbd, wout, ln1_g[i], ln1_b[i], rwt, rb,
                               [pltpu.VMEM((ML_HEADS, ML_DH, 2 * ML_DH), F32),
                                pltpu.VMEM((1, LANES), F32),
                                pltpu.VMEM((TM_MLSTM, MIX_W), F32)], "mlstm_half")
        x1, route_f, route_i, counts_f = outs
        plan, gates_sorted = _plan_blocks(route_f, route_i, counts_f, n_tok)
        xf = _moe_half(i, x1, plan, gates_sorted, w_gate, w_up, w_down, ln2_g[i], ln2_b[i])
    return xf[:n_tok * ROW_TILES].reshape(nb, seq, d)
```

```python
import functools

import jax
import jax.numpy as jnp
from jax import lax
from jax.experimental import pallas as pl
from jax.experimental.pallas import tpu as pltpu

F32 = jnp.float32
BF16 = jnp.bfloat16
I32 = jnp.int32

D_MODEL = 1024
DEPTH = 4
MIX_W = 768
XATT_HEADS = 4
XATT_DH = 64
XATT_W = XATT_HEADS * XATT_DH
CONV_K = 3
ML_HEADS = 6
ML_DH = MIX_W // ML_HEADS
CHUNK = 128
N_EXPERTS = 32
N_GROUPS = 8
EXPERTS_PER_GROUP = N_EXPERTS // N_GROUPS
D_FF = 512
ALPHA = (2.0 * DEPTH) ** 0.25
LN_EPS = 1e-5

LANES = 128
SUBLANES = 8
ROW_TILES = D_MODEL // LANES
GATE_PAD = LANES
MOE_ROWS = 128
RING = 3
W_SETS = 2
W_STAGE = 2
PAIR_LO = (0, 0, 1, 1, 2, 0)
PAIR_HI = (1, 2, 2, 3, 3, 3)
N_PAIRS = len(PAIR_LO)
N_CLASSES = N_GROUPS * N_PAIRS
CLASS_PAD = 64
VMEM_LIMIT = 52 * 1024 * 1024

TM_CONV = 512
TM_MLSTM = 256
TM_UNTILE = 1024


def _layer_norm(z, g, b):
    mu = jnp.mean(z, axis=-1, keepdims=True)
    zc = z - mu
    var = jnp.mean(zc * zc, axis=-1, keepdims=True)
    return zc * lax.rsqrt(var + LN_EPS) * g + b


def _load_rows(ref, n):
    return jnp.concatenate([ref[pl.ds(j, n, stride=ROW_TILES), :] for j in range(ROW_TILES)],
                           axis=-1)


def _load_x(x_ref):
    if x_ref.shape[1] == D_MODEL:
        return x_ref[...]
    return _load_rows(x_ref, x_ref.shape[0] // ROW_TILES)


def _store_rows(ref, val):
    for j in range(ROW_TILES):
        ref[pl.ds(j, val.shape[0], stride=ROW_TILES), :] = val[:, j * LANES:(j + 1) * LANES]


def _dot(a, b):
    return jnp.dot(a, b, preferred_element_type=F32)


def _dot_nt(a, b):
    return lax.dot_general(a, b, (((1,), (1,)), ((), ())), preferred_element_type=F32)


def _kv_kernel(mem_ref, w_ref, kbd_ref, vbd_ref):
    nb, mem_len = mem_ref.shape[0], mem_ref.shape[1]
    w = w_ref[0].astype(BF16)
    row_head = lax.broadcasted_iota(I32, (XATT_W, mem_len), 0) // XATT_DH
    col_head = lax.broadcasted_iota(I32, (mem_len, XATT_W), 1) // XATT_DH
    for b in range(nb):
        kv = _dot(mem_ref[b].astype(BF16), w)
        kt = (kv[:, :XATT_W] * (XATT_DH ** -0.5)).T
        v = kv[:, XATT_W:]
        for h in range(XATT_HEADS):
            kbd_ref[0, b, :, h * mem_len:(h + 1) * mem_len] = (
                jnp.where(row_head == h, kt, 0.0).astype(BF16))
            vbd_ref[0, b, h * mem_len:(h + 1) * mem_len, :] = (
                jnp.where(col_head == h, v, 0.0).astype(BF16))


def _memory_kv(mem, w_kv_mem):
    nb, mem_len, d = mem.shape
    depth = w_kv_mem.shape[0]
    return pl.pallas_call(
        _kv_kernel,
        grid=(depth,),
        in_specs=[pl.BlockSpec((nb, mem_len, d), lambda i: (0, 0, 0)),
                  pl.BlockSpec((1, d, 2 * XATT_W), lambda i: (i, 0, 0))],
        out_specs=[pl.BlockSpec((1, nb, XATT_W, XATT_HEADS * mem_len), lambda i: (i, 0, 0, 0)),
                   pl.BlockSpec((1, nb, XATT_HEADS * mem_len, XATT_W), lambda i: (i, 0, 0, 0))],
        out_shape=[jax.ShapeDtypeStruct((depth, nb, XATT_W, XATT_HEADS * mem_len), BF16),
                   jax.ShapeDtypeStruct((depth, nb, XATT_HEADS * mem_len, XATT_W), BF16)],
        compiler_params=pltpu.CompilerParams(
            dimension_semantics=("arbitrary",), vmem_limit_bytes=VMEM_LIMIT),
        name="memory_kv",
    )(mem, w_kv_mem)


def _route_tile(x1b, rwt_ref, rb_ref, tri_ref, base_ref, rf_ref, ri_ref, cnt_ref):
    tm = x1b.shape[0]
    scores = jax.nn.sigmoid(_dot_nt(rwt_ref[...], x1b))
    sel = scores + rb_ref[...]
    s = [sel[SUBLANES * j:SUBLANES * (j + 1)] for j in range(EXPERTS_PER_GROUP)]
    p = [scores[SUBLANES * j:SUBLANES * (j + 1)] for j in range(EXPERTS_PER_GROUP)]
    hi1, lo1 = jnp.maximum(s[0], s[1]), jnp.minimum(s[0], s[1])
    hi2, lo2 = jnp.maximum(s[2], s[3]), jnp.minimum(s[2], s[3])
    gs = jnp.maximum(hi1, hi2) + jnp.maximum(jnp.minimum(hi1, hi2), jnp.maximum(lo1, lo2))
    sub = lax.broadcasted_iota(I32, (N_GROUPS, tm), 0)
    gi = jnp.min(jnp.where(gs == jnp.max(gs, axis=0, keepdims=True), sub, N_GROUPS),
                 axis=0, keepdims=True)
    gm = sub == gi
    v = [jnp.sum(jnp.where(gm, sj, 0.0), axis=0, keepdims=True) for sj in s]
    pv = [jnp.sum(jnp.where(gm, pj, 0.0), axis=0, keepdims=True) for pj in p]
    best, i1 = v[0], jnp.zeros((1, tm), I32)
    for j in range(1, EXPERTS_PER_GROUP):
        u = v[j] > best
        best, i1 = jnp.where(u, v[j], best), jnp.where(u, j, i1)
    best2, i2 = jnp.full((1, tm), -jnp.inf, F32), jnp.zeros((1, tm), I32)
    for j in range(EXPERTS_PER_GROUP):
        cand = jnp.where(i1 == j, -jnp.inf, v[j])
        u = cand > best2
        best2, i2 = jnp.where(u, cand, best2), jnp.where(u, j, i2)

    def pick(idx):
        return jnp.where(idx == 0, pv[0], jnp.where(idx == 1, pv[1],
                                                    jnp.where(idx == 2, pv[2], pv[3])))

    p1, p2 = pick(i1), pick(i2)
    tot = p1 + p2
    first_lo = i1 < i2
    g_lo = jnp.where(first_lo, p1, p2) / tot
    g_hi = jnp.where(first_lo, p2, p1) / tot
    lo, hi = jnp.minimum(i1, i2), jnp.maximum(i1, i2)
    pair = jnp.where(hi == 1, 0, jnp.where(hi == 2, jnp.where(lo == 0, 1, 2),
                                           jnp.where(lo == 1, 3, jnp.where(lo == 2, 4, 5))))
    cls = gi * N_PAIRS + pair

    oh = lax.broadcasted_iota(I32, (CLASS_PAD, tm), 0) == cls
    pref = _dot(oh.astype(F32).astype(BF16), tri_ref[...])
    base = base_ref[...]
    rank = jnp.sum(jnp.where(oh, pref + base[:, 0:1] - 1.0, 0.0), axis=0, keepdims=True)
    base = base + pref[:, tm - 1:tm]
    base_ref[...] = base
    cnt_ref[...] = base

    sub8 = lax.broadcasted_iota(I32, (SUBLANES, tm), 0)
    rf_ref[...] = jnp.where(sub8 == 0, g_lo, jnp.where(sub8 == 1, g_hi, 0.0))
    ri_ref[...] = jnp.where(sub8 == 0, cls, jnp.where(sub8 == 1, rank.astype(I32), 0))


def _init_tail(tail_refs):
    base_ref = tail_refs[-1]

    @pl.when((pl.program_id(0) == 0) & (pl.program_id(1) == 0))
    def _():
        base_ref[...] = jnp.zeros_like(base_ref)


def _cross_attention(q_mem, tail_refs):
    kbd_ref, vbd_ref = tail_refs[:2]
    s = _dot(q_mem.astype(BF16), kbd_ref[0, 0])
    mem_len = s.shape[1] // XATT_HEADS
    sh = [s[:, h * mem_len:(h + 1) * mem_len] for h in range(XATT_HEADS)]
    mx = [jnp.max(t, axis=-1, keepdims=True) for t in sh]
    e = [jnp.exp(t - m) for t, m in zip(sh, mx)]
    tot = [jnp.sum(t, axis=-1, keepdims=True) for t in e]
    ps = [(t / n).astype(BF16) for t, n in zip(e, tot)]
    return _dot(jnp.concatenate(ps, axis=-1), vbd_ref[0, 0])


def _mixer_tail(x, h_loc, h_mem, tail_refs):
    (_, _, wout_ref, g_ref, b_ref, rwt_ref, rb_ref, tri_ref,
     x1_ref, rf_ref, ri_ref, cnt_ref, base_ref) = tail_refs
    hcat = jnp.concatenate([h_loc.astype(BF16), h_mem.astype(BF16)], axis=-1)
    y = _dot(hcat, wout_ref[...])
    x1 = _layer_norm(ALPHA * x + y, g_ref[...], b_ref[...])
    _store_rows(x1_ref, x1)
    _route_tile(x1.astype(BF16), rwt_ref, rb_ref, tri_ref, base_ref, rf_ref, ri_ref, cnt_ref)


def _conv_half_kernel(x_ref, win_ref, cw_ref, *refs):
    tail_refs, carry_ref = refs[:-1], refs[-1]
    _init_tail(tail_refs)

    @pl.when(pl.program_id(1) == 0)
    def _():
        carry_ref[...] = jnp.zeros_like(carry_ref)

    x = _load_x(x_ref)
    tm = x.shape[0]
    xb = x.astype(BF16)
    q_mem = _dot(xb, win_ref[:, 3 * MIX_W:])
    h_mem = _cross_attention(q_mem, tail_refs)
    proj = _dot(xb, win_ref[:, :3 * MIX_W])
    bg = proj[:, 0:MIX_W]
    cg = proj[:, MIX_W:2 * MIX_W]
    xin = proj[:, 2 * MIX_W:3 * MIX_W]
    u = cg * xin
    prev = carry_ref[...]
    row = lax.broadcasted_iota(I32, u.shape, 0)
    u1 = jnp.where(row == 0, prev[7:8], pltpu.roll(u, 1, 0))
    u2 = jnp.where(row == 0, prev[6:7],
                   jnp.where(row == 1, prev[7:8], pltpu.roll(u, 2, 0)))
    carry_ref[...] = u[tm - 8:tm]
    cw = cw_ref[...]
    c = cw[0:1] * u2 + cw[1:2] * u1 + cw[2:3] * u
    _mixer_tail(x, bg * c, h_mem, tail_refs)


def _log_sigmoid(z):
    return -(jnp.maximum(-z, 0.0) + jnp.log1p(jnp.exp(-jnp.abs(z))))


def _mlstm_half_kernel(x_ref, win_ref, gb_ref, ng_ref, *refs):
    tail_refs, (state_ref, m_ref, h_ref) = refs[:-3], refs[-3:]

    @pl.when(pl.program_id(1) == 0)
    def _():
        m_ref[...] = jnp.zeros_like(m_ref)

    x = _load_x(x_ref)
    tm = x.shape[0]
    n_cell = 3 * MIX_W + GATE_PAD
    xb = x.astype(BF16)
    proj = _dot(xb, win_ref[:, :n_cell])
    gates = proj[:, 3 * MIX_W:n_cell] + gb_ref[...]
    row = lax.broadcasted_iota(I32, (CHUNK, GATE_PAD), 0)
    head_lane = lax.broadcasted_iota(I32, (tm, GATE_PAD), 1) < ML_HEADS
    log_i = jnp.where(head_lane, gates, 0.0)
    log_f = jnp.where(head_lane, pltpu.roll(_log_sigmoid(gates), GATE_PAD - ML_HEADS, 1), 0.0)
    causal = (lax.broadcasted_iota(I32, (CHUNK, CHUNK), 0)
              >= lax.broadcasted_iota(I32, (CHUNK, CHUNK), 1))
    ones_col = (lax.broadcasted_iota(I32, (CHUNK, ML_DH), 1) == 0).astype(BF16)

    def scan_rows(a, op, fill):
        sh = 1
        while sh < CHUNK:
            a = op(a, jnp.where(row >= sh, pltpu.roll(a, sh, 0), fill))
            sh *= 2
        return a

    chunks, heads = range(tm // CHUNK), range(ML_HEADS)
    pairs = [(c, h) for c in chunks for h in heads]

    m_prev = m_ref[...]
    gate = []
    for c in chunks:
        li = log_i[c * CHUNK:(c + 1) * CHUNK]
        b = scan_rows(log_f[c * CHUNK:(c + 1) * CHUNK], jnp.add, 0.0)
        wd = li - b
        m_t = b + jnp.maximum(scan_rows(wd, jnp.maximum, -jnp.inf), m_prev)
        g = b[CHUNK - 1:CHUNK, :]
        a = g + wd
        m_new = jnp.maximum(g + m_prev, jnp.max(a, axis=0, keepdims=True))
        gate.append(dict(u=b - m_t, w_inter=jnp.exp(b + m_prev - m_t), clamp=jnp.exp(-m_t),
                         decay=jnp.exp(g + m_prev - m_new), wd_t=wd.T,
                         wn_t=jnp.exp(a - m_new).T))
        m_prev = m_new
    m_ref[...] = m_prev

    col = lambda c, h, o=0: proj[c * CHUNK:(c + 1) * CHUNK, o + h * ML_DH:o + (h + 1) * ML_DH]
    qb = {p: col(*p).astype(BF16) for p in pairs}
    kf = {p: col(*p, MIX_W) * (ML_DH ** -0.5) for p in pairs}
    kb = {p: kf[p].astype(BF16) for p in pairs}
    vb = {p: col(*p, 2 * MIX_W).astype(BF16) for p in pairs}
    qk = {p: _dot_nt(qb[p], kb[p]) for p in pairs}
    proj_tail = _dot(xb, win_ref[:, n_cell:])
    s = {(c, h): qk[c, h] * jnp.exp(jnp.where(
        causal, gate[c]["u"][:, h:h + 1] + gate[c]["wd_t"][h:h + 1, :], -jnp.inf))
        for c, h in pairs}
    sv = {p: _dot(s[p].astype(BF16), vb[p]) for p in pairs}
    s_sum = {p: jnp.sum(s[p], axis=-1, keepdims=True) for p in pairs}
    ktw = {(c, h): (kf[c, h].T * gate[c]["wn_t"][h:h + 1, :]).astype(BF16) for c, h in pairs}
    kv = {p: _dot(ktw[p], jnp.concatenate([vb[p], ones_col], axis=-1)) for p in pairs}
    @pl.when(pl.program_id(1) == 0)
    def _():
        state_ref[...] = jnp.zeros_like(state_ref)

    state = [state_ref[h] for h in heads]
    qs = {}
    for c in chunks:
        for h in heads:
            qs[c, h] = _dot(qb[c, h], state[h].astype(BF16))
        state = [gate[c]["decay"][:, h:h + 1] * state[h] + kv[c, h] for h in heads]
    for h in heads:
        state_ref[h] = state[h]
    wi = {(c, h): gate[c]["w_inter"][:, h:h + 1] for c, h in pairs}
    den = {p: s_sum[p] + wi[p] * qs[p][:, ML_DH:ML_DH + 1] for p in pairs}
    hh = {(c, h): (sv[c, h] + wi[c, h] * qs[c, h][:, 0:ML_DH])
          / jnp.maximum(jnp.abs(den[c, h]), gate[c]["clamp"][:, h:h + 1]) for c, h in pairs}
    hc = {p: hh[p] - jnp.mean(hh[p], axis=-1, keepdims=True) for p in pairs}
    var = {p: jnp.mean(hc[p] * hc[p], axis=-1, keepdims=True) for p in pairs}
    for c, h in pairs:
        h_ref[c * CHUNK:(c + 1) * CHUNK, h * ML_DH:(h + 1) * ML_DH] = (
            hc[c, h] * lax.rsqrt(var[c, h] + LN_EPS))

    h_loc = jax.nn.sigmoid(proj_tail[:, 0:MIX_W]) * (h_ref[...] * ng_ref[...])
    _init_tail(tail_refs)
    h_mem = _cross_attention(proj_tail[:, MIX_W:], tail_refs)
    _mixer_tail(x, h_loc, h_mem, tail_refs)


def _mixer_half(kernel_fn, tm, x, nb, seq, mixer_args, layer, kbd, vbd, wout, ln_g, ln_b, rwt, rb,
                scratch, name):
    d = D_MODEL
    n_tok = nb * seq
    spb = seq // tm
    const2 = lambda b, s: (0, 0)
    tri = jnp.triu(jnp.ones((tm, tm), BF16))
    tail_args = [kbd, vbd, wout, ln_g.reshape(1, d), ln_b.reshape(1, d), rwt, rb, tri]
    row_map = lambda b, s: (b * spb + s, 0)
    x_spec = (pl.BlockSpec((tm, d), row_map) if x.shape[1] == d
              else pl.BlockSpec((tm * ROW_TILES, LANES), row_map))
    in_specs = ([x_spec]
                + [pl.BlockSpec(a.shape, const2) for a in mixer_args]
                + [pl.BlockSpec((1, 1) + kbd.shape[2:], lambda b, s: (layer, b, 0, 0)),
                   pl.BlockSpec((1, 1) + vbd.shape[2:], lambda b, s: (layer, b, 0, 0))]
                + [pl.BlockSpec(a.shape, const2) for a in tail_args[2:]])
    tok_spec = lambda rows: pl.BlockSpec((rows, tm), lambda b, s: (0, b * spb + s))
    return pl.pallas_call(
        kernel_fn,
        grid=(nb, spb),
        in_specs=in_specs,
        out_specs=[pl.BlockSpec((tm * ROW_TILES, LANES), row_map),
                   tok_spec(SUBLANES), tok_spec(SUBLANES),
                   pl.BlockSpec((CLASS_PAD, LANES), const2)],
        out_shape=[jax.ShapeDtypeStruct((n_tok * ROW_TILES, LANES), F32),
                   jax.ShapeDtypeStruct((SUBLANES, n_tok), F32),
                   jax.ShapeDtypeStruct((SUBLANES, n_tok), I32),
                   jax.ShapeDtypeStruct((CLASS_PAD, LANES), F32)],
        scratch_shapes=[pltpu.VMEM((CLASS_PAD, LANES), F32)] + scratch,
        compiler_params=pltpu.CompilerParams(
            dimension_semantics=("arbitrary", "arbitrary"), vmem_limit_bytes=VMEM_LIMIT),
        name=name,
    )(x, *mixer_args, *tail_args)


def _moe_kernel(layer, n_blocks,
                order_ref, off_ref, nvalid_ref, trash_ref, e1_ref, e2_ref, nblk_ref,
                x_hbm, g_hbm, wg_hbm, wu_hbm, wd_hbm, lng_ref, lnb_ref, out_hbm,
                xbuf, gbuf, obuf, zbuf, sg, su, sd, wg_bf, wu_bf, wd_bf,
                gsem, ssem, zsem, wsem):
    nblk = nblk_ref[0]
    last = nblk - 1
    ring = lambda j: (j + RING) % RING
    n_res = W_SETS * EXPERTS_PER_GROUP

    def gate_copy(blk, slt):
        return pltpu.make_async_copy(g_hbm.at[pl.ds(off_ref[blk + 1], MOE_ROWS)], gbuf.at[slt],
                                     gsem.at[slt])

    def tile_row(r):
        return r * ROW_TILES if isinstance(r, int) else pl.multiple_of(r * ROW_TILES, ROW_TILES)

    def gather_row(base, r, slt):
        t8 = pl.multiple_of(order_ref[base + r], ROW_TILES)
        pltpu.make_async_copy(x_hbm.at[pl.ds(t8, ROW_TILES)],
                              xbuf.at[slt, pl.ds(tile_row(r), ROW_TILES)], gsem.at[slt]).start()

    def issue_gather(blk, slt, unrolled=True):
        base = off_ref[blk + 1]
        if unrolled:
            for r in range(MOE_ROWS):
                gather_row(base, r, slt)
        else:
            lax.fori_loop(0, MOE_ROWS, lambda r, c: (gather_row(base, r, slt), c)[1], 0)
        gate_copy(blk, slt).start()

    def wait_gather(slt):
        pltpu.make_async_copy(x_hbm.at[pl.ds(0, MOE_ROWS * ROW_TILES)], xbuf.at[slt],
                              gsem.at[slt]).wait()
        gate_copy(0, slt).wait()

    def scatter_row(base, nvalid, trash8, r, slt):
        t8 = jnp.where(r < nvalid, order_ref[base + r], trash8 + r * ROW_TILES)
        pltpu.make_async_copy(
            obuf.at[slt, pl.ds(tile_row(r), ROW_TILES)],
            out_hbm.at[pl.ds(pl.multiple_of(t8, ROW_TILES), ROW_TILES)], ssem.at[slt]).start()

    def issue_scatter(blk, slt, unrolled=True):
        args = (off_ref[blk + 1], nvalid_ref[blk + 1], trash_ref[blk + 1])
        if unrolled:
            for r in range(MOE_ROWS):
                scatter_row(*args, r, slt)
        else:
            lax.fori_loop(0, MOE_ROWS, lambda r, c: (scatter_row(*args, r, slt), c)[1], 0)

    def wait_scatter(slt):
        pltpu.make_async_copy(obuf.at[slt], out_hbm.at[pl.ds(0, MOE_ROWS * ROW_TILES)],
                              ssem.at[slt]).wait()

    def weight_copies(k, stage):
        return [pltpu.make_async_copy(src.at[layer, k], dst.at[stage], wsem.at[stage])
                for src, dst in ((wg_hbm, sg), (wu_hbm, su), (wd_hbm, sd))]

    def load_start(k):
        for cp in weight_copies(k, k % W_STAGE):
            cp.start()

    def load_wait(k):
        for cp in weight_copies(k, k % W_STAGE):
            cp.wait()

    def load_finish(k):
        load_wait(k)
        stage, res = k % W_STAGE, k % n_res
        wg_bf[res] = sg[stage].astype(BF16)
        wu_bf[res] = su[stage].astype(BF16)
        wd_bf[res] = sd[stage].astype(BF16)

    def start_if(pred, kd):
        @pl.when(pred)
        def _():
            load_start(kd)
        return kd + pred.astype(I32)

    def service_loader(group, kd, kc):
        cap = jnp.minimum((group + W_SETS) * EXPERTS_PER_GROUP, N_EXPERTS)
        need = (group + 1) * EXPERTS_PER_GROUP

        def catch_up(c):
            kd, kc = c
            kd = start_if(kd <= kc, kd)
            kd = start_if((kd < kc + W_STAGE) & (kd < cap), kd)
            load_finish(kc)
            return kd, kc + 1
        kd, kc = lax.while_loop(lambda c: c[1] < need, catch_up, (kd, kc))

        ready = (kc < kd) & (kc < cap)

        @pl.when(ready)
        def _():
            load_finish(kc)
        kc = kc + ready.astype(I32)
        kd = start_if((kd < kc + W_STAGE) & (kd < cap), kd)
        return kd, kc

    issue_gather(0, 0, unrolled=False)
    issue_gather(jnp.minimum(1, last), 1, unrolled=False)
    obuf[RING - 1] = jnp.zeros(obuf.shape[1:], F32)

    def block(b, loader):
        slot = b % RING
        e1, e2 = e1_ref[b], e2_ref[b]
        loader = service_loader(e1 // EXPERTS_PER_GROUP, *loader)

        @pl.when(b >= RING - 1)
        def _():
            wait_scatter(slot)

        wait_gather(slot)
        x = _load_rows(xbuf.at[slot], MOE_ROWS)
        xb = x.astype(BF16)
        issue_gather(jnp.minimum(b + 2, last), ring(b + 2))
        issue_scatter(b - 1, ring(b - 1))
        res = (e1 % n_res, e2 % n_res)
        hg = [_dot(xb, wg_bf[r]) for r in res]
        hu = [_dot(xb, wu_bf[r]) for r in res]
        act = [(g * jax.nn.sigmoid(g) * u).astype(BF16) for g, u in zip(hg, hu)]
        y = [_dot(a, wd_bf[r]) for a, r in zip(act, res)]
        gates = gbuf[slot]
        m = gates[:, 0:1] * y[0] + gates[:, 1:2] * y[1]
        o = _layer_norm(ALPHA * x + m, lng_ref[...], lnb_ref[...])
        _store_rows(obuf.at[slot], o)
        return loader

    first = (e1_ref[0] // EXPERTS_PER_GROUP) * EXPERTS_PER_GROUP
    kd, kc = lax.fori_loop(0, nblk, block, (first, first))

    lax.fori_loop(kc, kd, lambda k, c: (load_wait(k), c)[1], 0)
    issue_scatter(last, ring(last), unrolled=False)
    for slt in range(RING):
        wait_scatter(slt)
    wait_gather(ring(nblk))
    wait_gather(ring(nblk + 1))

    zbuf[...] = jnp.zeros_like(zbuf)

    def fill_copy(j):
        return pltpu.make_async_copy(
            zbuf, out_hbm.at[pl.ds(j * MOE_ROWS * ROW_TILES, MOE_ROWS * ROW_TILES)], zsem)
    lax.fori_loop(nblk, n_blocks, lambda j, c: (fill_copy(j).start(), c)[1], 0)
    lax.fori_loop(nblk, n_blocks, lambda j, c: (fill_copy(j).wait(), c)[1], 0)


def _moe_half(layer, x1, plan, gates_sorted, w_gate, w_up, w_down, ln_g, ln_b):
    n_blocks = plan[4].shape[0]
    n_rows = (n_blocks + 1) * MOE_ROWS
    d, f = D_MODEL, D_FF
    n_res = W_SETS * EXPERTS_PER_GROUP
    const_spec = pl.BlockSpec((1, d), lambda i, *_: (0, 0))
    any_spec = pl.BlockSpec(memory_space=pl.ANY)
    grid_spec = pltpu.PrefetchScalarGridSpec(
        num_scalar_prefetch=7,
        grid=(1,),
        in_specs=[any_spec, any_spec, any_spec, any_spec, any_spec, const_spec, const_spec],
        out_specs=any_spec,
        scratch_shapes=[
            pltpu.VMEM((RING, MOE_ROWS * ROW_TILES, LANES), F32),
            pltpu.VMEM((RING, MOE_ROWS, LANES), F32),
            pltpu.VMEM((RING, MOE_ROWS * ROW_TILES, LANES), F32),
            pltpu.VMEM((MOE_ROWS * ROW_TILES, LANES), F32),
            pltpu.VMEM((W_STAGE, d, f), F32), pltpu.VMEM((W_STAGE, d, f), F32),
            pltpu.VMEM((W_STAGE, f, d), F32),
            pltpu.VMEM((n_res, d, f), BF16), pltpu.VMEM((n_res, d, f), BF16),
            pltpu.VMEM((n_res, f, d), BF16),
            pltpu.SemaphoreType.DMA((RING,)),
            pltpu.SemaphoreType.DMA((RING,)),
            pltpu.SemaphoreType.DMA(()),
            pltpu.SemaphoreType.DMA((W_STAGE,)),
        ],
    )
    return pl.pallas_call(
        functools.partial(_moe_kernel, layer, n_blocks),
        grid_spec=grid_spec,
        out_shape=jax.ShapeDtypeStruct((n_rows * ROW_TILES, LANES), F32),
        compiler_params=pltpu.CompilerParams(
            dimension_semantics=("arbitrary",), vmem_limit_bytes=VMEM_LIMIT),
        name="moe_half",
    )(*plan, x1, gates_sorted, w_gate, w_up, w_down, ln_g.reshape(1, d), ln_b.reshape(1, d))


def _plan_blocks(route_f, route_i, counts_f, n_tok):
    cls, rank = route_i[0], route_i[1]
    counts = counts_f[:N_CLASSES, 0].astype(I32)
    padded = (counts + MOE_ROWS - 1) // MOE_ROWS * MOE_ROWS
    pad_end = jnp.cumsum(padded)
    pad_start = pad_end - padded
    start = jnp.cumsum(counts) - counts
    class_ids = jnp.arange(N_CLASSES, dtype=I32)
    dest = jnp.sum(jnp.where(cls[:, None] == class_ids[None], pad_start[None], 0), axis=1) + rank
    _, order, g_lo, g_hi = lax.sort(
        (dest, jnp.arange(n_tok, dtype=I32), route_f[0], route_f[1]), num_keys=1)
    order = jnp.pad(order, (0, MOE_ROWS)) * ROW_TILES
    gates_sorted = jnp.pad(jnp.stack([g_lo, g_hi], axis=1), ((0, MOE_ROWS), (0, LANES - 2)))

    n_blocks = n_tok // MOE_ROWS + N_CLASSES
    blk_ids = jnp.arange(n_blocks, dtype=I32)
    blk_start = blk_ids * MOE_ROWS
    nblk = pad_end[-1] // MOE_ROWS
    blk_cls = jnp.minimum(jnp.sum(pad_end[None] <= blk_start[:, None], axis=1), N_CLASSES - 1)
    blk_cls = jnp.where(blk_ids < nblk, blk_cls, blk_cls[jnp.maximum(nblk - 1, 0)]).astype(I32)
    in_cls = blk_start - pad_start[blk_cls]
    active = blk_ids < nblk
    blk_off = jnp.where(active, start[blk_cls] + in_cls, 0).astype(I32)
    nvalid = jnp.where(active, jnp.clip(counts[blk_cls] - in_cls, 0, MOE_ROWS), 0).astype(I32)
    trash = ((n_tok + blk_start - blk_off - nvalid) * ROW_TILES).astype(I32)
    pair_lo = jnp.array(PAIR_LO, I32)
    pair_hi = jnp.array(PAIR_HI, I32)
    blk_e1 = (blk_cls // N_PAIRS) * EXPERTS_PER_GROUP + pair_lo[blk_cls % N_PAIRS]
    blk_e2 = (blk_cls // N_PAIRS) * EXPERTS_PER_GROUP + pair_hi[blk_cls % N_PAIRS]
    lead = lambda a, v: jnp.concatenate([jnp.full((1,), v, I32), a])
    plan = (order, lead(blk_off, 0), lead(nvalid, 0),
            lead(trash, n_blocks * MOE_ROWS * ROW_TILES),
            blk_e1, blk_e2, nblk.astype(I32).reshape(1))
    return plan, gates_sorted


def _untile_kernel(x_ref, o_ref):
    o_ref[...] = _load_rows(x_ref, o_ref.shape[0])


def _token_rows_to_matrix(xf, n_tok, tm):
    return pl.pallas_call(
        _untile_kernel,
        grid=(n_tok // tm,),
        in_specs=[pl.BlockSpec((tm * ROW_TILES, LANES), lambda i: (i, 0))],
        out_specs=pl.BlockSpec((tm, D_MODEL), lambda i: (i, 0)),
        out_shape=jax.ShapeDtypeStruct((n_tok, D_MODEL), F32),
        compiler_params=pltpu.CompilerParams(
            dimension_semantics=("arbitrary",), vmem_limit_bytes=VMEM_LIMIT),
        name="token_rows_to_matrix",
    )(xf)


def kernel(x, mem, conv_w_in, conv_w, mlstm_w_in, mlstm_gate_b, mlstm_norm_g, w_kv_mem, w_out,
           ln1_g, ln1_b, ln2_g, ln2_b, router_w, router_b, w_gate, w_up, w_down):
    nb, seq, d = x.shape
    n_tok = nb * seq
    assert d == D_MODEL and seq % TM_CONV == 0 and seq % TM_MLSTM == 0 and TM_MLSTM % CHUNK == 0
    assert n_tok // MOE_ROWS >= RING
    assert n_tok % TM_UNTILE == 0

    kbd, vbd = _memory_kv(mem, w_kv_mem)

    perm = lambda a: jnp.swapaxes(a.reshape(N_GROUPS, EXPERTS_PER_GROUP, -1), 0, 1)
    rwt = perm(router_w.T).reshape(N_EXPERTS, d).astype(BF16)
    rb = perm(router_b).reshape(N_EXPERTS, 1)

    xf = x.reshape(n_tok, d)
    for i in range(DEPTH):
        j = i // 2
        wout = w_out[i].astype(BF16)
        if i % 2 == 0:
            win = conv_w_in[j].astype(BF16)
            outs = _mixer_half(_conv_half_kernel, TM_CONV, xf, nb, seq, [win, conv_w[j]],
                               i, kbd, vbd, wout, ln1_g[i], ln1_b[i], rwt, rb,
                               [pltpu.VMEM((8, MIX_W), F32)], "conv_half")
        else:
            w = mlstm_w_in[j]
            o0 = 4 * MIX_W
            w_gates = jnp.pad(w[:, o0:o0 + 2 * ML_HEADS], ((0, 0), (0, GATE_PAD - 2 * ML_HEADS)))
            win = jnp.concatenate([w[:, :3 * MIX_W], w_gates, w[:, 3 * MIX_W:o0],
                                   w[:, o0 + 2 * ML_HEADS:]], axis=1).astype(BF16)
            gate_b = jnp.pad(mlstm_gate_b[j], (0, GATE_PAD - 2 * ML_HEADS)).reshape(1, GATE_PAD)
            norm_g = mlstm_norm_g[j].reshape(1, MIX_W)
            outs = _mixer_half(_mlstm_half_kernel, TM_MLSTM, xf, nb, seq, [win, gate_b, norm_g],
                               i, kbd, vbd, wout, ln1_g[i], ln1_b[i], rwt, rb,
                               [pltpu.VMEM((ML_HEADS, ML_DH, 2 * ML_DH), F32),
                                pltpu.VMEM((1, LANES), F32),
                                pltpu.VMEM((TM_MLSTM, MIX_W), F32)], "mlstm_half")
        x1, route_f, route_i, counts_f = outs
        plan, gates_sorted = _plan_blocks(route_f, route_i, counts_f, n_tok)
        xf = _moe_half(i, x1, plan, gates_sorted, w_gate, w_up, w_down, ln2_g[i], ln2_b[i])
    return _token_rows_to_matrix(xf, n_tok, TM_UNTILE).reshape(nb, seq, d)
```

```python
import functools

import jax
import jax.numpy as jnp
from jax import lax
from jax.experimental import pallas as pl
from jax.experimental.pallas import tpu as pltpu

F32 = jnp.float32
BF16 = jnp.bfloat16
I32 = jnp.int32

D_MODEL = 1024
DEPTH = 4
MIX_W = 768
XATT_HEADS = 4
XATT_DH = 64
XATT_W = XATT_HEADS * XATT_DH
CONV_K = 3
ML_HEADS = 6
ML_DH = MIX_W // ML_HEADS
CHUNK = 128
N_EXPERTS = 32
N_GROUPS = 8
EXPERTS_PER_GROUP = N_EXPERTS // N_GROUPS
D_FF = 512
ALPHA = (2.0 * DEPTH) ** 0.25
LN_EPS = 1e-5

LANES = 128
SUBLANES = 8
ROW_TILES = D_MODEL // LANES
GATE_PAD = LANES
MOE_ROWS = 128
RING = 3
W_SETS = 2
W_STAGE = 2
PAIR_LO = (0, 0, 1, 1, 2, 0)
PAIR_HI = (1, 2, 2, 3, 3, 3)
N_PAIRS = len(PAIR_LO)
N_CLASSES = N_GROUPS * N_PAIRS
CLASS_PAD = 64
VMEM_LIMIT = 52 * 1024 * 1024

TM_CONV = 1024
TM_MLSTM = 512
TM_UNTILE = 1024


def _layer_norm(z, g, b):
    mu = jnp.mean(z, axis=-1, keepdims=True)
    zc = z - mu
    var = jnp.mean(zc * zc, axis=-1, keepdims=True)
    return zc * lax.rsqrt(var + LN_EPS) * g + b


def _load_rows(ref, n):
    return jnp.concatenate([ref[pl.ds(j, n, stride=ROW_TILES), :] for j in range(ROW_TILES)],
                           axis=-1)


def _load_x(x_ref):
    if x_ref.shape[1] == D_MODEL:
        return x_ref[...]
    return _load_rows(x_ref, x_ref.shape[0] // ROW_TILES)


def _store_rows(ref, val):
    for j in range(ROW_TILES):
        ref[pl.ds(j, val.shape[0], stride=ROW_TILES), :] = val[:, j * LANES:(j + 1) * LANES]


def _dot(a, b):
    return jnp.dot(a, b, preferred_element_type=F32)


def _dot_nt(a, b):
    return lax.dot_general(a, b, (((1,), (1,)), ((), ())), preferred_element_type=F32)


def _kv_kernel(mem_ref, w_ref, kbd_ref, vbd_ref):
    nb, mem_len = mem_ref.shape[0], mem_ref.shape[1]
    w = w_ref[0].astype(BF16)
    row_head = lax.broadcasted_iota(I32, (XATT_W, mem_len), 0) // XATT_DH
    col_head = lax.broadcasted_iota(I32, (mem_len, XATT_W), 1) // XATT_DH
    for b in range(nb):
        kv = _dot(mem_ref[b].astype(BF16), w)
        kt = (kv[:, :XATT_W] * (XATT_DH ** -0.5)).T
        v = kv[:, XATT_W:]
        for h in range(XATT_HEADS):
            kbd_ref[0, b, :, h * mem_len:(h + 1) * mem_len] = (
                jnp.where(row_head == h, kt, 0.0).astype(BF16))
            vbd_ref[0, b, h * mem_len:(h + 1) * mem_len, :] = (
                jnp.where(col_head == h, v, 0.0).astype(BF16))


def _memory_kv(mem, w_kv_mem):
    nb, mem_len, d = mem.shape
    depth = w_kv_mem.shape[0]
    return pl.pallas_call(
        _kv_kernel,
        grid=(depth,),
        in_specs=[pl.BlockSpec((nb, mem_len, d), lambda i: (0, 0, 0)),
                  pl.BlockSpec((1, d, 2 * XATT_W), lambda i: (i, 0, 0))],
        out_specs=[pl.BlockSpec((1, nb, XATT_W, XATT_HEADS * mem_len), lambda i: (i, 0, 0, 0)),
                   pl.BlockSpec((1, nb, XATT_HEADS * mem_len, XATT_W), lambda i: (i, 0, 0, 0))],
        out_shape=[jax.ShapeDtypeStruct((depth, nb, XATT_W, XATT_HEADS * mem_len), BF16),
                   jax.ShapeDtypeStruct((depth, nb, XATT_HEADS * mem_len, XATT_W), BF16)],
        compiler_params=pltpu.CompilerParams(
            dimension_semantics=("arbitrary",), vmem_limit_bytes=VMEM_LIMIT),
        name="memory_kv",
    )(mem, w_kv_mem)


def _route_tile(x1b, rwt_ref, rb_ref, tri_ref, base_ref, rf_ref, ri_ref, cnt_ref):
    tm = x1b.shape[0]
    scores = jax.nn.sigmoid(_dot_nt(rwt_ref[...], x1b))
    sel = scores + rb_ref[...]
    s = [sel[SUBLANES * j:SUBLANES * (j + 1)] for j in range(EXPERTS_PER_GROUP)]
    p = [scores[SUBLANES * j:SUBLANES * (j + 1)] for j in range(EXPERTS_PER_GROUP)]
    hi1, lo1 = jnp.maximum(s[0], s[1]), jnp.minimum(s[0], s[1])
    hi2, lo2 = jnp.maximum(s[2], s[3]), jnp.minimum(s[2], s[3])
    gs = jnp.maximum(hi1, hi2) + jnp.maximum(jnp.minimum(hi1, hi2), jnp.maximum(lo1, lo2))
    sub = lax.broadcasted_iota(I32, (N_GROUPS, tm), 0)
    gi = jnp.min(jnp.where(gs == jnp.max(gs, axis=0, keepdims=True), sub, N_GROUPS),
                 axis=0, keepdims=True)
    gm = sub == gi
    v = [jnp.sum(jnp.where(gm, sj, 0.0), axis=0, keepdims=True) for sj in s]
    pv = [jnp.sum(jnp.where(gm, pj, 0.0), axis=0, keepdims=True) for pj in p]
    best, i1 = v[0], jnp.zeros((1, tm), I32)
    for j in range(1, EXPERTS_PER_GROUP):
        u = v[j] > best
        best, i1 = jnp.where(u, v[j], best), jnp.where(u, j, i1)
    best2, i2 = jnp.full((1, tm), -jnp.inf, F32), jnp.zeros((1, tm), I32)
    for j in range(EXPERTS_PER_GROUP):
        cand = jnp.where(i1 == j, -jnp.inf, v[j])
        u = cand > best2
        best2, i2 = jnp.where(u, cand, best2), jnp.where(u, j, i2)

    def pick(idx):
        return jnp.where(idx == 0, pv[0], jnp.where(idx == 1, pv[1],
                                                    jnp.where(idx == 2, pv[2], pv[3])))

    p1, p2 = pick(i1), pick(i2)
    tot = p1 + p2
    first_lo = i1 < i2
    g_lo = jnp.where(first_lo, p1, p2) / tot
    g_hi = jnp.where(first_lo, p2, p1) / tot
    lo, hi = jnp.minimum(i1, i2), jnp.maximum(i1, i2)
    pair = jnp.where(hi == 1, 0, jnp.where(hi == 2, jnp.where(lo == 0, 1, 2),
                                           jnp.where(lo == 1, 3, jnp.where(lo == 2, 4, 5))))
    cls = gi * N_PAIRS + pair

    oh = lax.broadcasted_iota(I32, (CLASS_PAD, tm), 0) == cls
    pref = _dot(oh.astype(F32).astype(BF16), tri_ref[...])
    base = base_ref[...]
    rank = jnp.sum(jnp.where(oh, pref + base[:, 0:1] - 1.0, 0.0), axis=0, keepdims=True)
    base = base + pref[:, tm - 1:tm]
    base_ref[...] = base
    cnt_ref[...] = base

    sub8 = lax.broadcasted_iota(I32, (SUBLANES, tm), 0)
    rf_ref[...] = jnp.where(sub8 == 0, g_lo, jnp.where(sub8 == 1, g_hi, 0.0))
    ri_ref[...] = jnp.where(sub8 == 0, cls, jnp.where(sub8 == 1, rank.astype(I32), 0))


def _init_tail(tail_refs):
    base_ref = tail_refs[-1]

    @pl.when((pl.program_id(0) == 0) & (pl.program_id(1) == 0))
    def _():
        base_ref[...] = jnp.zeros_like(base_ref)


def _cross_attention(q_mem, tail_refs):
    kbd_ref, vbd_ref = tail_refs[:2]
    s = _dot(q_mem.astype(BF16), kbd_ref[0, 0])
    mem_len = s.shape[1] // XATT_HEADS
    sh = [s[:, h * mem_len:(h + 1) * mem_len] for h in range(XATT_HEADS)]
    mx = [jnp.max(t, axis=-1, keepdims=True) for t in sh]
    e = [jnp.exp(t - m) for t, m in zip(sh, mx)]
    tot = [jnp.sum(t, axis=-1, keepdims=True) for t in e]
    ps = [(t / n).astype(BF16) for t, n in zip(e, tot)]
    return _dot(jnp.concatenate(ps, axis=-1), vbd_ref[0, 0])


def _mixer_tail(x, h_loc, h_mem, tail_refs):
    (_, _, wout_ref, g_ref, b_ref, rwt_ref, rb_ref, tri_ref,
     x1_ref, rf_ref, ri_ref, cnt_ref, base_ref) = tail_refs
    hcat = jnp.concatenate([h_loc.astype(BF16), h_mem.astype(BF16)], axis=-1)
    y = _dot(hcat, wout_ref[...])
    x1 = _layer_norm(ALPHA * x + y, g_ref[...], b_ref[...])
    _store_rows(x1_ref, x1)
    _route_tile(x1.astype(BF16), rwt_ref, rb_ref, tri_ref, base_ref, rf_ref, ri_ref, cnt_ref)


def _conv_half_kernel(x_ref, win_ref, cw_ref, *refs):
    tail_refs, carry_ref = refs[:-1], refs[-1]
    _init_tail(tail_refs)

    @pl.when(pl.program_id(1) == 0)
    def _():
        carry_ref[...] = jnp.zeros_like(carry_ref)

    x = _load_x(x_ref)
    tm = x.shape[0]
    xb = x.astype(BF16)
    q_mem = _dot(xb, win_ref[:, 3 * MIX_W:])
    h_mem = _cross_attention(q_mem, tail_refs)
    proj = _dot(xb, win_ref[:, :3 * MIX_W])
    bg = proj[:, 0:MIX_W]
    cg = proj[:, MIX_W:2 * MIX_W]
    xin = proj[:, 2 * MIX_W:3 * MIX_W]
    u = cg * xin
    prev = carry_ref[...]
    row = lax.broadcasted_iota(I32, u.shape, 0)
    u1 = jnp.where(row == 0, prev[7:8], pltpu.roll(u, 1, 0))
    u2 = jnp.where(row == 0, prev[6:7],
                   jnp.where(row == 1, prev[7:8], pltpu.roll(u, 2, 0)))
    carry_ref[...] = u[tm - 8:tm]
    cw = cw_ref[...]
    c = cw[0:1] * u2 + cw[1:2] * u1 + cw[2:3] * u
    _mixer_tail(x, bg * c, h_mem, tail_refs)


def _log_sigmoid(z):
    return -(jnp.maximum(-z, 0.0) + jnp.log1p(jnp.exp(-jnp.abs(z))))


def _mlstm_half_kernel(x_ref, win_ref, gb_ref, ng_ref, *refs):
    tail_refs, (state_ref, m_ref, h_ref) = refs[:-3], refs[-3:]

    @pl.when(pl.program_id(1) == 0)
    def _():
        m_ref[...] = jnp.zeros_like(m_ref)

    x = _load_x(x_ref)
    tm = x.shape[0]
    n_cell = 3 * MIX_W + GATE_PAD
    xb = x.astype(BF16)
    proj = _dot(xb, win_ref[:, :n_cell])
    gates = proj[:, 3 * MIX_W:n_cell] + gb_ref[...]
    row = lax.broadcasted_iota(I32, (CHUNK, GATE_PAD), 0)
    head_lane = lax.broadcasted_iota(I32, (tm, GATE_PAD), 1) < ML_HEADS
    log_i = jnp.where(head_lane, gates, 0.0)
    log_f = jnp.where(head_lane, pltpu.roll(_log_sigmoid(gates), GATE_PAD - ML_HEADS, 1), 0.0)
    causal = (lax.broadcasted_iota(I32, (CHUNK, CHUNK), 0)
              >= lax.broadcasted_iota(I32, (CHUNK, CHUNK), 1))
    ones_col = (lax.broadcasted_iota(I32, (CHUNK, ML_DH), 1) == 0).astype(BF16)

    def scan_rows(a, op, fill):
        sh = 1
        while sh < CHUNK:
            a = op(a, jnp.where(row >= sh, pltpu.roll(a, sh, 0), fill))
            sh *= 2
        return a

    chunks, heads = range(tm // CHUNK), range(ML_HEADS)
    pairs = [(c, h) for c in chunks for h in heads]

    m_prev = m_ref[...]
    gate = []
    for c in chunks:
        li = log_i[c * CHUNK:(c + 1) * CHUNK]
        b = scan_rows(log_f[c * CHUNK:(c + 1) * CHUNK], jnp.add, 0.0)
        wd = li - b
        m_t = b + jnp.maximum(scan_rows(wd, jnp.maximum, -jnp.inf), m_prev)
        g = b[CHUNK - 1:CHUNK, :]
        a = g + wd
        m_new = jnp.maximum(g + m_prev, jnp.max(a, axis=0, keepdims=True))
        gate.append(dict(u=b - m_t, w_inter=jnp.exp(b + m_prev - m_t), clamp=jnp.exp(-m_t),
                         decay=jnp.exp(g + m_prev - m_new), wd_t=wd.T,
                         wn_t=jnp.exp(a - m_new).T))
        m_prev = m_new
    m_ref[...] = m_prev

    col = lambda c, h, o=0: proj[c * CHUNK:(c + 1) * CHUNK, o + h * ML_DH:o + (h + 1) * ML_DH]
    qb = {p: col(*p).astype(BF16) for p in pairs}
    kf = {p: col(*p, MIX_W) * (ML_DH ** -0.5) for p in pairs}
    kb = {p: kf[p].astype(BF16) for p in pairs}
    vb = {p: col(*p, 2 * MIX_W).astype(BF16) for p in pairs}
    qk = {p: _dot_nt(qb[p], kb[p]) for p in pairs}
    proj_tail = _dot(xb, win_ref[:, n_cell:])
    s = {(c, h): qk[c, h] * jnp.exp(jnp.where(
        causal, gate[c]["u"][:, h:h + 1] + gate[c]["wd_t"][h:h + 1, :], -jnp.inf))
        for c, h in pairs}
    sv = {p: _dot(s[p].astype(BF16), vb[p]) for p in pairs}
    s_sum = {p: jnp.sum(s[p], axis=-1, keepdims=True) for p in pairs}
    ktw = {(c, h): (kf[c, h].T * gate[c]["wn_t"][h:h + 1, :]).astype(BF16) for c, h in pairs}
    kv = {p: _dot(ktw[p], jnp.concatenate([vb[p], ones_col], axis=-1)) for p in pairs}
    @pl.when(pl.program_id(1) == 0)
    def _():
        state_ref[...] = jnp.zeros_like(state_ref)

    state = [state_ref[h] for h in heads]
    qs = {}
    for c in chunks:
        for h in heads:
            qs[c, h] = _dot(qb[c, h], state[h].astype(BF16))
        state = [gate[c]["decay"][:, h:h + 1] * state[h] + kv[c, h] for h in heads]
    for h in heads:
        state_ref[h] = state[h]
    wi = {(c, h): gate[c]["w_inter"][:, h:h + 1] for c, h in pairs}
    den = {p: s_sum[p] + wi[p] * qs[p][:, ML_DH:ML_DH + 1] for p in pairs}
    hh = {(c, h): (sv[c, h] + wi[c, h] * qs[c, h][:, 0:ML_DH])
          / jnp.maximum(jnp.abs(den[c, h]), gate[c]["clamp"][:, h:h + 1]) for c, h in pairs}
    hc = {p: hh[p] - jnp.mean(hh[p], axis=-1, keepdims=True) for p in pairs}
    var = {p: jnp.mean(hc[p] * hc[p], axis=-1, keepdims=True) for p in pairs}
    for c, h in pairs:
        h_ref[c * CHUNK:(c + 1) * CHUNK, h * ML_DH:(h + 1) * ML_DH] = (
            hc[c, h] * lax.rsqrt(var[c, h] + LN_EPS))

    h_loc = jax.nn.sigmoid(proj_tail[:, 0:MIX_W]) * (h_ref[...] * ng_ref[...])
    _init_tail(tail_refs)
    h_mem = _cross_attention(proj_tail[:, MIX_W:], tail_refs)
    _mixer_tail(x, h_loc, h_mem, tail_refs)


def _mixer_half(kernel_fn, tm, x, nb, seq, mixer_args, layer, kbd, vbd, wout, ln_g, ln_b, rwt, rb,
                scratch, name):
    d = D_MODEL
    n_tok = nb * seq
    spb = seq // tm
    const2 = lambda b, s: (0, 0)
    tri = jnp.triu(jnp.ones((tm, tm), BF16))
    tail_args = [kbd, vbd, wout, ln_g.reshape(1, d), ln_b.reshape(1, d), rwt, rb, tri]
    row_map = lambda b, s: (b * spb + s, 0)
    x_spec = (pl.BlockSpec((tm, d), row_map) if x.shape[1] == d
              else pl.BlockSpec((tm * ROW_TILES, LANES), row_map))
    in_specs = ([x_spec]
                + [pl.BlockSpec(a.shape, const2) for a in mixer_args]
                + [pl.BlockSpec((1, 1) + kbd.shape[2:], lambda b, s: (layer, b, 0, 0)),
                   pl.BlockSpec((1, 1) + vbd.shape[2:], lambda b, s: (layer, b, 0, 0))]
                + [pl.BlockSpec(a.shape, const2) for a in tail_args[2:]])
    tok_spec = lambda rows: pl.BlockSpec((rows, tm), lambda b, s: (0, b * spb + s))
    return pl.pallas_call(
        kernel_fn,
        grid=(nb, spb),
        in_specs=in_specs,
        out_specs=[pl.BlockSpec((tm * ROW_TILES, LANES), row_map),
                   tok_spec(SUBLANES), tok_spec(SUBLANES),
                   pl.BlockSpec((CLASS_PAD, LANES), const2)],
        out_shape=[jax.ShapeDtypeStruct((n_tok * ROW_TILES, LANES), F32),
                   jax.ShapeDtypeStruct((SUBLANES, n_tok), F32),
                   jax.ShapeDtypeStruct((SUBLANES, n_tok), I32),
                   jax.ShapeDtypeStruct((CLASS_PAD, LANES), F32)],
        scratch_shapes=[pltpu.VMEM((CLASS_PAD, LANES), F32)] + scratch,
        compiler_params=pltpu.CompilerParams(
            dimension_semantics=("arbitrary", "arbitrary"), vmem_limit_bytes=VMEM_LIMIT),
        name=name,
    )(x, *mixer_args, *tail_args)


def _moe_kernel(layer, n_blocks,
                order_ref, off_ref, nvalid_ref, trash_ref, e1_ref, e2_ref, nblk_ref,
                x_hbm, g_hbm, wg_hbm, wu_hbm, wd_hbm, lng_ref, lnb_ref, out_hbm,
                xbuf, gbuf, obuf, zbuf, sg, su, sd, wg_bf, wu_bf, wd_bf,
                gsem, ssem, zsem, wsem):
    nblk = nblk_ref[0]
    last = nblk - 1
    ring = lambda j: (j + RING) % RING
    n_res = W_SETS * EXPERTS_PER_GROUP

    def gate_copy(blk, slt):
        return pltpu.make_async_copy(g_hbm.at[pl.ds(off_ref[blk + 1], MOE_ROWS)], gbuf.at[slt],
                                     gsem.at[slt])

    def tile_row(r):
        return r * ROW_TILES if isinstance(r, int) else pl.multiple_of(r * ROW_TILES, ROW_TILES)

    def gather_row(base, r, slt):
        t8 = pl.multiple_of(order_ref[base + r], ROW_TILES)
        pltpu.make_async_copy(x_hbm.at[pl.ds(t8, ROW_TILES)],
                              xbuf.at[slt, pl.ds(tile_row(r), ROW_TILES)], gsem.at[slt]).start()

    def issue_gather(blk, slt, unrolled=True):
        base = off_ref[blk + 1]
        if unrolled:
            for r in range(MOE_ROWS):
                gather_row(base, r, slt)
        else:
            lax.fori_loop(0, MOE_ROWS, lambda r, c: (gather_row(base, r, slt), c)[1], 0)
        gate_copy(blk, slt).start()

    def wait_gather(slt):
        pltpu.make_async_copy(x_hbm.at[pl.ds(0, MOE_ROWS * ROW_TILES)], xbuf.at[slt],
                              gsem.at[slt]).wait()
        gate_copy(0, slt).wait()

    def scatter_row(base, nvalid, trash8, r, slt):
        t8 = jnp.where(r < nvalid, order_ref[base + r], trash8 + r * ROW_TILES)
        pltpu.make_async_copy(
            obuf.at[slt, pl.ds(tile_row(r), ROW_TILES)],
            out_hbm.at[pl.ds(pl.multiple_of(t8, ROW_TILES), ROW_TILES)], ssem.at[slt]).start()

    def issue_scatter(blk, slt, unrolled=True):
        args = (off_ref[blk + 1], nvalid_ref[blk + 1], trash_ref[blk + 1])
        if unrolled:
            for r in range(MOE_ROWS):
                scatter_row(*args, r, slt)
        else:
            lax.fori_loop(0, MOE_ROWS, lambda r, c: (scatter_row(*args, r, slt), c)[1], 0)

    def wait_scatter(slt):
        pltpu.make_async_copy(obuf.at[slt], out_hbm.at[pl.ds(0, MOE_ROWS * ROW_TILES)],
                              ssem.at[slt]).wait()

    def weight_copies(k, stage):
        return [pltpu.make_async_copy(src.at[layer, k], dst.at[stage], wsem.at[stage])
                for src, dst in ((wg_hbm, sg), (wu_hbm, su), (wd_hbm, sd))]

    def load_start(k):
        for cp in weight_copies(k, k % W_STAGE):
            cp.start()

    def load_wait(k):
        for cp in weight_copies(k, k % W_STAGE):
            cp.wait()

    def load_finish(k):
        load_wait(k)
        stage, res = k % W_STAGE, k % n_res
        wg_bf[res] = sg[stage].astype(BF16)
        wu_bf[res] = su[stage].astype(BF16)
        wd_bf[res] = sd[stage].astype(BF16)

    def start_if(pred, kd):
        @pl.when(pred)
        def _():
            load_start(kd)
        return kd + pred.astype(I32)

    def service_loader(group, kd, kc):
        cap = jnp.minimum((group + W_SETS) * EXPERTS_PER_GROUP, N_EXPERTS)
        need = (group + 1) * EXPERTS_PER_GROUP

        def catch_up(c):
            kd, kc = c
            kd = start_if(kd <= kc, kd)
            kd = start_if((kd < kc + W_STAGE) & (kd < cap), kd)
            load_finish(kc)
            return kd, kc + 1
        kd, kc = lax.while_loop(lambda c: c[1] < need, catch_up, (kd, kc))

        ready = (kc < kd) & (kc < cap)

        @pl.when(ready)
        def _():
            load_finish(kc)
        kc = kc + ready.astype(I32)
        kd = start_if((kd < kc + W_STAGE) & (kd < cap), kd)
        return kd, kc

    issue_gather(0, 0, unrolled=False)
    issue_gather(jnp.minimum(1, last), 1, unrolled=False)
    obuf[RING - 1] = jnp.zeros(obuf.shape[1:], F32)

    def block(b, loader):
        slot = b % RING
        e1, e2 = e1_ref[b], e2_ref[b]
        loader = service_loader(e1 // EXPERTS_PER_GROUP, *loader)

        @pl.when(b >= RING - 1)
        def _():
            wait_scatter(slot)

        wait_gather(slot)
        x = _load_rows(xbuf.at[slot], MOE_ROWS)
        xb = x.astype(BF16)
        issue_gather(jnp.minimum(b + 2, last), ring(b + 2))
        issue_scatter(b - 1, ring(b - 1))
        res = (e1 % n_res, e2 % n_res)
        hg = [_dot(xb, wg_bf[r]) for r in res]
        hu = [_dot(xb, wu_bf[r]) for r in res]
        act = [(g * jax.nn.sigmoid(g) * u).astype(BF16) for g, u in zip(hg, hu)]
        y = [_dot(a, wd_bf[r]) for a, r in zip(act, res)]
        gates = gbuf[slot]
        m = gates[:, 0:1] * y[0] + gates[:, 1:2] * y[1]
        o = _layer_norm(ALPHA * x + m, lng_ref[...], lnb_ref[...])
        _store_rows(obuf.at[slot], o)
        return loader

    first = (e1_ref[0] // EXPERTS_PER_GROUP) * EXPERTS_PER_GROUP
    kd, kc = lax.fori_loop(0, nblk, block, (first, first))

    lax.fori_loop(kc, kd, lambda k, c: (load_wait(k), c)[1], 0)
    issue_scatter(last, ring(last), unrolled=False)
    for slt in range(RING):
        wait_scatter(slt)
    wait_gather(ring(nblk))
    wait_gather(ring(nblk + 1))

    zbuf[...] = jnp.zeros_like(zbuf)

    def fill_copy(j):
        return pltpu.make_async_copy(
            zbuf, out_hbm.at[pl.ds(j * MOE_ROWS * ROW_TILES, MOE_ROWS * ROW_TILES)], zsem)
    lax.fori_loop(nblk, n_blocks, lambda j, c: (fill_copy(j).start(), c)[1], 0)
    lax.fori_loop(nblk, n_blocks, lambda j, c: (fill_copy(j).wait(), c)[1], 0)


def _moe_half(layer, x1, plan, gates_sorted, w_gate, w_up, w_down, ln_g, ln_b):
    n_blocks = plan[4].shape[0]
    n_rows = (n_blocks + 1) * MOE_ROWS
    d, f = D_MODEL, D_FF
    n_res = W_SETS * EXPERTS_PER_GROUP
    const_spec = pl.BlockSpec((1, d), lambda i, *_: (0, 0))
    any_spec = pl.BlockSpec(memory_space=pl.ANY)
    grid_spec = pltpu.PrefetchScalarGridSpec(
        num_scalar_prefetch=7,
        grid=(1,),
        in_specs=[any_spec, any_spec, any_spec, any_spec, any_spec, const_spec, const_spec],
        out_specs=any_spec,
        scratch_shapes=[
            pltpu.VMEM((RING, MOE_ROWS * ROW_TILES, LANES), F32),
            pltpu.VMEM((RING, MOE_ROWS, LANES), F32),
            pltpu.VMEM((RING, MOE_ROWS * ROW_TILES, LANES), F32),
            pltpu.VMEM((MOE_ROWS * ROW_TILES, LANES), F32),
            pltpu.VMEM((W_STAGE, d, f), F32), pltpu.VMEM((W_STAGE, d, f), F32),
            pltpu.VMEM((W_STAGE, f, d), F32),
            pltpu.VMEM((n_res, d, f), BF16), pltpu.VMEM((n_res, d, f), BF16),
            pltpu.VMEM((n_res, f, d), BF16),
            pltpu.SemaphoreType.DMA((RING,)),
            pltpu.SemaphoreType.DMA((RING,)),
            pltpu.SemaphoreType.DMA(()),
            pltpu.SemaphoreType.DMA((W_STAGE,)),
        ],
    )
    return pl.pallas_call(
        functools.partial(_moe_kernel, layer, n_blocks),
        grid_spec=grid_spec,
        out_shape=jax.ShapeDtypeStruct((n_rows * ROW_TILES, LANES), F32),
        compiler_params=pltpu.CompilerParams(
            dimension_semantics=("arbitrary",), vmem_limit_bytes=VMEM_LIMIT),
        name="moe_half",
    )(*plan, x1, gates_sorted, w_gate, w_up, w_down, ln_g.reshape(1, d), ln_b.reshape(1, d))


def _plan_blocks(route_f, route_i, counts_f, n_tok):
    cls, rank = route_i[0], route_i[1]
    counts = counts_f[:N_CLASSES, 0].astype(I32)
    padded = (counts + MOE_ROWS - 1) // MOE_ROWS * MOE_ROWS
    pad_end = jnp.cumsum(padded)
    pad_start = pad_end - padded
    start = jnp.cumsum(counts) - counts
    class_ids = jnp.arange(N_CLASSES, dtype=I32)
    dest = jnp.sum(jnp.where(cls[:, None] == class_ids[None], pad_start[None], 0), axis=1) + rank
    _, order, g_lo, g_hi = lax.sort(
        (dest, jnp.arange(n_tok, dtype=I32), route_f[0], route_f[1]), num_keys=1)
    order = jnp.pad(order, (0, MOE_ROWS)) * ROW_TILES
    gates_sorted = jnp.pad(jnp.stack([g_lo, g_hi], axis=1), ((0, MOE_ROWS), (0, LANES - 2)))

    n_blocks = n_tok // MOE_ROWS + N_CLASSES
    blk_ids = jnp.arange(n_blocks, dtype=I32)
    blk_start = blk_ids * MOE_ROWS
    nblk = pad_end[-1] // MOE_ROWS
    blk_cls = jnp.minimum(jnp.sum(pad_end[None] <= blk_start[:, None], axis=1), N_CLASSES - 1)
    blk_cls = jnp.where(blk_ids < nblk, blk_cls, blk_cls[jnp.maximum(nblk - 1, 0)]).astype(I32)
    in_cls = blk_start - pad_start[blk_cls]
    active = blk_ids < nblk
    blk_off = jnp.where(active, start[blk_cls] + in_cls, 0).astype(I32)
    nvalid = jnp.where(active, jnp.clip(counts[blk_cls] - in_cls, 0, MOE_ROWS), 0).astype(I32)
    trash = ((n_tok + blk_start - blk_off - nvalid) * ROW_TILES).astype(I32)
    pair_lo = jnp.array(PAIR_LO, I32)
    pair_hi = jnp.array(PAIR_HI, I32)
    blk_e1 = (blk_cls // N_PAIRS) * EXPERTS_PER_GROUP + pair_lo[blk_cls % N_PAIRS]
    blk_e2 = (blk_cls // N_PAIRS) * EXPERTS_PER_GROUP + pair_hi[blk_cls % N_PAIRS]
    lead = lambda a, v: jnp.concatenate([jnp.full((1,), v, I32), a])
    plan = (order, lead(blk_off, 0), lead(nvalid, 0),
            lead(trash, n_blocks * MOE_ROWS * ROW_TILES),
            blk_e1, blk_e2, nblk.astype(I32).reshape(1))
    return plan, gates_sorted


def _untile_kernel(x_ref, o_ref):
    o_ref[...] = _load_rows(x_ref, o_ref.shape[0])


def _token_rows_to_matrix(xf, n_tok, tm):
    return pl.pallas_call(
        _untile_kernel,
        grid=(n_tok // tm,),
        in_specs=[pl.BlockSpec((tm * ROW_TILES, LANES), lambda i: (i, 0))],
        out_specs=pl.BlockSpec((tm, D_MODEL), lambda i: (i, 0)),
        out_shape=jax.ShapeDtypeStruct((n_tok, D_MODEL), F32),
        compiler_params=pltpu.CompilerParams(
            dimension_semantics=("arbitrary",), vmem_limit_bytes=VMEM_LIMIT),
        name="token_rows_to_matrix",
    )(xf)


def kernel(x, mem, conv_w_in, conv_w, mlstm_w_in, mlstm_gate_b, mlstm_norm_g, w_kv_mem, w_out,
           ln1_g, ln1_b, ln2_g, ln2_b, router_w, router_b, w_gate, w_up, w_down):
    nb, seq, d = x.shape
    n_tok = nb * seq
    assert d == D_MODEL and seq % TM_CONV == 0 and seq % TM_MLSTM == 0 and TM_MLSTM % CHUNK == 0
    assert n_tok // MOE_ROWS >= RING
    assert n_tok % TM_UNTILE == 0

    kbd, vbd = _memory_kv(mem, w_kv_mem)

    perm = lambda a: jnp.swapaxes(a.reshape(N_GROUPS, EXPERTS_PER_GROUP, -1), 0, 1)
    rwt = perm(router_w.T).reshape(N_EXPERTS, d).astype(BF16)
    rb = perm(router_b).reshape(N_EXPERTS, 1)

    xf = x.reshape(n_tok, d)
    for i in range(DEPTH):
        j = i // 2
        wout = w_out[i].astype(BF16)
        if i % 2 == 0:
            win = conv_w_in[j].astype(BF16)
            outs = _mixer_half(_conv_half_kernel, TM_CONV, xf, nb, seq, [win, conv_w[j]],
                               i, kbd, vbd, wout, ln1_g[i], ln1_b[i], rwt, rb,
                               [pltpu.VMEM((8, MIX_W), F32)], "conv_half")
        else:
            w = mlstm_w_in[j]
            o0 = 4 * MIX_W
            w_gates = jnp.pad(w[:, o0:o0 + 2 * ML_HEADS], ((0, 0), (0, GATE_PAD - 2 * ML_HEADS)))
            win = jnp.concatenate([w[:, :3 * MIX_W], w_gates, w[:, 3 * MIX_W:o0],
                                   w[:, o0 + 2 * ML_HEADS:]], axis=1).astype(BF16)
            gate_b = jnp.pad(mlstm_gate_b[j], (0, GATE_PAD - 2 * ML_HEADS)).reshape(1, GATE_PAD)
            norm_g = mlstm_norm_g[j].reshape(1, MIX_W)
            outs = _mixer_half(_mlstm_half_kernel, TM_MLSTM, xf, nb, seq, [win, gate_b, norm_g],
                               i, kbd, vbd, wout, ln1_g[i], ln1_b[i], rwt, rb,
                               [pltpu.VMEM((ML_HEADS, ML_DH, 2 * ML_DH), F32),
                                pltpu.VMEM((1, LANES), F32),
                                pltpu.VMEM((TM_MLSTM, MIX_W), F32)], "mlstm_half")
        x1, route_f, route_i, counts_f = outs
        plan, gates_sorted = _plan_blocks(route_f, route_i, counts_f, n_tok)
        xf = _moe_half(i, x1, plan, gates_sorted, w_gate, w_up, w_down, ln2_g[i], ln2_b[i])
    return _token_rows_to_matrix(xf, n_tok, TM_UNTILE).reshape(nb, seq, d)
```

```python
import functools

import jax
import jax.numpy as jnp
from jax import lax
from jax.experimental import pallas as pl
from jax.experimental.pallas import tpu as pltpu

F32 = jnp.float32
BF16 = jnp.bfloat16
I32 = jnp.int32

D_MODEL = 1024
DEPTH = 4
MIX_W = 768
XATT_HEADS = 4
XATT_DH = 64
XATT_W = XATT_HEADS * XATT_DH
CONV_K = 3
ML_HEADS = 6
ML_DH = MIX_W // ML_HEADS
CHUNK = 128
N_EXPERTS = 32
N_GROUPS = 8
EXPERTS_PER_GROUP = N_EXPERTS // N_GROUPS
D_FF = 512
ALPHA = (2.0 * DEPTH) ** 0.25
LN_EPS = 1e-5

LANES = 128
SUBLANES = 8
ROW_TILES = D_MODEL // LANES
GATE_PAD = LANES
MOE_ROWS = 128
RING = 3
W_SETS = 2
W_STAGE = 2
PAIR_LO = (0, 0, 1, 1, 2, 0)
PAIR_HI = (1, 2, 2, 3, 3, 3)
N_PAIRS = len(PAIR_LO)
N_CLASSES = N_GROUPS * N_PAIRS
CLASS_PAD = 64
VMEM_LIMIT = 52 * 1024 * 1024

TM_CONV = 1024
TM_MLSTM = 512
TM_UNTILE = 1024


def _layer_norm(z, g, b):
    mu = jnp.mean(z, axis=-1, keepdims=True)
    zc = z - mu
    var = jnp.mean(zc * zc, axis=-1, keepdims=True)
    return zc * lax.rsqrt(var + LN_EPS) * g + b


def _load_rows(ref, n):
    return jnp.concatenate([ref[pl.ds(j, n, stride=ROW_TILES), :] for j in range(ROW_TILES)],
                           axis=-1)


def _load_x(x_ref):
    if x_ref.shape[1] == D_MODEL:
        return x_ref[...]
    return _load_rows(x_ref, x_ref.shape[0] // ROW_TILES)


def _store_rows(ref, val):
    for j in range(ROW_TILES):
        ref[pl.ds(j, val.shape[0], stride=ROW_TILES), :] = val[:, j * LANES:(j + 1) * LANES]


def _dot(a, b):
    return jnp.dot(a, b, preferred_element_type=F32)


def _dot_nt(a, b):
    return lax.dot_general(a, b, (((1,), (1,)), ((), ())), preferred_element_type=F32)


def _kv_kernel(mem_ref, w_ref, kbd_ref, vbd_ref):
    nb, mem_len = mem_ref.shape[0], mem_ref.shape[1]
    w = w_ref[0].astype(BF16)
    row_head = lax.broadcasted_iota(I32, (XATT_W, mem_len), 0) // XATT_DH
    col_head = lax.broadcasted_iota(I32, (mem_len, XATT_W), 1) // XATT_DH
    for b in range(nb):
        kv = _dot(mem_ref[b].astype(BF16), w)
        kt = (kv[:, :XATT_W] * (XATT_DH ** -0.5)).T
        v = kv[:, XATT_W:]
        for h in range(XATT_HEADS):
            kbd_ref[0, b, :, h * mem_len:(h + 1) * mem_len] = (
                jnp.where(row_head == h, kt, 0.0).astype(BF16))
            vbd_ref[0, b, h * mem_len:(h + 1) * mem_len, :] = (
                jnp.where(col_head == h, v, 0.0).astype(BF16))


def _memory_kv(mem, w_kv_mem):
    nb, mem_len, d = mem.shape
    depth = w_kv_mem.shape[0]
    return pl.pallas_call(
        _kv_kernel,
        grid=(depth,),
        in_specs=[pl.BlockSpec((nb, mem_len, d), lambda i: (0, 0, 0)),
                  pl.BlockSpec((1, d, 2 * XATT_W), lambda i: (i, 0, 0))],
        out_specs=[pl.BlockSpec((1, nb, XATT_W, XATT_HEADS * mem_len), lambda i: (i, 0, 0, 0)),
                   pl.BlockSpec((1, nb, XATT_HEADS * mem_len, XATT_W), lambda i: (i, 0, 0, 0))],
        out_shape=[jax.ShapeDtypeStruct((depth, nb, XATT_W, XATT_HEADS * mem_len), BF16),
                   jax.ShapeDtypeStruct((depth, nb, XATT_HEADS * mem_len, XATT_W), BF16)],
        compiler_params=pltpu.CompilerParams(
            dimension_semantics=("arbitrary",), vmem_limit_bytes=VMEM_LIMIT),
        name="memory_kv",
    )(mem, w_kv_mem)


def _route_tile(x1b, rwt_ref, rb_ref, tri_ref, base_ref, rf_ref, ri_ref, cnt_ref):
    tm = x1b.shape[0]
    scores = jax.nn.sigmoid(_dot_nt(rwt_ref[...], x1b))
    sel = scores + rb_ref[...]
    s = [sel[SUBLANES * j:SUBLANES * (j + 1)] for j in range(EXPERTS_PER_GROUP)]
    p = [scores[SUBLANES * j:SUBLANES * (j + 1)] for j in range(EXPERTS_PER_GROUP)]
    hi1, lo1 = jnp.maximum(s[0], s[1]), jnp.minimum(s[0], s[1])
    hi2, lo2 = jnp.maximum(s[2], s[3]), jnp.minimum(s[2], s[3])
    gs = jnp.maximum(hi1, hi2) + jnp.maximum(jnp.minimum(hi1, hi2), jnp.maximum(lo1, lo2))
    sub = lax.broadcasted_iota(I32, (N_GROUPS, tm), 0)
    gi = jnp.min(jnp.where(gs == jnp.max(gs, axis=0, keepdims=True), sub, N_GROUPS),
                 axis=0, keepdims=True)
    gm = sub == gi
    v = [jnp.sum(jnp.where(gm, sj, 0.0), axis=0, keepdims=True) for sj in s]
    pv = [jnp.sum(jnp.where(gm, pj, 0.0), axis=0, keepdims=True) for pj in p]
    best, i1 = v[0], jnp.zeros((1, tm), I32)
    for j in range(1, EXPERTS_PER_GROUP):
        u = v[j] > best
        best, i1 = jnp.where(u, v[j], best), jnp.where(u, j, i1)
    best2, i2 = jnp.full((1, tm), -jnp.inf, F32), jnp.zeros((1, tm), I32)
    for j in range(EXPERTS_PER_GROUP):
        cand = jnp.where(i1 == j, -jnp.inf, v[j])
        u = cand > best2
        best2, i2 = jnp.where(u, cand, best2), jnp.where(u, j, i2)

    def pick(idx):
        return jnp.where(idx == 0, pv[0], jnp.where(idx == 1, pv[1],
                                                    jnp.where(idx == 2, pv[2], pv[3])))

    p1, p2 = pick(i1), pick(i2)
    tot = p1 + p2
    first_lo = i1 < i2
    g_lo = jnp.where(first_lo, p1, p2) / tot
    g_hi = jnp.where(first_lo, p2, p1) / tot
    lo, hi = jnp.minimum(i1, i2), jnp.maximum(i1, i2)
    pair = jnp.where(hi == 1, 0, jnp.where(hi == 2, jnp.where(lo == 0, 1, 2),
                                           jnp.where(lo == 1, 3, jnp.where(lo == 2, 4, 5))))
    cls = gi * N_PAIRS + pair

    oh = lax.broadcasted_iota(I32, (CLASS_PAD, tm), 0) == cls
    pref = _dot(oh.astype(F32).astype(BF16), tri_ref[...])
    base = base_ref[...]
    rank = jnp.sum(jnp.where(oh, pref + base[:, 0:1] - 1.0, 0.0), axis=0, keepdims=True)
    base = base + pref[:, tm - 1:tm]
    base_ref[...] = base
    cnt_ref[...] = base

    sub8 = lax.broadcasted_iota(I32, (SUBLANES, tm), 0)
    rf_ref[...] = jnp.where(sub8 == 0, g_lo, jnp.where(sub8 == 1, g_hi, 0.0))
    ri_ref[...] = jnp.where(sub8 == 0, cls, jnp.where(sub8 == 1, rank.astype(I32), 0))


def _init_tail(tail_refs):
    base_ref = tail_refs[-1]

    @pl.when((pl.program_id(0) == 0) & (pl.program_id(1) == 0))
    def _():
        base_ref[...] = jnp.zeros_like(base_ref)


def _cross_attention(q_mem, tail_refs):
    kbd_ref, vbd_ref = tail_refs[:2]
    s = _dot(q_mem.astype(BF16), kbd_ref[0, 0])
    mem_len = s.shape[1] // XATT_HEADS
    sh = [s[:, h * mem_len:(h + 1) * mem_len] for h in range(XATT_HEADS)]
    mx = [jnp.max(t, axis=-1, keepdims=True) for t in sh]
    e = [jnp.exp(t - m) for t, m in zip(sh, mx)]
    tot = [jnp.sum(t, axis=-1, keepdims=True) for t in e]
    ps = [(t / n).astype(BF16) for t, n in zip(e, tot)]
    return _dot(jnp.concatenate(ps, axis=-1), vbd_ref[0, 0])


def _mixer_tail(x, h_loc, h_mem, tail_refs):
    (_, _, wout_ref, g_ref, b_ref, rwt_ref, rb_ref, tri_ref,
     x1_ref, rf_ref, ri_ref, cnt_ref, base_ref) = tail_refs
    hcat = jnp.concatenate([h_loc.astype(BF16), h_mem.astype(BF16)], axis=-1)
    y = _dot(hcat, wout_ref[...])
    x1 = _layer_norm(ALPHA * x + y, g_ref[...], b_ref[...])
    _store_rows(x1_ref, x1)
    _route_tile(x1.astype(BF16), rwt_ref, rb_ref, tri_ref, base_ref, rf_ref, ri_ref, cnt_ref)


def _conv_half_kernel(x_ref, win_ref, cw_ref, *refs):
    tail_refs, carry_ref = refs[:-1], refs[-1]
    _init_tail(tail_refs)

    @pl.when(pl.program_id(1) == 0)
    def _():
        carry_ref[...] = jnp.zeros_like(carry_ref)

    x = _load_x(x_ref)
    tm = x.shape[0]
    xb = x.astype(BF16)
    q_mem = _dot(xb, win_ref[:, 3 * MIX_W:])
    h_mem = _cross_attention(q_mem, tail_refs)
    proj = _dot(xb, win_ref[:, :3 * MIX_W])
    bg = proj[:, 0:MIX_W]
    cg = proj[:, MIX_W:2 * MIX_W]
    xin = proj[:, 2 * MIX_W:3 * MIX_W]
    u = cg * xin
    prev = carry_ref[...]
    row = lax.broadcasted_iota(I32, u.shape, 0)
    u1 = jnp.where(row == 0, prev[7:8], pltpu.roll(u, 1, 0))
    u2 = jnp.where(row == 0, prev[6:7],
                   jnp.where(row == 1, prev[7:8], pltpu.roll(u, 2, 0)))
    carry_ref[...] = u[tm - 8:tm]
    cw = cw_ref[...]
    c = cw[0:1] * u2 + cw[1:2] * u1 + cw[2:3] * u
    _mixer_tail(x, bg * c, h_mem, tail_refs)


def _log_sigmoid(z):
    return -(jnp.maximum(-z, 0.0) + jnp.log1p(jnp.exp(-jnp.abs(z))))


def _mlstm_half_kernel(x_ref, win_ref, gb_ref, ng_ref, *refs):
    tail_refs, (state_ref, m_ref, h_ref) = refs[:-3], refs[-3:]

    @pl.when(pl.program_id(1) == 0)
    def _():
        m_ref[...] = jnp.zeros_like(m_ref)

    x = _load_x(x_ref)
    tm = x.shape[0]
    n_cell = 3 * MIX_W + GATE_PAD
    xb = x.astype(BF16)
    proj = _dot(xb, win_ref[:, :n_cell])
    gates = proj[:, 3 * MIX_W:n_cell] + gb_ref[...]
    row = lax.broadcasted_iota(I32, (CHUNK, GATE_PAD), 0)
    head_lane = lax.broadcasted_iota(I32, (tm, GATE_PAD), 1) < ML_HEADS
    log_i = jnp.where(head_lane, gates, 0.0)
    log_f = jnp.where(head_lane, pltpu.roll(_log_sigmoid(gates), GATE_PAD - ML_HEADS, 1), 0.0)
    causal = (lax.broadcasted_iota(I32, (CHUNK, CHUNK), 0)
              >= lax.broadcasted_iota(I32, (CHUNK, CHUNK), 1))
    ones_col = (lax.broadcasted_iota(I32, (CHUNK, ML_DH), 1) == 0).astype(BF16)

    def scan_rows(a, op, fill):
        sh = 1
        while sh < CHUNK:
            a = op(a, jnp.where(row >= sh, pltpu.roll(a, sh, 0), fill))
            sh *= 2
        return a

    chunks, heads = range(tm // CHUNK), range(ML_HEADS)
    pairs = [(c, h) for c in chunks for h in heads]

    m_prev = m_ref[...]
    gate = []
    for c in chunks:
        li = log_i[c * CHUNK:(c + 1) * CHUNK]
        b = scan_rows(log_f[c * CHUNK:(c + 1) * CHUNK], jnp.add, 0.0)
        wd = li - b
        m_t = b + jnp.maximum(scan_rows(wd, jnp.maximum, -jnp.inf), m_prev)
        g = b[CHUNK - 1:CHUNK, :]
        a = g + wd
        m_new = jnp.maximum(g + m_prev, jnp.max(a, axis=0, keepdims=True))
        gate.append(dict(u=b - m_t, w_inter=jnp.exp(b + m_prev - m_t), clamp=jnp.exp(-m_t),
                         decay=jnp.exp(g + m_prev - m_new), wd_t=wd.T,
                         wn_t=jnp.exp(a - m_new).T))
        m_prev = m_new
    m_ref[...] = m_prev

    col = lambda c, h, o=0: proj[c * CHUNK:(c + 1) * CHUNK, o + h * ML_DH:o + (h + 1) * ML_DH]
    qb = {p: col(*p).astype(BF16) for p in pairs}
    kf = {p: col(*p, MIX_W) * (ML_DH ** -0.5) for p in pairs}
    kb = {p: kf[p].astype(BF16) for p in pairs}
    vb = {p: col(*p, 2 * MIX_W).astype(BF16) for p in pairs}
    qk = {p: _dot_nt(qb[p], kb[p]) for p in pairs}
    proj_tail = _dot(xb, win_ref[:, n_cell:])
    s = {(c, h): qk[c, h] * jnp.exp(jnp.where(
        causal, gate[c]["u"][:, h:h + 1] + gate[c]["wd_t"][h:h + 1, :], -jnp.inf))
        for c, h in pairs}
    sv = {p: _dot(s[p].astype(BF16), vb[p]) for p in pairs}
    s_sum = {p: jnp.sum(s[p], axis=-1, keepdims=True) for p in pairs}
    ktw = {(c, h): (kf[c, h].T * gate[c]["wn_t"][h:h + 1, :]).astype(BF16) for c, h in pairs}
    kv = {p: _dot(ktw[p], jnp.concatenate([vb[p], ones_col], axis=-1)) for p in pairs}
    @pl.when(pl.program_id(1) == 0)
    def _():
        state_ref[...] = jnp.zeros_like(state_ref)

    state = [state_ref[h] for h in heads]
    qs = {}
    for c in chunks:
        for h in heads:
            qs[c, h] = _dot(qb[c, h], state[h].astype(BF16))
        state = [gate[c]["decay"][:, h:h + 1] * state[h] + kv[c, h] for h in heads]
    for h in heads:
        state_ref[h] = state[h]
    wi = {(c, h): gate[c]["w_inter"][:, h:h + 1] for c, h in pairs}
    den = {p: s_sum[p] + wi[p] * qs[p][:, ML_DH:ML_DH + 1] for p in pairs}
    hh = {(c, h): (sv[c, h] + wi[c, h] * qs[c, h][:, 0:ML_DH])
          / jnp.maximum(jnp.abs(den[c, h]), gate[c]["clamp"][:, h:h + 1]) for c, h in pairs}
    hc = {p: hh[p] - jnp.mean(hh[p], axis=-1, keepdims=True) for p in pairs}
    var = {p: jnp.mean(hc[p] * hc[p], axis=-1, keepdims=True) for p in pairs}
    for c, h in pairs:
        h_ref[c * CHUNK:(c + 1) * CHUNK, h * ML_DH:(h + 1) * ML_DH] = (
            hc[c, h] * lax.rsqrt(var[c, h] + LN_EPS))

    h_loc = jax.nn.sigmoid(proj_tail[:, 0:MIX_W]) * (h_ref[...] * ng_ref[...])
    _init_tail(tail_refs)
    h_mem = _cross_attention(proj_tail[:, MIX_W:], tail_refs)
    _mixer_tail(x, h_loc, h_mem, tail_refs)


def _mixer_half(kernel_fn, tm, x, nb, seq, mixer_args, layer, kbd, vbd, wout, ln_g, ln_b, rwt, rb,
                scratch, name):
    d = D_MODEL
    n_tok = nb * seq
    spb = seq // tm
    const2 = lambda b, s: (0, 0)
    tri = jnp.triu(jnp.ones((tm, tm), BF16))
    tail_args = [kbd, vbd, wout, ln_g.reshape(1, d), ln_b.reshape(1, d), rwt, rb, tri]
    row_map = lambda b, s: (b * spb + s, 0)
    x_spec = (pl.BlockSpec((tm, d), row_map) if x.shape[1] == d
              else pl.BlockSpec((tm * ROW_TILES, LANES), row_map))
    in_specs = ([x_spec]
                + [pl.BlockSpec(a.shape, const2) for a in mixer_args]
                + [pl.BlockSpec((1, 1) + kbd.shape[2:], lambda b, s: (layer, b, 0, 0)),
                   pl.BlockSpec((1, 1) + vbd.shape[2:], lambda b, s: (layer, b, 0, 0))]
                + [pl.BlockSpec(a.shape, const2) for a in tail_args[2:]])
    tok_spec = lambda rows: pl.BlockSpec((rows, tm), lambda b, s: (0, b * spb + s))
    return pl.pallas_call(
        kernel_fn,
        grid=(nb, spb),
        in_specs=in_specs,
        out_specs=[pl.BlockSpec((tm * ROW_TILES, LANES), row_map),
                   tok_spec(SUBLANES), tok_spec(SUBLANES),
                   pl.BlockSpec((CLASS_PAD, LANES), const2)],
        out_shape=[jax.ShapeDtypeStruct((n_tok * ROW_TILES, LANES), F32),
                   jax.ShapeDtypeStruct((SUBLANES, n_tok), F32),
                   jax.ShapeDtypeStruct((SUBLANES, n_tok), I32),
                   jax.ShapeDtypeStruct((CLASS_PAD, LANES), F32)],
        scratch_shapes=[pltpu.VMEM((CLASS_PAD, LANES), F32)] + scratch,
        compiler_params=pltpu.CompilerParams(
            dimension_semantics=("arbitrary", "arbitrary"), vmem_limit_bytes=VMEM_LIMIT),
        name=name,
    )(x, *mixer_args, *tail_args)


def _moe_kernel(layer, n_blocks,
                order_ref, off_ref, nvalid_ref, trash_ref, e1_ref, e2_ref, nblk_ref,
                x_hbm, g_hbm, wg_hbm, wu_hbm, wd_hbm, lng_ref, lnb_ref, out_hbm,
                xbuf, gbuf, obuf, zbuf, sg, su, sd, wg_bf, wu_bf, wd_bf,
                gsem, ssem, zsem, wsem):
    nblk = nblk_ref[0]
    last = nblk - 1
    ring = lambda j: (j + RING) % RING
    n_res = W_SETS * EXPERTS_PER_GROUP

    def gate_copy(blk, slt):
        return pltpu.make_async_copy(g_hbm.at[pl.ds(off_ref[blk + 1], MOE_ROWS)], gbuf.at[slt],
                                     gsem.at[slt])

    def tile_row(r):
        return r * ROW_TILES if isinstance(r, int) else pl.multiple_of(r * ROW_TILES, ROW_TILES)

    def gather_row(base, r, slt):
        t8 = pl.multiple_of(order_ref[base + r], ROW_TILES)
        pltpu.make_async_copy(x_hbm.at[pl.ds(t8, ROW_TILES)],
                              xbuf.at[slt, pl.ds(tile_row(r), ROW_TILES)], gsem.at[slt]).start()

    def issue_gather(blk, slt, unrolled=True):
        base = off_ref[blk + 1]
        if unrolled:
            for r in range(MOE_ROWS):
                gather_row(base, r, slt)
        else:
            lax.fori_loop(0, MOE_ROWS, lambda r, c: (gather_row(base, r, slt), c)[1], 0)
        gate_copy(blk, slt).start()

    def wait_gather(slt):
        pltpu.make_async_copy(x_hbm.at[pl.ds(0, MOE_ROWS * ROW_TILES)], xbuf.at[slt],
                              gsem.at[slt]).wait()
        gate_copy(0, slt).wait()

    def scatter_row(base, nvalid, trash8, r, slt):
        t8 = jnp.where(r < nvalid, order_ref[base + r], trash8 + r * ROW_TILES)
        pltpu.make_async_copy(
            obuf.at[slt, pl.ds(tile_row(r), ROW_TILES)],
            out_hbm.at[pl.ds(pl.multiple_of(t8, ROW_TILES), ROW_TILES)], ssem.at[slt]).start()

    def issue_scatter(blk, slt, unrolled=True):
        args = (off_ref[blk + 1], nvalid_ref[blk + 1], trash_ref[blk + 1])
        if unrolled:
            for r in range(MOE_ROWS):
                scatter_row(*args, r, slt)
        else:
            lax.fori_loop(0, MOE_ROWS, lambda r, c: (scatter_row(*args, r, slt), c)[1], 0)

    def wait_scatter(slt):
        pltpu.make_async_copy(obuf.at[slt], out_hbm.at[pl.ds(0, MOE_ROWS * ROW_TILES)],
                              ssem.at[slt]).wait()

    def weight_copies(k, stage):
        return [pltpu.make_async_copy(src.at[layer, k], dst.at[stage], wsem.at[stage])
                for src, dst in ((wg_hbm, sg), (wu_hbm, su), (wd_hbm, sd))]

    def load_start(k):
        for cp in weight_copies(k, k % W_STAGE):
            cp.start(priority=1)

    def load_wait(k):
        for cp in weight_copies(k, k % W_STAGE):
            cp.wait()

    def load_finish(k):
        load_wait(k)
        stage, res = k % W_STAGE, k % n_res
        wg_bf[res] = sg[stage].astype(BF16)
        wu_bf[res] = su[stage].astype(BF16)
        wd_bf[res] = sd[stage].astype(BF16)

    def start_if(pred, kd):
        @pl.when(pred)
        def _():
            load_start(kd)
        return kd + pred.astype(I32)

    def service_loader(group, kd, kc):
        cap = jnp.minimum((group + W_SETS) * EXPERTS_PER_GROUP, N_EXPERTS)
        need = (group + 1) * EXPERTS_PER_GROUP

        def catch_up(c):
            kd, kc = c
            kd = start_if(kd <= kc, kd)
            kd = start_if((kd < kc + W_STAGE) & (kd < cap), kd)
            load_finish(kc)
            return kd, kc + 1
        kd, kc = lax.while_loop(lambda c: c[1] < need, catch_up, (kd, kc))

        ready = (kc < kd) & ((kd == kc + W_STAGE) | (kd >= cap))

        @pl.when(ready)
        def _():
            load_finish(kc)
        kc = kc + ready.astype(I32)
        kd = start_if((kd < kc + W_STAGE) & (kd < cap), kd)
        return kd, kc

    issue_gather(0, 0, unrolled=False)
    issue_gather(jnp.minimum(1, last), 1, unrolled=False)
    obuf[RING - 1] = jnp.zeros(obuf.shape[1:], F32)

    def block(b, loader):
        slot = b % RING
        e1, e2 = e1_ref[b], e2_ref[b]
        loader = service_loader(e1 // EXPERTS_PER_GROUP, *loader)

        @pl.when(b >= RING - 1)
        def _():
            wait_scatter(slot)

        wait_gather(slot)
        x = _load_rows(xbuf.at[slot], MOE_ROWS)
        xb = x.astype(BF16)
        issue_gather(jnp.minimum(b + 2, last), ring(b + 2))
        issue_scatter(b - 1, ring(b - 1))
        res = (e1 % n_res, e2 % n_res)
        hg = [_dot(xb, wg_bf[r]) for r in res]
        hu = [_dot(xb, wu_bf[r]) for r in res]
        act = [(g * jax.nn.sigmoid(g) * u).astype(BF16) for g, u in zip(hg, hu)]
        y = [_dot(a, wd_bf[r]) for a, r in zip(act, res)]
        gates = gbuf[slot]
        m = gates[:, 0:1] * y[0] + gates[:, 1:2] * y[1]
        o = _layer_norm(ALPHA * x + m, lng_ref[...], lnb_ref[...])
        _store_rows(obuf.at[slot], o)
        return loader

    first = (e1_ref[0] // EXPERTS_PER_GROUP) * EXPERTS_PER_GROUP
    kd, kc = lax.fori_loop(0, nblk, block, (first, first))

    lax.fori_loop(kc, kd, lambda k, c: (load_wait(k), c)[1], 0)
    issue_scatter(last, ring(last), unrolled=False)
    for slt in range(RING):
        wait_scatter(slt)
    wait_gather(ring(nblk))
    wait_gather(ring(nblk + 1))

    zbuf[...] = jnp.zeros_like(zbuf)

    def fill_copy(j):
        return pltpu.make_async_copy(
            zbuf, out_hbm.at[pl.ds(j * MOE_ROWS * ROW_TILES, MOE_ROWS * ROW_TILES)], zsem)
    lax.fori_loop(nblk, n_blocks, lambda j, c: (fill_copy(j).start(), c)[1], 0)
    lax.fori_loop(nblk, n_blocks, lambda j, c: (fill_copy(j).wait(), c)[1], 0)


def _moe_half(layer, x1, plan, gates_sorted, w_gate, w_up, w_down, ln_g, ln_b):
    n_blocks = plan[4].shape[0]
    n_rows = (n_blocks + 1) * MOE_ROWS
    d, f = D_MODEL, D_FF
    n_res = W_SETS * EXPERTS_PER_GROUP
    const_spec = pl.BlockSpec((1, d), lambda i, *_: (0, 0))
    any_spec = pl.BlockSpec(memory_space=pl.ANY)
    grid_spec = pltpu.PrefetchScalarGridSpec(
        num_scalar_prefetch=7,
        grid=(1,),
        in_specs=[any_spec, any_spec, any_spec, any_spec, any_spec, const_spec, const_spec],
        out_specs=any_spec,
        scratch_shapes=[
            pltpu.VMEM((RING, MOE_ROWS * ROW_TILES, LANES), F32),
            pltpu.VMEM((RING, MOE_ROWS, LANES), F32),
            pltpu.VMEM((RING, MOE_ROWS * ROW_TILES, LANES), F32),
            pltpu.VMEM((MOE_ROWS * ROW_TILES, LANES), F32),
            pltpu.VMEM((W_STAGE, d, f), F32), pltpu.VMEM((W_STAGE, d, f), F32),
            pltpu.VMEM((W_STAGE, f, d), F32),
            pltpu.VMEM((n_res, d, f), BF16), pltpu.VMEM((n_res, d, f), BF16),
            pltpu.VMEM((n_res, f, d), BF16),
            pltpu.SemaphoreType.DMA((RING,)),
            pltpu.SemaphoreType.DMA((RING,)),
            pltpu.SemaphoreType.DMA(()),
            pltpu.SemaphoreType.DMA((W_STAGE,)),
        ],
    )
    return pl.pallas_call(
        functools.partial(_moe_kernel, layer, n_blocks),
        grid_spec=grid_spec,
        out_shape=jax.ShapeDtypeStruct((n_rows * ROW_TILES, LANES), F32),
        compiler_params=pltpu.CompilerParams(
            dimension_semantics=("arbitrary",), vmem_limit_bytes=VMEM_LIMIT),
        name="moe_half",
    )(*plan, x1, gates_sorted, w_gate, w_up, w_down, ln_g.reshape(1, d), ln_b.reshape(1, d))


def _plan_blocks(route_f, route_i, counts_f, n_tok):
    cls, rank = route_i[0], route_i[1]
    counts = counts_f[:N_CLASSES, 0].astype(I32)
    padded = (counts + MOE_ROWS - 1) // MOE_ROWS * MOE_ROWS
    pad_end = jnp.cumsum(padded)
    pad_start = pad_end - padded
    start = jnp.cumsum(counts) - counts
    class_ids = jnp.arange(N_CLASSES, dtype=I32)
    dest = jnp.sum(jnp.where(cls[:, None] == class_ids[None], pad_start[None], 0), axis=1) + rank
    _, order, g_lo, g_hi = lax.sort(
        (dest, jnp.arange(n_tok, dtype=I32), route_f[0], route_f[1]), num_keys=1)
    order = jnp.pad(order, (0, MOE_ROWS)) * ROW_TILES
    gates_sorted = jnp.pad(jnp.stack([g_lo, g_hi], axis=1), ((0, MOE_ROWS), (0, LANES - 2)))

    n_blocks = n_tok // MOE_ROWS + N_CLASSES
    blk_ids = jnp.arange(n_blocks, dtype=I32)
    blk_start = blk_ids * MOE_ROWS
    nblk = pad_end[-1] // MOE_ROWS
    blk_cls = jnp.minimum(jnp.sum(pad_end[None] <= blk_start[:, None], axis=1), N_CLASSES - 1)
    blk_cls = jnp.where(blk_ids < nblk, blk_cls, blk_cls[jnp.maximum(nblk - 1, 0)]).astype(I32)
    in_cls = blk_start - pad_start[blk_cls]
    active = blk_ids < nblk
    blk_off = jnp.where(active, start[blk_cls] + in_cls, 0).astype(I32)
    nvalid = jnp.where(active, jnp.clip(counts[blk_cls] - in_cls, 0, MOE_ROWS), 0).astype(I32)
    trash = ((n_tok + blk_start - blk_off - nvalid) * ROW_TILES).astype(I32)
    pair_lo = jnp.array(PAIR_LO, I32)
    pair_hi = jnp.array(PAIR_HI, I32)
    blk_e1 = (blk_cls // N_PAIRS) * EXPERTS_PER_GROUP + pair_lo[blk_cls % N_PAIRS]
    blk_e2 = (blk_cls // N_PAIRS) * EXPERTS_PER_GROUP + pair_hi[blk_cls % N_PAIRS]
    lead = lambda a, v: jnp.concatenate([jnp.full((1,), v, I32), a])
    plan = (order, lead(blk_off, 0), lead(nvalid, 0),
            lead(trash, n_blocks * MOE_ROWS * ROW_TILES),
            blk_e1, blk_e2, nblk.astype(I32).reshape(1))
    return plan, gates_sorted


def _untile_kernel(x_ref, o_ref):
    o_ref[...] = _load_rows(x_ref, o_ref.shape[0])


def _token_rows_to_matrix(xf, n_tok, tm):
    return pl.pallas_call(
        _untile_kernel,
        grid=(n_tok // tm,),
        in_specs=[pl.BlockSpec((tm * ROW_TILES, LANES), lambda i: (i, 0))],
        out_specs=pl.BlockSpec((tm, D_MODEL), lambda i: (i, 0)),
        out_shape=jax.ShapeDtypeStruct((n_tok, D_MODEL), F32),
        compiler_params=pltpu.CompilerParams(
            dimension_semantics=("arbitrary",), vmem_limit_bytes=VMEM_LIMIT),
        name="token_rows_to_matrix",
    )(xf)


def kernel(x, mem, conv_w_in, conv_w, mlstm_w_in, mlstm_gate_b, mlstm_norm_g, w_kv_mem, w_out,
           ln1_g, ln1_b, ln2_g, ln2_b, router_w, router_b, w_gate, w_up, w_down):
    nb, seq, d = x.shape
    n_tok = nb * seq
    assert d == D_MODEL and seq % TM_CONV == 0 and seq % TM_MLSTM == 0 and TM_MLSTM % CHUNK == 0
    assert n_tok // MOE_ROWS >= RING
    assert n_tok % TM_UNTILE == 0

    kbd, vbd = _memory_kv(mem, w_kv_mem)

    perm = lambda a: jnp.swapaxes(a.reshape(N_GROUPS, EXPERTS_PER_GROUP, -1), 0, 1)
    rwt = perm(router_w.T).reshape(N_EXPERTS, d).astype(BF16)
    rb = perm(router_b).reshape(N_EXPERTS, 1)

    xf = x.reshape(n_tok, d)
    for i in range(DEPTH):
        j = i // 2
        wout = w_out[i].astype(BF16)
        if i % 2 == 0:
            win = conv_w_in[j].astype(BF16)
            outs = _mixer_half(_conv_half_kernel, TM_CONV, xf, nb, seq, [win, conv_w[j]],
                               i, kbd, vbd, wout, ln1_g[i], ln1_b[i], rwt, rb,
                               [pltpu.VMEM((8, MIX_W), F32)], "conv_half")
        else:
            w = mlstm_w_in[j]
            o0 = 4 * MIX_W
            w_gates = jnp.pad(w[:, o0:o0 + 2 * ML_HEADS], ((0, 0), (0, GATE_PAD - 2 * ML_HEADS)))
            win = jnp.concatenate([w[:, :3 * MIX_W], w_gates, w[:, 3 * MIX_W:o0],
                                   w[:, o0 + 2 * ML_HEADS:]], axis=1).astype(BF16)
            gate_b = jnp.pad(mlstm_gate_b[j], (0, GATE_PAD - 2 * ML_HEADS)).reshape(1, GATE_PAD)
            norm_g = mlstm_norm_g[j].reshape(1, MIX_W)
            outs = _mixer_half(_mlstm_half_kernel, TM_MLSTM, xf, nb, seq, [win, gate_b, norm_g],
                               i, kbd, vbd, wout, ln1_g[i], ln1_b[i], rwt, rb,
                               [pltpu.VMEM((ML_HEADS, ML_DH, 2 * ML_DH), F32),
                                pltpu.VMEM((1, LANES), F32),
                                pltpu.VMEM((TM_MLSTM, MIX_W), F32)], "mlstm_half")
        x1, route_f, route_i, counts_f = outs
        plan, gates_sorted = _plan_blocks(route_f, route_i, counts_f, n_tok)
        xf = _moe_half(i, x1, plan, gates_sorted, w_gate, w_up, w_down, ln2_g[i], ln2_b[i])
    return _token_rows_to_matrix(xf, n_tok, TM_UNTILE).reshape(nb, seq, d)
```

```python
import functools

import jax
import jax.numpy as jnp
from jax import lax
from jax.experimental import pallas as pl
from jax.experimental.pallas import tpu as pltpu

F32 = jnp.float32
BF16 = jnp.bfloat16
I32 = jnp.int32

D_MODEL = 1024
DEPTH = 4
MIX_W = 768
XATT_HEADS = 4
XATT_DH = 64
XATT_W = XATT_HEADS * XATT_DH
CONV_K = 3
ML_HEADS = 6
ML_DH = MIX_W // ML_HEADS
CHUNK = 128
N_EXPERTS = 32
N_GROUPS = 8
EXPERTS_PER_GROUP = N_EXPERTS // N_GROUPS
D_FF = 512
ALPHA = (2.0 * DEPTH) ** 0.25
LN_EPS = 1e-5

LANES = 128
SUBLANES = 8
ROW_TILES = D_MODEL // LANES
GATE_PAD = LANES
MOE_ROWS = 128
RING = 3
W_SETS = 2
W_STAGE = 3
PAIR_LO = (0, 0, 1, 1, 2, 0)
PAIR_HI = (1, 2, 2, 3, 3, 3)
N_PAIRS = len(PAIR_LO)
N_CLASSES = N_GROUPS * N_PAIRS
CLASS_PAD = 64
VMEM_LIMIT = 52 * 1024 * 1024

TM_CONV = 1024
TM_MLSTM = 512
TM_UNTILE = 1024


def _layer_norm(z, g, b):
    mu = jnp.mean(z, axis=-1, keepdims=True)
    zc = z - mu
    var = jnp.mean(zc * zc, axis=-1, keepdims=True)
    return zc * lax.rsqrt(var + LN_EPS) * g + b


def _load_rows(ref, n):
    return jnp.concatenate([ref[pl.ds(j, n, stride=ROW_TILES), :] for j in range(ROW_TILES)],
                           axis=-1)


def _load_x(x_ref):
    if x_ref.shape[1] == D_MODEL:
        return x_ref[...]
    return _load_rows(x_ref, x_ref.shape[0] // ROW_TILES)


def _store_rows(ref, val):
    for j in range(ROW_TILES):
        ref[pl.ds(j, val.shape[0], stride=ROW_TILES), :] = val[:, j * LANES:(j + 1) * LANES]


def _dot(a, b):
    return jnp.dot(a, b, preferred_element_type=F32)


def _dot_nt(a, b):
    return lax.dot_general(a, b, (((1,), (1,)), ((), ())), preferred_element_type=F32)


def _kv_kernel(mem_ref, w_ref, kbd_ref, vbd_ref):
    nb, mem_len = mem_ref.shape[0], mem_ref.shape[1]
    w = w_ref[0].astype(BF16)
    row_head = lax.broadcasted_iota(I32, (XATT_W, mem_len), 0) // XATT_DH
    col_head = lax.broadcasted_iota(I32, (mem_len, XATT_W), 1) // XATT_DH
    for b in range(nb):
        kv = _dot(mem_ref[b].astype(BF16), w)
        kt = (kv[:, :XATT_W] * (XATT_DH ** -0.5)).T
        v = kv[:, XATT_W:]
        for h in range(XATT_HEADS):
            kbd_ref[0, b, :, h * mem_len:(h + 1) * mem_len] = (
                jnp.where(row_head == h, kt, 0.0).astype(BF16))
            vbd_ref[0, b, h * mem_len:(h + 1) * mem_len, :] = (
                jnp.where(col_head == h, v, 0.0).astype(BF16))


def _memory_kv(mem, w_kv_mem):
    nb, mem_len, d = mem.shape
    depth = w_kv_mem.shape[0]
    return pl.pallas_call(
        _kv_kernel,
        grid=(depth,),
        in_specs=[pl.BlockSpec((nb, mem_len, d), lambda i: (0, 0, 0)),
                  pl.BlockSpec((1, d, 2 * XATT_W), lambda i: (i, 0, 0))],
        out_specs=[pl.BlockSpec((1, nb, XATT_W, XATT_HEADS * mem_len), lambda i: (i, 0, 0, 0)),
                   pl.BlockSpec((1, nb, XATT_HEADS * mem_len, XATT_W), lambda i: (i, 0, 0, 0))],
        out_shape=[jax.ShapeDtypeStruct((depth, nb, XATT_W, XATT_HEADS * mem_len), BF16),
                   jax.ShapeDtypeStruct((depth, nb, XATT_HEADS * mem_len, XATT_W), BF16)],
        compiler_params=pltpu.CompilerParams(
            dimension_semantics=("arbitrary",), vmem_limit_bytes=VMEM_LIMIT),
        name="memory_kv",
    )(mem, w_kv_mem)


def _route_tile(x1b, rwt_ref, rb_ref, tri_ref, base_ref, rf_ref, ri_ref, cnt_ref):
    tm = x1b.shape[0]
    scores = jax.nn.sigmoid(_dot_nt(rwt_ref[...], x1b))
    sel = scores + rb_ref[...]
    s = [sel[SUBLANES * j:SUBLANES * (j + 1)] for j in range(EXPERTS_PER_GROUP)]
    p = [scores[SUBLANES * j:SUBLANES * (j + 1)] for j in range(EXPERTS_PER_GROUP)]
    hi1, lo1 = jnp.maximum(s[0], s[1]), jnp.minimum(s[0], s[1])
    hi2, lo2 = jnp.maximum(s[2], s[3]), jnp.minimum(s[2], s[3])
    gs = jnp.maximum(hi1, hi2) + jnp.maximum(jnp.minimum(hi1, hi2), jnp.maximum(lo1, lo2))
    sub = lax.broadcasted_iota(I32, (N_GROUPS, tm), 0)
    gi = jnp.min(jnp.where(gs == jnp.max(gs, axis=0, keepdims=True), sub, N_GROUPS),
                 axis=0, keepdims=True)
    gm = sub == gi
    v = [jnp.sum(jnp.where(gm, sj, 0.0), axis=0, keepdims=True) for sj in s]
    pv = [jnp.sum(jnp.where(gm, pj, 0.0), axis=0, keepdims=True) for pj in p]
    best, i1 = v[0], jnp.zeros((1, tm), I32)
    for j in range(1, EXPERTS_PER_GROUP):
        u = v[j] > best
        best, i1 = jnp.where(u, v[j], best), jnp.where(u, j, i1)
    best2, i2 = jnp.full((1, tm), -jnp.inf, F32), jnp.zeros((1, tm), I32)
    for j in range(EXPERTS_PER_GROUP):
        cand = jnp.where(i1 == j, -jnp.inf, v[j])
        u = cand > best2
        best2, i2 = jnp.where(u, cand, best2), jnp.where(u, j, i2)

    def pick(idx):
        return jnp.where(idx == 0, pv[0], jnp.where(idx == 1, pv[1],
                                                    jnp.where(idx == 2, pv[2], pv[3])))

    p1, p2 = pick(i1), pick(i2)
    tot = p1 + p2
    first_lo = i1 < i2
    g_lo = jnp.where(first_lo, p1, p2) / tot
    g_hi = jnp.where(first_lo, p2, p1) / tot
    lo, hi = jnp.minimum(i1, i2), jnp.maximum(i1, i2)
    pair = jnp.where(hi == 1, 0, jnp.where(hi == 2, jnp.where(lo == 0, 1, 2),
                                           jnp.where(lo == 1, 3, jnp.where(lo == 2, 4, 5))))
    cls = gi * N_PAIRS + pair

    oh = lax.broadcasted_iota(I32, (CLASS_PAD, tm), 0) == cls
    pref = _dot(oh.astype(F32).astype(BF16), tri_ref[...])
    base = base_ref[...]
    rank = jnp.sum(jnp.where(oh, pref + base[:, 0:1] - 1.0, 0.0), axis=0, keepdims=True)
    base = base + pref[:, tm - 1:tm]
    base_ref[...] = base
    cnt_ref[...] = base

    sub8 = lax.broadcasted_iota(I32, (SUBLANES, tm), 0)
    rf_ref[...] = jnp.where(sub8 == 0, g_lo, jnp.where(sub8 == 1, g_hi, 0.0))
    ri_ref[...] = jnp.where(sub8 == 0, cls, jnp.where(sub8 == 1, rank.astype(I32), 0))


def _init_tail(tail_refs):
    base_ref = tail_refs[-1]

    @pl.when((pl.program_id(0) == 0) & (pl.program_id(1) == 0))
    def _():
        base_ref[...] = jnp.zeros_like(base_ref)


def _cross_attention(q_mem, tail_refs):
    kbd_ref, vbd_ref = tail_refs[:2]
    s = _dot(q_mem.astype(BF16), kbd_ref[0, 0])
    mem_len = s.shape[1] // XATT_HEADS
    sh = [s[:, h * mem_len:(h + 1) * mem_len] for h in range(XATT_HEADS)]
    mx = [jnp.max(t, axis=-1, keepdims=True) for t in sh]
    e = [jnp.exp(t - m) for t, m in zip(sh, mx)]
    tot = [jnp.sum(t, axis=-1, keepdims=True) for t in e]
    ps = [(t / n).astype(BF16) for t, n in zip(e, tot)]
    return _dot(jnp.concatenate(ps, axis=-1), vbd_ref[0, 0])


def _mixer_tail(x, h_loc, h_mem, tail_refs):
    (_, _, wout_ref, g_ref, b_ref, rwt_ref, rb_ref, tri_ref,
     x1_ref, rf_ref, ri_ref, cnt_ref, base_ref) = tail_refs
    hcat = jnp.concatenate([h_loc.astype(BF16), h_mem.astype(BF16)], axis=-1)
    y = _dot(hcat, wout_ref[...])
    x1 = _layer_norm(ALPHA * x + y, g_ref[...], b_ref[...])
    _store_rows(x1_ref, x1)
    _route_tile(x1.astype(BF16), rwt_ref, rb_ref, tri_ref, base_ref, rf_ref, ri_ref, cnt_ref)


def _conv_half_kernel(x_ref, win_ref, cw_ref, *refs):
    tail_refs, carry_ref = refs[:-1], refs[-1]
    _init_tail(tail_refs)

    @pl.when(pl.program_id(1) == 0)
    def _():
        carry_ref[...] = jnp.zeros_like(carry_ref)

    x = _load_x(x_ref)
    tm = x.shape[0]
    xb = x.astype(BF16)
    q_mem = _dot(xb, win_ref[:, 3 * MIX_W:])
    h_mem = _cross_attention(q_mem, tail_refs)
    proj = _dot(xb, win_ref[:, :3 * MIX_W])
    bg = proj[:, 0:MIX_W]
    cg = proj[:, MIX_W:2 * MIX_W]
    xin = proj[:, 2 * MIX_W:3 * MIX_W]
    u = cg * xin
    prev = carry_ref[...]
    row = lax.broadcasted_iota(I32, u.shape, 0)
    u1 = jnp.where(row == 0, prev[7:8], pltpu.roll(u, 1, 0))
    u2 = jnp.where(row == 0, prev[6:7],
                   jnp.where(row == 1, prev[7:8], pltpu.roll(u, 2, 0)))
    carry_ref[...] = u[tm - 8:tm]
    cw = cw_ref[...]
    c = cw[0:1] * u2 + cw[1:2] * u1 + cw[2:3] * u
    _mixer_tail(x, bg * c, h_mem, tail_refs)


def _log_sigmoid(z):
    return -(jnp.maximum(-z, 0.0) + jnp.log1p(jnp.exp(-jnp.abs(z))))


def _mlstm_half_kernel(x_ref, win_ref, gb_ref, ng_ref, *refs):
    tail_refs, (state_ref, m_ref, h_ref) = refs[:-3], refs[-3:]

    @pl.when(pl.program_id(1) == 0)
    def _():
        m_ref[...] = jnp.zeros_like(m_ref)

    x = _load_x(x_ref)
    tm = x.shape[0]
    n_cell = 3 * MIX_W + GATE_PAD
    xb = x.astype(BF16)
    proj = _dot(xb, win_ref[:, :n_cell])
    gates = proj[:, 3 * MIX_W:n_cell] + gb_ref[...]
    row = lax.broadcasted_iota(I32, (CHUNK, GATE_PAD), 0)
    head_lane = lax.broadcasted_iota(I32, (tm, GATE_PAD), 1) < ML_HEADS
    log_i = jnp.where(head_lane, gates, 0.0)
    log_f = jnp.where(head_lane, pltpu.roll(_log_sigmoid(gates), GATE_PAD - ML_HEADS, 1), 0.0)
    causal = (lax.broadcasted_iota(I32, (CHUNK, CHUNK), 0)
              >= lax.broadcasted_iota(I32, (CHUNK, CHUNK), 1))
    ones_col = (lax.broadcasted_iota(I32, (CHUNK, ML_DH), 1) == 0).astype(BF16)

    def scan_rows(a, op, fill):
        sh = 1
        while sh < CHUNK:
            a = op(a, jnp.where(row >= sh, pltpu.roll(a, sh, 0), fill))
            sh *= 2
        return a

    chunks, heads = range(tm // CHUNK), range(ML_HEADS)
    pairs = [(c, h) for c in chunks for h in heads]

    m_prev = m_ref[...]
    gate = []
    for c in chunks:
        li = log_i[c * CHUNK:(c + 1) * CHUNK]
        b = scan_rows(log_f[c * CHUNK:(c + 1) * CHUNK], jnp.add, 0.0)
        wd = li - b
        m_t = b + jnp.maximum(scan_rows(wd, jnp.maximum, -jnp.inf), m_prev)
        g = b[CHUNK - 1:CHUNK, :]
        a = g + wd
        m_new = jnp.maximum(g + m_prev, jnp.max(a, axis=0, keepdims=True))
        gate.append(dict(u=b - m_t, w_inter=jnp.exp(b + m_prev - m_t), clamp=jnp.exp(-m_t),
                         decay=jnp.exp(g + m_prev - m_new), wd_t=wd.T,
                         wn_t=jnp.exp(a - m_new).T))
        m_prev = m_new
    m_ref[...] = m_prev

    col = lambda c, h, o=0: proj[c * CHUNK:(c + 1) * CHUNK, o + h * ML_DH:o + (h + 1) * ML_DH]
    qb = {p: col(*p).astype(BF16) for p in pairs}
    kf = {p: col(*p, MIX_W) * (ML_DH ** -0.5) for p in pairs}
    kb = {p: kf[p].astype(BF16) for p in pairs}
    vb = {p: col(*p, 2 * MIX_W).astype(BF16) for p in pairs}
    qk = {p: _dot_nt(qb[p], kb[p]) for p in pairs}
    proj_tail = _dot(xb, win_ref[:, n_cell:])
    s = {(c, h): qk[c, h] * jnp.exp(jnp.where(
        causal, gate[c]["u"][:, h:h + 1] + gate[c]["wd_t"][h:h + 1, :], -jnp.inf))
        for c, h in pairs}
    sv = {p: _dot(s[p].astype(BF16), vb[p]) for p in pairs}
    s_sum = {p: jnp.sum(s[p], axis=-1, keepdims=True) for p in pairs}
    ktw = {(c, h): (kf[c, h].T * gate[c]["wn_t"][h:h + 1, :]).astype(BF16) for c, h in pairs}
    kv = {p: _dot(ktw[p], jnp.concatenate([vb[p], ones_col], axis=-1)) for p in pairs}
    @pl.when(pl.program_id(1) == 0)
    def _():
        state_ref[...] = jnp.zeros_like(state_ref)

    state = [state_ref[h] for h in heads]
    qs = {}
    for c in chunks:
        for h in heads:
            qs[c, h] = _dot(qb[c, h], state[h].astype(BF16))
        state = [gate[c]["decay"][:, h:h + 1] * state[h] + kv[c, h] for h in heads]
    for h in heads:
        state_ref[h] = state[h]
    wi = {(c, h): gate[c]["w_inter"][:, h:h + 1] for c, h in pairs}
    den = {p: s_sum[p] + wi[p] * qs[p][:, ML_DH:ML_DH + 1] for p in pairs}
    hh = {(c, h): (sv[c, h] + wi[c, h] * qs[c, h][:, 0:ML_DH])
          / jnp.maximum(jnp.abs(den[c, h]), gate[c]["clamp"][:, h:h + 1]) for c, h in pairs}
    hc = {p: hh[p] - jnp.mean(hh[p], axis=-1, keepdims=True) for p in pairs}
    var = {p: jnp.mean(hc[p] * hc[p], axis=-1, keepdims=True) for p in pairs}
    for c, h in pairs:
        h_ref[c * CHUNK:(c + 1) * CHUNK, h * ML_DH:(h + 1) * ML_DH] = (
            hc[c, h] * lax.rsqrt(var[c, h] + LN_EPS))

    h_loc = jax.nn.sigmoid(proj_tail[:, 0:MIX_W]) * (h_ref[...] * ng_ref[...])
    _init_tail(tail_refs)
    h_mem = _cross_attention(proj_tail[:, MIX_W:], tail_refs)
    _mixer_tail(x, h_loc, h_mem, tail_refs)


def _mixer_half(kernel_fn, tm, x, nb, seq, mixer_args, layer, kbd, vbd, wout, ln_g, ln_b, rwt, rb,
                scratch, name):
    d = D_MODEL
    n_tok = nb * seq
    spb = seq // tm
    const2 = lambda b, s: (0, 0)
    tri = jnp.triu(jnp.ones((tm, tm), BF16))
    tail_args = [kbd, vbd, wout, ln_g.reshape(1, d), ln_b.reshape(1, d), rwt, rb, tri]
    row_map = lambda b, s: (b * spb + s, 0)
    x_spec = (pl.BlockSpec((tm, d), row_map) if x.shape[1] == d
              else pl.BlockSpec((tm * ROW_TILES, LANES), row_map))
    in_specs = ([x_spec]
                + [pl.BlockSpec(a.shape, const2) for a in mixer_args]
                + [pl.BlockSpec((1, 1) + kbd.shape[2:], lambda b, s: (layer, b, 0, 0)),
                   pl.BlockSpec((1, 1) + vbd.shape[2:], lambda b, s: (layer, b, 0, 0))]
                + [pl.BlockSpec(a.shape, const2) for a in tail_args[2:]])
    tok_spec = lambda rows: pl.BlockSpec((rows, tm), lambda b, s: (0, b * spb + s))
    return pl.pallas_call(
        kernel_fn,
        grid=(nb, spb),
        in_specs=in_specs,
        out_specs=[pl.BlockSpec((tm * ROW_TILES, LANES), row_map),
                   tok_spec(SUBLANES), tok_spec(SUBLANES),
                   pl.BlockSpec((CLASS_PAD, LANES), const2)],
        out_shape=[jax.ShapeDtypeStruct((n_tok * ROW_TILES, LANES), F32),
                   jax.ShapeDtypeStruct((SUBLANES, n_tok), F32),
                   jax.ShapeDtypeStruct((SUBLANES, n_tok), I32),
                   jax.ShapeDtypeStruct((CLASS_PAD, LANES), F32)],
        scratch_shapes=[pltpu.VMEM((CLASS_PAD, LANES), F32)] + scratch,
        compiler_params=pltpu.CompilerParams(
            dimension_semantics=("arbitrary", "arbitrary"), vmem_limit_bytes=VMEM_LIMIT),
        name=name,
    )(x, *mixer_args, *tail_args)


def _moe_kernel(layer, n_blocks,
                order_ref, off_ref, nvalid_ref, trash_ref, e1_ref, e2_ref, nblk_ref,
                x_hbm, g_hbm, wg_hbm, wu_hbm, wd_hbm, lng_ref, lnb_ref, out_hbm,
                xbuf, gbuf, obuf, zbuf, sg, su, sd, wg_bf, wu_bf, wd_bf,
                gsem, ssem, zsem, wsem):
    nblk = nblk_ref[0]
    last = nblk - 1
    ring = lambda j: (j + RING) % RING
    n_res = W_SETS * EXPERTS_PER_GROUP

    def gate_copy(blk, slt):
        return pltpu.make_async_copy(g_hbm.at[pl.ds(off_ref[blk + 1], MOE_ROWS)], gbuf.at[slt],
                                     gsem.at[slt])

    def tile_row(r):
        return r * ROW_TILES if isinstance(r, int) else pl.multiple_of(r * ROW_TILES, ROW_TILES)

    def gather_row(base, r, slt):
        t8 = pl.multiple_of(order_ref[base + r], ROW_TILES)
        pltpu.make_async_copy(x_hbm.at[pl.ds(t8, ROW_TILES)],
                              xbuf.at[slt, pl.ds(tile_row(r), ROW_TILES)], gsem.at[slt]).start()

    def issue_gather(blk, slt, unrolled=True):
        base = off_ref[blk + 1]
        if unrolled:
            for r in range(MOE_ROWS):
                gather_row(base, r, slt)
        else:
            lax.fori_loop(0, MOE_ROWS, lambda r, c: (gather_row(base, r, slt), c)[1], 0)
        gate_copy(blk, slt).start()

    def wait_gather(slt):
        pltpu.make_async_copy(x_hbm.at[pl.ds(0, MOE_ROWS * ROW_TILES)], xbuf.at[slt],
                              gsem.at[slt]).wait()
        gate_copy(0, slt).wait()

    def scatter_row(base, nvalid, trash8, r, slt):
        t8 = jnp.where(r < nvalid, order_ref[base + r], trash8 + r * ROW_TILES)
        pltpu.make_async_copy(
            obuf.at[slt, pl.ds(tile_row(r), ROW_TILES)],
            out_hbm.at[pl.ds(pl.multiple_of(t8, ROW_TILES), ROW_TILES)], ssem.at[slt]).start(
                priority=1)

    def issue_scatter(blk, slt, unrolled=True):
        args = (off_ref[blk + 1], nvalid_ref[blk + 1], trash_ref[blk + 1])
        if unrolled:
            for r in range(MOE_ROWS):
                scatter_row(*args, r, slt)
        else:
            lax.fori_loop(0, MOE_ROWS, lambda r, c: (scatter_row(*args, r, slt), c)[1], 0)

    def wait_scatter(slt):
        pltpu.make_async_copy(obuf.at[slt], out_hbm.at[pl.ds(0, MOE_ROWS * ROW_TILES)],
                              ssem.at[slt]).wait()

    def weight_copies(k, stage):
        return [pltpu.make_async_copy(src.at[layer, k], dst.at[stage], wsem.at[stage])
                for src, dst in ((wg_hbm, sg), (wu_hbm, su), (wd_hbm, sd))]

    def load_start(k):
        for cp in weight_copies(k, k % W_STAGE):
            cp.start(priority=1)

    def load_wait(k):
        for cp in weight_copies(k, k % W_STAGE):
            cp.wait()

    def load_finish(k):
        load_wait(k)
        stage, res = k % W_STAGE, k % n_res
        wg_bf[res] = sg[stage].astype(BF16)
        wu_bf[res] = su[stage].astype(BF16)
        wd_bf[res] = sd[stage].astype(BF16)

    def start_if(pred, kd):
        @pl.when(pred)
        def _():
            load_start(kd)
        return kd + pred.astype(I32)

    def service_loader(group, kd, kc):
        cap = jnp.minimum((group + W_SETS) * EXPERTS_PER_GROUP, N_EXPERTS)
        need = (group + 1) * EXPERTS_PER_GROUP

        def catch_up(c):
            kd, kc = c
            kd = start_if(kd <= kc, kd)
            kd = start_if((kd < kc + W_STAGE) & (kd < cap), kd)
            load_finish(kc)
            return kd, kc + 1
        kd, kc = lax.while_loop(lambda c: c[1] < need, catch_up, (kd, kc))

        ready = (kc < kd) & ((kd == kc + W_STAGE) | (kd >= cap))

        @pl.when(ready)
        def _():
            load_finish(kc)
        kc = kc + ready.astype(I32)
        kd = start_if((kd < kc + W_STAGE) & (kd < cap), kd)
        return kd, kc

    issue_gather(0, 0, unrolled=False)
    issue_gather(jnp.minimum(1, last), 1, unrolled=False)
    obuf[RING - 1] = jnp.zeros(obuf.shape[1:], F32)

    def block(b, loader):
        slot = b % RING
        e1, e2 = e1_ref[b], e2_ref[b]
        loader = service_loader(e1 // EXPERTS_PER_GROUP, *loader)

        @pl.when(b >= RING - 1)
        def _():
            wait_scatter(slot)

        wait_gather(slot)
        x = _load_rows(xbuf.at[slot], MOE_ROWS)
        xb = x.astype(BF16)
        issue_gather(jnp.minimum(b + 2, last), ring(b + 2))
        issue_scatter(b - 1, ring(b - 1))
        res = (e1 % n_res, e2 % n_res)
        hg = [_dot(xb, wg_bf[r]) for r in res]
        hu = [_dot(xb, wu_bf[r]) for r in res]
        act = [(g * jax.nn.sigmoid(g) * u).astype(BF16) for g, u in zip(hg, hu)]
        y = [_dot(a, wd_bf[r]) for a, r in zip(act, res)]
        gates = gbuf[slot]
        m = gates[:, 0:1] * y[0] + gates[:, 1:2] * y[1]
        o = _layer_norm(ALPHA * x + m, lng_ref[...], lnb_ref[...])
        _store_rows(obuf.at[slot], o)
        return loader

    first = (e1_ref[0] // EXPERTS_PER_GROUP) * EXPERTS_PER_GROUP
    kd, kc = lax.fori_loop(0, nblk, block, (first, first))

    lax.fori_loop(kc, kd, lambda k, c: (load_wait(k), c)[1], 0)
    issue_scatter(last, ring(last), unrolled=False)
    for slt in range(RING):
        wait_scatter(slt)
    wait_gather(ring(nblk))
    wait_gather(ring(nblk + 1))

    zbuf[...] = jnp.zeros_like(zbuf)

    def fill_copy(j):
        return pltpu.make_async_copy(
            zbuf, out_hbm.at[pl.ds(j * MOE_ROWS * ROW_TILES, MOE_ROWS * ROW_TILES)], zsem)
    lax.fori_loop(nblk, n_blocks, lambda j, c: (fill_copy(j).start(), c)[1], 0)
    lax.fori_loop(nblk, n_blocks, lambda j, c: (fill_copy(j).wait(), c)[1], 0)


def _moe_half(layer, x1, plan, gates_sorted, w_gate, w_up, w_down, ln_g, ln_b):
    n_blocks = plan[4].shape[0]
    n_rows = (n_blocks + 1) * MOE_ROWS
    d, f = D_MODEL, D_FF
    n_res = W_SETS * EXPERTS_PER_GROUP
    const_spec = pl.BlockSpec((1, d), lambda i, *_: (0, 0))
    any_spec = pl.BlockSpec(memory_space=pl.ANY)
    grid_spec = pltpu.PrefetchScalarGridSpec(
        num_scalar_prefetch=7,
        grid=(1,),
        in_specs=[any_spec, any_spec, any_spec, any_spec, any_spec, const_spec, const_spec],
        out_specs=any_spec,
        scratch_shapes=[
            pltpu.VMEM((RING, MOE_ROWS * ROW_TILES, LANES), F32),
            pltpu.VMEM((RING, MOE_ROWS, LANES), F32),
            pltpu.VMEM((RING, MOE_ROWS * ROW_TILES, LANES), F32),
            pltpu.VMEM((MOE_ROWS * ROW_TILES, LANES), F32),
            pltpu.VMEM((W_STAGE, d, f), F32), pltpu.VMEM((W_STAGE, d, f), F32),
            pltpu.VMEM((W_STAGE, f, d), F32),
            pltpu.VMEM((n_res, d, f), BF16), pltpu.VMEM((n_res, d, f), BF16),
            pltpu.VMEM((n_res, f, d), BF16),
            pltpu.SemaphoreType.DMA((RING,)),
            pltpu.SemaphoreType.DMA((RING,)),
            pltpu.SemaphoreType.DMA(()),
            pltpu.SemaphoreType.DMA((W_STAGE,)),
        ],
    )
    return pl.pallas_call(
        functools.partial(_moe_kernel, layer, n_blocks),
        grid_spec=grid_spec,
        out_shape=jax.ShapeDtypeStruct((n_rows * ROW_TILES, LANES), F32),
        compiler_params=pltpu.CompilerParams(
            dimension_semantics=("arbitrary",), vmem_limit_bytes=VMEM_LIMIT),
        name="moe_half",
    )(*plan, x1, gates_sorted, w_gate, w_up, w_down, ln_g.reshape(1, d), ln_b.reshape(1, d))


def _plan_blocks(route_f, route_i, counts_f, n_tok):
    cls, rank = route_i[0], route_i[1]
    counts = counts_f[:N_CLASSES, 0].astype(I32)
    padded = (counts + MOE_ROWS - 1) // MOE_ROWS * MOE_ROWS
    pad_end = jnp.cumsum(padded)
    pad_start = pad_end - padded
    start = jnp.cumsum(counts) - counts
    class_ids = jnp.arange(N_CLASSES, dtype=I32)
    dest = jnp.sum(jnp.where(cls[:, None] == class_ids[None], pad_start[None], 0), axis=1) + rank
    _, order, g_lo, g_hi = lax.sort(
        (dest, jnp.arange(n_tok, dtype=I32), route_f[0], route_f[1]), num_keys=1)
    order = jnp.pad(order, (0, MOE_ROWS)) * ROW_TILES
    gates_sorted = jnp.pad(jnp.stack([g_lo, g_hi], axis=1), ((0, MOE_ROWS), (0, LANES - 2)))

    n_blocks = n_tok // MOE_ROWS + N_CLASSES
    blk_ids = jnp.arange(n_blocks, dtype=I32)
    blk_start = blk_ids * MOE_ROWS
    nblk = pad_end[-1] // MOE_ROWS
    blk_cls = jnp.minimum(jnp.sum(pad_end[None] <= blk_start[:, None], axis=1), N_CLASSES - 1)
    blk_cls = jnp.where(blk_ids < nblk, blk_cls, blk_cls[jnp.maximum(nblk - 1, 0)]).astype(I32)
    in_cls = blk_start - pad_start[blk_cls]
    active = blk_ids < nblk
    blk_off = jnp.where(active, start[blk_cls] + in_cls, 0).astype(I32)
    nvalid = jnp.where(active, jnp.clip(counts[blk_cls] - in_cls, 0, MOE_ROWS), 0).astype(I32)
    trash = ((n_tok + blk_start - blk_off - nvalid) * ROW_TILES).astype(I32)
    pair_lo = jnp.array(PAIR_LO, I32)
    pair_hi = jnp.array(PAIR_HI, I32)
    blk_e1 = (blk_cls // N_PAIRS) * EXPERTS_PER_GROUP + pair_lo[blk_cls % N_PAIRS]
    blk_e2 = (blk_cls // N_PAIRS) * EXPERTS_PER_GROUP + pair_hi[blk_cls % N_PAIRS]
    lead = lambda a, v: jnp.concatenate([jnp.full((1,), v, I32), a])
    plan = (order, lead(blk_off, 0), lead(nvalid, 0),
            lead(trash, n_blocks * MOE_ROWS * ROW_TILES),
            blk_e1, blk_e2, nblk.astype(I32).reshape(1))
    return plan, gates_sorted


def _untile_kernel(x_ref, o_ref):
    o_ref[...] = _load_rows(x_ref, o_ref.shape[0])


def _token_rows_to_matrix(xf, n_tok, tm):
    return pl.pallas_call(
        _untile_kernel,
        grid=(n_tok // tm,),
        in_specs=[pl.BlockSpec((tm * ROW_TILES, LANES), lambda i: (i, 0))],
        out_specs=pl.BlockSpec((tm, D_MODEL), lambda i: (i, 0)),
        out_shape=jax.ShapeDtypeStruct((n_tok, D_MODEL), F32),
        compiler_params=pltpu.CompilerParams(
            dimension_semantics=("arbitrary",), vmem_limit_bytes=VMEM_LIMIT),
        name="token_rows_to_matrix",
    )(xf)


def kernel(x, mem, conv_w_in, conv_w, mlstm_w_in, mlstm_gate_b, mlstm_norm_g, w_kv_mem, w_out,
           ln1_g, ln1_b, ln2_g, ln2_b, router_w, router_b, w_gate, w_up, w_down):
    nb, seq, d = x.shape
    n_tok = nb * seq
    assert d == D_MODEL and seq % TM_CONV == 0 and seq % TM_MLSTM == 0 and TM_MLSTM % CHUNK == 0
    assert n_tok // MOE_ROWS >= RING
    assert n_tok % TM_UNTILE == 0

    kbd, vbd = _memory_kv(mem, w_kv_mem)

    perm = lambda a: jnp.swapaxes(a.reshape(N_GROUPS, EXPERTS_PER_GROUP, -1), 0, 1)
    rwt = perm(router_w.T).reshape(N_EXPERTS, d).astype(BF16)
    rb = perm(router_b).reshape(N_EXPERTS, 1)

    xf = x.reshape(n_tok, d)
    for i in range(DEPTH):
        j = i // 2
        wout = w_out[i].astype(BF16)
        if i % 2 == 0:
            win = conv_w_in[j].astype(BF16)
            outs = _mixer_half(_conv_half_kernel, TM_CONV, xf, nb, seq, [win, conv_w[j]],
                               i, kbd, vbd, wout, ln1_g[i], ln1_b[i], rwt, rb,
                               [pltpu.VMEM((8, MIX_W), F32)], "conv_half")
        else:
            w = mlstm_w_in[j]
            o0 = 4 * MIX_W
            w_gates = jnp.pad(w[:, o0:o0 + 2 * ML_HEADS], ((0, 0), (0, GATE_PAD - 2 * ML_HEADS)))
            win = jnp.concatenate([w[:, :3 * MIX_W], w_gates, w[:, 3 * MIX_W:o0],
                                   w[:, o0 + 2 * ML_HEADS:]], axis=1).astype(BF16)
            gate_b = jnp.pad(mlstm_gate_b[j], (0, GATE_PAD - 2 * ML_HEADS)).reshape(1, GATE_PAD)
            norm_g = mlstm_norm_g[j].reshape(1, MIX_W)
            outs = _mixer_half(_mlstm_half_kernel, TM_MLSTM, xf, nb, seq, [win, gate_b, norm_g],
                               i, kbd, vbd, wout, ln1_g[i], ln1_b[i], rwt, rb,
                               [pltpu.VMEM((ML_HEADS, ML_DH, 2 * ML_DH), F32),
                                pltpu.VMEM((1, LANES), F32),
                                pltpu.VMEM((TM_MLSTM, MIX_W), F32)], "mlstm_half")
        x1, route_f, route_i, counts_f = outs
        plan, gates_sorted = _plan_blocks(route_f, route_i, counts_f, n_tok)
        xf = _moe_half(i, x1, plan, gates_sorted, w_gate, w_up, w_down, ln2_g[i], ln2_b[i])
    return _token_rows_to_matrix(xf, n_tok, TM_UNTILE).reshape(nb, seq, d)
```
